```python
import math
import jax, jax.numpy as jnp
from jax import lax
import numpy as np

D_MODEL = 2048
BATCH = 1
SEQ = 16384
DEPTH = 1

CHUNK = 64
SSD_HEAD_DIM = 64
SSD_INNER = D_MODEL
SSD_HEADS = SSD_INNER // SSD_HEAD_DIM
SSD_GROUPS = 8
SSD_HEADS_PER_GROUP = SSD_HEADS // SSD_GROUPS
SSD_STATE = 128
CONV_WIDTH = 4
XBC_DIM = SSD_INNER + 2 * SSD_GROUPS * SSD_STATE
S5_WIDTH = D_MODEL // 2
S5_GROUP_CH = 16
S5_GROUPS = S5_WIDTH // S5_GROUP_CH
S5_STATE = 64
N_EXPERT_GROUPS = 4
EXPERTS_PER_GROUP = 8
N_EXPERTS = N_EXPERT_GROUPS * EXPERTS_PER_GROUP
TOP_K_INNER = 2
EXPERT_FF = D_MODEL // 4
MOE_BLOCK = 128
IN_SIZES = (SSD_INNER, XBC_DIM, SSD_HEADS, S5_WIDTH, 2 * D_MODEL)
IN_DIM = sum(IN_SIZES)
IN_SPLITS = tuple(int(s) for s in np.cumsum(IN_SIZES)[:-1])
RMS_EPS = 1e-6

kernel_name = "hybrid_ssd_s5_hmoe_block"


def rmsnorm(v, w):
    vf = v.astype(jnp.float32)
    vf = vf * lax.rsqrt(jnp.mean(vf * vf, axis=-1, keepdims=True) + RMS_EPS)
    return vf.astype(v.dtype) * w


def causal_depthwise_conv(v, w, bias):
    seq = v.shape[1]
    vp = jnp.pad(v, ((0, 0), (CONV_WIDTH - 1, 0), (0, 0)))
    out = bias
    for k in range(CONV_WIDTH):
        out = out + vp[:, k:k + seq] * w[k]
    return out


def ssd_chunked(xh, dt, a_log, bmat, cmat, d_skip):
    b, seq = xh.shape[0], xh.shape[1]
    nc = seq // CHUNK
    G, R, P, N = SSD_GROUPS, SSD_HEADS_PER_GROUP, SSD_HEAD_DIM, SSD_STATE
    a = -jnp.exp(a_log.astype(jnp.float32))
    a_dt = (dt * a).reshape(b, nc, CHUNK, G, R).transpose(0, 3, 4, 1, 2)
    xdt = (xh * dt[..., None]).reshape(b, nc, CHUNK, G, R, P)
    bc = bmat.reshape(b, nc, CHUNK, G, N)
    cc = cmat.reshape(b, nc, CHUNK, G, N)
    a_cs = jnp.cumsum(a_dt, axis=-1)
    causal = jnp.tril(jnp.ones((CHUNK, CHUNK), dtype=bool))
    seg = a_cs[..., :, None] - a_cs[..., None, :]
    decay_in = jnp.where(causal, jnp.exp(jnp.where(causal, seg, 0.0)), 0.0)
    cb = jnp.einsum('bclgn,bcsgn->bgcls', cc, bc)
    y_diag = jnp.einsum('bgcls,bgrcls,bcsgrp->bclgrp', cb, decay_in, xdt)
    decay_states = jnp.exp(a_cs[..., -1:] - a_cs)
    states = jnp.einsum('bclgn,bgrcl,bclgrp->bcgrpn', bc, decay_states, xdt)
    chunk_decay = jnp.exp(a_cs[..., -1])

    def step(carry, inp):
        st, dec = inp
        return carry * dec[..., None, None] + st, carry

    init = jnp.zeros((b, G, R, P, N), states.dtype)
    _, prev = lax.scan(step, init, (jnp.moveaxis(states, 1, 0), jnp.moveaxis(chunk_decay, -1, 0)))
    prev = jnp.moveaxis(prev, 0, 1)
    y_off = jnp.einsum('bclgn,bcgrpn,bgrcl->bclgrp', cc, prev, jnp.exp(a_cs))
    y = (y_diag + y_off).reshape(b, seq, SSD_HEADS, P) + xh * d_skip[:, None]
    return y.reshape(b, seq, SSD_HEADS * P)


def s5_combine(e1, e2):
    a1r, a1i, b1r, b1i = e1
    a2r, a2i, b2r, b2i = e2
    return (a2r * a1r - a2i * a1i,
            a2r * a1i + a2i * a1r,
            a2r * b1r - a2i * b1i + b2r,
            a2r * b1i + a2i * b1r + b2i)


def s5_mixer(u, lam_re, lam_im, log_dt, b_re, b_im, c_re, c_im, d_skip):
    b, seq, w = u.shape
    ug = u.reshape(b, seq, S5_GROUPS, S5_GROUP_CH)
    f32 = jnp.float32
    lr, li = lam_re.astype(f32), lam_im.astype(f32)
    dt = jnp.exp(log_dt.astype(f32))[:, None]
    mag = jnp.exp(lr * dt)
    ang = li * dt
    abar_r, abar_i = mag * jnp.cos(ang), mag * jnp.sin(ang)
    den = lr * lr + li * li
    nr, ni = abar_r - 1.0, abar_i
    coef_r = (nr * lr + ni * li) / den
    coef_i = (ni * lr - nr * li) / den
    bre, bim = b_re.astype(f32), b_im.astype(f32)
    bb_r = coef_r[..., None] * bre - coef_i[..., None] * bim
    bb_i = coef_r[..., None] * bim + coef_i[..., None] * bre
    bu_r = jnp.einsum('blgc,gnc->blgn', ug, bb_r)
    bu_i = jnp.einsum('blgc,gnc->blgn', ug, bb_i)
    a_r = jnp.broadcast_to(abar_r, bu_r.shape)
    a_i = jnp.broadcast_to(abar_i, bu_i.shape)
    _, _, s_r, s_i = lax.associative_scan(s5_combine, (a_r, a_i, bu_r, bu_i), axis=1)
    y = (jnp.einsum('blgn,gcn->blgc', s_r, c_re.astype(f32))
         - jnp.einsum('blgn,gcn->blgc', s_i, c_im.astype(f32)))
    return y.reshape(b, seq, w) + d_skip * u


def hierarchical_moe(h, w_rg, b_rg, w_re, b_re, w_g, w_u, w_d):
    b, seq, d = h.shape
    T = b * seq
    ht = h.reshape(T, d)
    f32 = jnp.float32
    logit_g = (ht @ w_rg).astype(f32) + b_rg.astype(f32)
    p_g = jax.nn.softmax(logit_g, axis=-1)
    grp = jnp.argmax(logit_g, axis=-1).astype(jnp.int32)
    pg_sel = jnp.take_along_axis(p_g, grp[:, None], axis=-1)
    logit_e = ((ht @ w_re).astype(f32) + b_re.astype(f32)).reshape(T, N_EXPERT_GROUPS, EXPERTS_PER_GROUP)
    logit_e = jnp.take_along_axis(logit_e, grp[:, None, None], axis=1)[:, 0]
    top_v, top_i = lax.top_k(logit_e, TOP_K_INNER)
    w_sel = jax.nn.softmax(top_v, axis=-1) * pg_sel
    expert_ids = (grp[:, None] * EXPERTS_PER_GROUP + top_i).reshape(-1).astype(jnp.int32)
    weights = w_sel.reshape(-1)
    token_idx = jnp.repeat(jnp.arange(T, dtype=jnp.int32), TOP_K_INNER)
    n_assign = T * TOP_K_INNER
    order = jnp.argsort(expert_ids)
    e_sorted = expert_ids[order]
    counts = jnp.bincount(expert_ids, length=N_EXPERTS)
    padded = ((counts + MOE_BLOCK - 1) // MOE_BLOCK) * MOE_BLOCK
    start_sorted = jnp.cumsum(counts) - counts
    ends_padded = jnp.cumsum(padded)
    start_padded = ends_padded - padded
    rank = jnp.arange(n_assign, dtype=jnp.int32) - start_sorted[e_sorted]
    dest = start_padded[e_sorted] + rank
    n_rows = n_assign + N_EXPERTS * MOE_BLOCK
    n_blocks = n_rows // MOE_BLOCK
    row_token = jnp.full((n_rows,), T, jnp.int32).at[dest].set(token_idx[order])
    row_weight = jnp.zeros((n_rows,), weights.dtype).at[dest].set(weights[order])
    block_start = jnp.arange(n_blocks, dtype=jnp.int32) * MOE_BLOCK
    block_expert = jnp.clip(jnp.searchsorted(ends_padded, block_start, side='right'), 0, N_EXPERTS - 1)
    ht_pad = jnp.concatenate([ht, jnp.zeros((1, d), ht.dtype)], axis=0)

    def run_block(args):
        tok, wgt, e = args
        xb = ht_pad[tok]
        y = (jax.nn.silu(xb @ w_g[e]) * (xb @ w_u[e])) @ w_d[e]
        return y * wgt[:, None].astype(y.dtype)

    ys = lax.map(run_block, (row_token.reshape(n_blocks, MOE_BLOCK),
                             row_weight.reshape(n_blocks, MOE_BLOCK), block_expert))
    out = jnp.zeros((T + 1, d), ys.dtype).at[row_token].add(ys.reshape(n_rows, d))[:T]
    return out.reshape(b, seq, d).astype(h.dtype)


def setup_inputs(seed: int = 0) -> dict:
    key = jax.random.key(seed)
    ks = jax.random.split(key, 32)
    f32 = jnp.float32
    L = DEPTH

    def nrm(k, shape, scale):
        return jax.random.normal(k, shape, f32) * scale

    def gain(k, shape):
        return 1.0 + 0.02 * jax.random.normal(k, shape, f32)

    dt0 = jnp.exp(jax.random.uniform(ks[5], (L, SSD_HEADS), f32, math.log(1e-3), math.log(1e-1)))
    return {
        "x": nrm(ks[0], (BATCH, SEQ, D_MODEL), 1.0),
        "norm_mix_w": gain(ks[1], (L, D_MODEL)),
        "w_in": nrm(ks[2], (L, D_MODEL, IN_DIM), D_MODEL ** -0.5),
        "conv_w": nrm(ks[3], (L, CONV_WIDTH, XBC_DIM), CONV_WIDTH ** -0.5),
        "conv_b": nrm(ks[4], (L, XBC_DIM), 0.02),
        "dt_bias": dt0 + jnp.log(-jnp.expm1(-dt0)),
        "a_log": jnp.log(jax.random.uniform(ks[6], (L, SSD_HEADS), f32, 1.0, 16.0)),
        "d_ssd": gain(ks[7], (L, SSD_HEADS)),
        "norm_ssd_w": gain(ks[8], (L, SSD_INNER)),
        "w_a_up": nrm(ks[9], (L, SSD_INNER, D_MODEL), SSD_INNER ** -0.5),
        "s5_lambda_re": -0.5 + nrm(ks[10], (L, S5_GROUPS, S5_STATE), 0.01),
        "s5_lambda_im": math.pi * jnp.arange(S5_STATE, dtype=f32) + nrm(ks[11], (L, S5_GROUPS, S5_STATE), 0.01),
        "s5_log_dt": jax.random.uniform(ks[12], (L, S5_GROUPS), f32, math.log(1e-3), math.log(1e-1)),
        "s5_b_re": nrm(ks[13], (L, S5_GROUPS, S5_STATE, S5_GROUP_CH), (2.0 * S5_GROUP_CH) ** -0.5),
        "s5_b_im": nrm(ks[14], (L, S5_GROUPS, S5_STATE, S5_GROUP_CH), (2.0 * S5_GROUP_CH) ** -0.5),
        "s5_c_re": nrm(ks[15], (L, S5_GROUPS, S5_GROUP_CH, S5_STATE), (2.0 * S5_STATE) ** -0.5),
        "s5_c_im": nrm(ks[16], (L, S5_GROUPS, S5_GROUP_CH, S5_STATE), (2.0 * S5_STATE) ** -0.5),
        "s5_d": nrm(ks[17], (L, S5_WIDTH), 1.0),
        "w_glu": nrm(ks[18], (L, S5_WIDTH, S5_WIDTH), S5_WIDTH ** -0.5),
        "w_b_up": nrm(ks[19], (L, S5_WIDTH, D_MODEL), S5_WIDTH ** -0.5),
        "gate_b": nrm(ks[20], (L, 2 * D_MODEL), 0.02),
        "w_out": nrm(ks[21], (L, D_MODEL, D_MODEL), D_MODEL ** -0.5),
        "norm_ffn_w": gain(ks[22], (L, D_MODEL)),
        "w_route_group": nrm(ks[23], (L, D_MODEL, N_EXPERT_GROUPS), D_MODEL ** -0.5),
        "b_route_group": nrm(ks[24], (L, N_EXPERT_GROUPS), 0.01),
        "w_route_expert": nrm(ks[25], (L, D_MODEL, N_EXPERTS), D_MODEL ** -0.5),
        "b_route_expert": nrm(ks[26], (L, N_EXPERTS), 0.01),
        "w_exp_gate": nrm(ks[27], (L, N_EXPERTS, D_MODEL, EXPERT_FF), D_MODEL ** -0.5),
        "w_exp_up": nrm(ks[28], (L, N_EXPERTS, D_MODEL, EXPERT_FF), D_MODEL ** -0.5),
        "w_exp_down": nrm(ks[29], (L, N_EXPERTS, EXPERT_FF, D_MODEL), EXPERT_FF ** -0.5),
        "norm_final_w": gain(ks[30], (D_MODEL,)),
    }


def reference(x, norm_mix_w, w_in, conv_w, conv_b, dt_bias, a_log, d_ssd, norm_ssd_w, w_a_up,
              s5_lambda_re, s5_lambda_im, s5_log_dt, s5_b_re, s5_b_im, s5_c_re, s5_c_im, s5_d,
              w_glu, w_b_up, gate_b, w_out, norm_ffn_w, w_route_group, b_route_group,
              w_route_expert, b_route_expert, w_exp_gate, w_exp_up, w_exp_down, norm_final_w):
    b, seq, _ = x.shape
    for i in range(DEPTH):
        h = rmsnorm(x, norm_mix_w[i])
        proj = h @ w_in[i]
        z, xbc, dt_raw, u, gate_logits = jnp.split(proj, IN_SPLITS, axis=-1)
        xbc = jax.nn.silu(causal_depthwise_conv(xbc, conv_w[i], conv_b[i]))
        xs, bm, cm = jnp.split(xbc, (SSD_INNER, SSD_INNER + SSD_GROUPS * SSD_STATE), axis=-1)
        dt = jax.nn.softplus(dt_raw.astype(jnp.float32) + dt_bias[i].astype(jnp.float32))
        ya = ssd_chunked(xs.reshape(b, seq, SSD_HEADS, SSD_HEAD_DIM), dt, a_log[i],
                         bm.reshape(b, seq, SSD_GROUPS, SSD_STATE),
                         cm.reshape(b, seq, SSD_GROUPS, SSD_STATE), d_ssd[i])
        ya = rmsnorm(ya.astype(x.dtype) * jax.nn.silu(z), norm_ssd_w[i]) @ w_a_up[i]
        yb = s5_mixer(u, s5_lambda_re[i], s5_lambda_im[i], s5_log_dt[i], s5_b_re[i], s5_b_im[i],
                      s5_c_re[i], s5_c_im[i], s5_d[i]).astype(x.dtype)
        v = jax.nn.gelu(yb)
        yb = (v * jax.nn.sigmoid(v @ w_glu[i])) @ w_b_up[i]
        g = jax.nn.sigmoid(gate_logits + gate_b[i])
        merged = g[..., :D_MODEL] * ya + g[..., D_MODEL:] * yb
        x = x + merged @ w_out[i]
        h2 = rmsnorm(x, norm_ffn_w[i])
        x = x + hierarchical_moe(h2, w_route_group[i], b_route_group[i], w_route_expert[i],
                                 b_route_expert[i], w_exp_gate[i], w_exp_up[i], w_exp_down[i])
    return rmsnorm(x, norm_final_w)
```

```python
import functools
import math

import jax
import jax.numpy as jnp
from jax import lax
from jax.experimental import pallas as pl
from jax.experimental.pallas import tpu as pltpu

F32 = jnp.float32
BF16 = jnp.bfloat16

SSD_HEAD_DIM = 64
SSD_GROUPS = 8
SSD_STATE = 128
CONV_WIDTH = 4
S5_GROUP_CH = 16
S5_STATE = 64
N_EXPERT_GROUPS = 4
EXPERTS_PER_GROUP = 8
N_EXPERTS = N_EXPERT_GROUPS * EXPERTS_PER_GROUP
TOP_K_INNER = 2
RMS_EPS = 1e-6

LANES = 128
SUBLANES = 8
VMEM_LIMIT_BYTES = 52 * 1024 * 1024

ROW_TILE = 512
MM_ROW_TILE = 1024
MM_COL_TILE = 1024
SSD_CHUNK = 128
S5_CHUNK = 16
S5_SUPER = S5_GROUP_CH * 8
S5_ROWS = 256
MOE_BLOCK = 256
GATHER_TILE = 256


def _cparams(sem, vmem=VMEM_LIMIT_BYTES):
    return pltpu.CompilerParams(dimension_semantics=sem, vmem_limit_bytes=vmem)


def _sigmoid(v):
    return 1.0 / (1.0 + jnp.exp(-v))


def _silu(v):
    return v * _sigmoid(v)


def _softplus(v):
    return jnp.maximum(v, 0.0) + jnp.log(1.0 + jnp.exp(-jnp.abs(v)))


def _gelu_tanh(v):
    c = math.sqrt(2.0 / math.pi)
    return 0.5 * v * (1.0 + jnp.tanh(c * (v + 0.044715 * (v * v * v))))


def _rms(v, w):
    ms = jnp.mean(v * v, axis=-1, keepdims=True)
    return v * lax.rsqrt(ms + RMS_EPS) * w


def _rmsnorm_kernel(x_ref, w_ref, o_ref):
    o_ref[...] = _rms(x_ref[...], w_ref[...]).astype(o_ref.dtype)


def _rmsnorm(x, w, tm):
    t, d = x.shape
    return pl.pallas_call(
        _rmsnorm_kernel,
        grid=(t // tm,),
        in_specs=[pl.BlockSpec((tm, d), lambda i: (i, 0)),
                  pl.BlockSpec((1, d), lambda i: (0, 0))],
        out_specs=pl.BlockSpec((tm, d), lambda i: (i, 0)),
        out_shape=jax.ShapeDtypeStruct((t, d), BF16),
        compiler_params=_cparams(("parallel",)),
        name="rmsnorm",
    )(x, w.reshape(1, d))


def _proj_kernel(a_ref, w_ref, b_ref, o_ref, *, act):
    p = jnp.dot(a_ref[...], w_ref[...], preferred_element_type=F32)
    if act == "silu":
        p = _silu(p)
    elif act == "sigmoid_bias":
        p = _sigmoid(p + b_ref[...])
    o_ref[...] = p.astype(o_ref.dtype)


def _proj(h, w, b, act, tm, tn):
    t, k = h.shape
    n = w.shape[1]
    tn = min(tn, n)
    if b is None:
        b = jnp.zeros((1, n), F32)
    return pl.pallas_call(
        functools.partial(_proj_kernel, act=act),
        grid=(n // tn, t // tm),
        in_specs=[pl.BlockSpec((tm, k), lambda j, i: (i, 0)),
                  pl.BlockSpec((k, tn), lambda j, i: (0, j)),
                  pl.BlockSpec((1, tn), lambda j, i: (0, j))],
        out_specs=pl.BlockSpec((tm, tn), lambda j, i: (i, j)),
        out_shape=jax.ShapeDtypeStruct((t, n), BF16),
        compiler_params=_cparams(("parallel", "parallel")),
        name="proj_" + act,
    )(h, w, b.reshape(1, n))


def _conv_proj_kernel(a_ref, w_ref, cw_ref, cb_ref, o_ref, ext_ref):
    tm = a_ref.shape[0]
    halo = SUBLANES

    @pl.when(pl.program_id(1) == 0)
    def _():
        ext_ref[pl.ds(0, halo), :] = jnp.zeros((halo, ext_ref.shape[1]), F32)

    p = jnp.dot(a_ref[...], w_ref[...], preferred_element_type=F32)
    ext_ref[pl.ds(halo, tm), :] = p
    acc = cb_ref[...] + cw_ref[CONV_WIDTH - 1:CONV_WIDTH, :] * p
    for k in range(CONV_WIDTH - 1):
        back = CONV_WIDTH - 1 - k
        acc = acc + cw_ref[k:k + 1, :] * ext_ref[pl.ds(halo - back, tm), :]
    o_ref[...] = _silu(acc).astype(o_ref.dtype)
    ext_ref[pl.ds(0, halo), :] = p[tm - halo:, :]


def _conv_proj(h, w, conv_w, conv_b, tm, tn):
    t, k = h.shape
    n = w.shape[1]
    return pl.pallas_call(
        _conv_proj_kernel,
        grid=(n // tn, t // tm),
        in_specs=[pl.BlockSpec((tm, k), lambda j, i: (i, 0)),
                  pl.BlockSpec((k, tn), lambda j, i: (0, j)),
                  pl.BlockSpec((CONV_WIDTH, tn), lambda j, i: (0, j)),
                  pl.BlockSpec((1, tn), lambda j, i: (0, j))],
        out_specs=pl.BlockSpec((tm, tn), lambda j, i: (i, j)),
        out_shape=jax.ShapeDtypeStruct((t, n), BF16),
        scratch_shapes=[pltpu.VMEM((tm + SUBLANES, tn), F32)],
        compiler_params=_cparams(("parallel", "arbitrary")),
        name="proj_conv",
    )(h, w, conv_w, conv_b.reshape(1, n))


def _dt_proj_kernel(a_ref, w_ref, b_ref, dt_ref, dtt_ref):
    p = jnp.dot(a_ref[...], w_ref[...], preferred_element_type=F32) + b_ref[...]
    dt = _softplus(p)
    dt_ref[...] = dt
    dtt_ref[...] = dt.T


def _dt_proj(h, w_pad, b_pad, tm):
    t, k = h.shape
    return pl.pallas_call(
        _dt_proj_kernel,
        grid=(t // tm,),
        in_specs=[pl.BlockSpec((tm, k), lambda i: (i, 0)),
                  pl.BlockSpec((k, LANES), lambda i: (0, 0)),
                  pl.BlockSpec((1, LANES), lambda i: (0, 0))],
        out_specs=[pl.BlockSpec((tm, LANES), lambda i: (i, 0)),
                   pl.BlockSpec((LANES, tm), lambda i: (0, i))],
        out_shape=[jax.ShapeDtypeStruct((t, LANES), F32),
                   jax.ShapeDtypeStruct((LANES, t), F32)],
        compiler_params=_cparams(("parallel",)),
        name="proj_dt",
    )(h, w_pad, b_pad)


def _step_major_perm(n_rows):
    assert n_rows == S5_CHUNK * S5_CHUNK
    shift = S5_CHUNK.bit_length() - 1
    row = lax.broadcasted_iota(jnp.int32, (n_rows, n_rows), 0)
    col = lax.broadcasted_iota(jnp.int32, (n_rows, n_rows), 1)
    swapped = ((row & (S5_CHUNK - 1)) << shift) | (row >> shift)
    return jnp.where(col == swapped, 1.0, 0.0).astype(BF16)


def _chunk_proj_kernel(a_ref, w_ref, o_ref):
    u = jnp.dot(a_ref[...], w_ref[...], preferred_element_type=F32).astype(BF16)
    u_steps = jnp.dot(_step_major_perm(u.shape[0]), u, preferred_element_type=F32).astype(o_ref.dtype)
    nj = o_ref.shape[1]
    for s in range(o_ref.shape[0]):
        o_ref[s] = u_steps[s * nj:(s + 1) * nj, :]


def _chunk_proj(h, w):
    t, k = h.shape
    n = w.shape[1]
    nj = t // S5_CHUNK
    tm = S5_CHUNK * S5_CHUNK
    return pl.pallas_call(
        _chunk_proj_kernel,
        grid=(t // tm,),
        in_specs=[pl.BlockSpec((tm, k), lambda i: (i, 0)),
                  pl.BlockSpec((k, n), lambda i: (0, 0))],
        out_specs=pl.BlockSpec((S5_CHUNK, S5_CHUNK, n), lambda i: (0, i, 0)),
        out_shape=jax.ShapeDtypeStruct((S5_CHUNK, nj, n), BF16),
        compiler_params=_cparams(("parallel",)),
        name="proj_u",
    )(h, w)


def _cumsum_rows(v):
    n = v.shape[0]
    idx = lax.broadcasted_iota(jnp.int32, v.shape, 0)
    k = 1
    while k < n:
        v = v + jnp.where(idx >= k, pltpu.roll(v, k, 0), 0.0)
        k *= 2
    return v


def _cumsum_lanes(v):
    n = v.shape[1]
    idx = lax.broadcasted_iota(jnp.int32, v.shape, 1)
    k = 1
    while k < n:
        v = v + jnp.where(idx >= k, pltpu.roll(v, k, 1), 0.0)
        k *= 2
    return v


def _ssd_kernel(x_ref, b_ref, c_ref, dt_ref, dtt_ref, alog_r_ref, alog_c_ref, dskip_ref,
                o_ref, state_ref, csc_ref, dtc_ref):
    ci = pl.program_id(0)
    g = pl.program_id(1)
    q = x_ref.shape[0]
    r_heads = dtt_ref.shape[0]
    p_dim = SSD_HEAD_DIM

    @pl.when(ci == 0)
    def _():
        state_ref[g] = jnp.zeros(state_ref.shape[1:], F32)

    @pl.when(g == 0)
    def _():
        dt_all = dt_ref[...]
        a_all = -jnp.exp(alog_r_ref[...])
        csc_ref[...] = _cumsum_rows(dt_all * a_all)
        dtc_ref[...] = dt_all

    shift = (LANES - g * r_heads) % LANES
    cs_col = pltpu.roll(csc_ref[...], shift, 1)
    dt_col = pltpu.roll(dtc_ref[...], shift, 1)

    a_c = -jnp.exp(alog_c_ref[...])
    adt_row = dtt_ref[...] * a_c
    pad = jnp.zeros((SUBLANES - r_heads, q), F32)
    cs_row = _cumsum_lanes(jnp.concatenate([adt_row, pad], axis=0))

    x = x_ref[...].astype(F32)
    bm = b_ref[...]
    cm = c_ref[...]
    cb = lax.dot_general(cm, bm, (((1,), (1,)), ((), ())), preferred_element_type=F32)
    row = lax.broadcasted_iota(jnp.int32, (q, q), 0)
    col = lax.broadcasted_iota(jnp.int32, (q, q), 1)
    causal = row >= col

    s_prev = state_ref[g]
    y_off = jnp.dot(cm, s_prev.astype(BF16), preferred_element_type=F32)

    y_parts, xd_parts, cd_parts = [], [], []
    for r in range(r_heads):
        csc = cs_col[:, r:r + 1]
        csr = cs_row[r:r + 1, :]
        cs_last = csc[q - 1:q, :]
        seg = jnp.where(causal, csc - csr, -1e30)
        gmat = (cb * jnp.exp(seg)).astype(BF16)
        xdt = x[:, r * p_dim:(r + 1) * p_dim] * dt_col[:, r:r + 1]
        y_diag = jnp.dot(gmat, xdt.astype(BF16), preferred_element_type=F32)
        y_parts.append(y_diag + y_off[:, r * p_dim:(r + 1) * p_dim] * jnp.exp(csc))
        xd_parts.append((xdt * jnp.exp(cs_last - csc)).astype(BF16))
        cd_parts.append(jnp.broadcast_to(jnp.exp(cs_last), (1, p_dim)))
    xd = jnp.concatenate(xd_parts, axis=1)
    new_state = lax.dot_general(bm, xd, (((0,), (0,)), ((), ())), preferred_element_type=F32)
    state_ref[g] = s_prev * jnp.concatenate(cd_parts, axis=1) + new_state
    y = jnp.concatenate(y_parts, axis=1) + x * dskip_ref[...]
    o_ref[...] = y.astype(o_ref.dtype)


def _ssd(xbc, dt, dtt, a_log, d_skip, n_heads, q):
    t = xbc.shape[0]
    inner = n_heads * SSD_HEAD_DIM
    gw = inner // SSD_GROUPS
    r_heads = n_heads // SSD_GROUPS
    n = SSD_STATE
    b_off = inner // n
    c_off = b_off + SSD_GROUPS
    alog_r = jnp.zeros((1, LANES), F32).at[0, :n_heads].set(a_log)
    alog_c = a_log.reshape(SSD_GROUPS, r_heads, 1)
    dskip = jnp.repeat(d_skip, SSD_HEAD_DIM).reshape(SSD_GROUPS, 1, gw)
    dtt_g = dtt[:n_heads].reshape(SSD_GROUPS, r_heads, t)
    return pl.pallas_call(
        _ssd_kernel,
        grid=(t // q, SSD_GROUPS),
        in_specs=[pl.BlockSpec((q, gw), lambda c, g: (c, g)),
                  pl.BlockSpec((q, n), lambda c, g: (c, b_off + g)),
                  pl.BlockSpec((q, n), lambda c, g: (c, c_off + g)),
                  pl.BlockSpec((q, LANES), lambda c, g: (c, 0)),
                  pl.BlockSpec((None, r_heads, q), lambda c, g: (g, 0, c)),
                  pl.BlockSpec((1, LANES), lambda c, g: (0, 0)),
                  pl.BlockSpec((None, r_heads, 1), lambda c, g: (g, 0, 0)),
                  pl.BlockSpec((None, 1, gw), lambda c, g: (g, 0, 0))],
        out_specs=pl.BlockSpec((q, gw), lambda c, g: (c, g)),
        out_shape=jax.ShapeDtypeStruct((t, inner), BF16),
        scratch_shapes=[pltpu.VMEM((SSD_GROUPS, n, gw), F32),
                        pltpu.VMEM((q, LANES), F32),
                        pltpu.VMEM((q, LANES), F32)],
        compiler_params=_cparams(("arbitrary", "arbitrary")),
        name="ssd_scan",
    )(xbc, xbc, xbc, dt, dtt_g, alog_r, alog_c, dskip)


def _gated_up_kernel(y_ref, z_ref, nw_ref, w_ref, o_ref):
    v = y_ref[...].astype(F32) * z_ref[...].astype(F32)
    na = _rms(v, nw_ref[...]).astype(BF16)
    o_ref[...] = jnp.dot(na, w_ref[...], preferred_element_type=F32).astype(o_ref.dtype)


def _gated_up(y, zs, norm_w, w, tm):
    t, d = y.shape
    n = w.shape[1]
    return pl.pallas_call(
        _gated_up_kernel,
        grid=(t // tm,),
        in_specs=[pl.BlockSpec((tm, d), lambda i: (i, 0)),
                  pl.BlockSpec((tm, d), lambda i: (i, 0)),
                  pl.BlockSpec((1, d), lambda i: (0, 0)),
                  pl.BlockSpec((d, n), lambda i: (0, 0))],
        out_specs=pl.BlockSpec((tm, n), lambda i: (i, 0)),
        out_shape=jax.ShapeDtypeStruct((t, n), BF16),
        compiler_params=_cparams(("parallel",)),
        name="ssd_gated_up",
    )(y, zs, norm_w.reshape(1, d), w)


def _s5_operators(lam_re, lam_im, log_dt, b_re, b_im, c_re, c_im):
    ng, ns = lam_re.shape
    nc = S5_GROUP_CH
    L = S5_CHUNK
    per = S5_SUPER // nc
    nsg = ng // per
    lr, li = lam_re.astype(F32), lam_im.astype(F32)
    dt = jnp.exp(log_dt.astype(F32))[:, None]
    mag = jnp.exp(lr * dt)
    ang = li * dt
    abar_r, abar_i = mag * jnp.cos(ang), mag * jnp.sin(ang)
    den = lr * lr + li * li
    nr, ni = abar_r - 1.0, abar_i
    coef_r = (nr * lr + ni * li) / den
    coef_i = (ni * lr - nr * li) / den
    bre, bim = b_re.astype(F32), b_im.astype(F32)
    bb_r = coef_r[..., None] * bre - coef_i[..., None] * bim
    bb_i = coef_r[..., None] * bim + coef_i[..., None] * bre
    cre, cim = c_re.astype(F32), c_im.astype(F32)
    ks = jnp.arange(L + 1, dtype=F32)[:, None, None]
    pmag = jnp.exp(ks * (lr * dt)[None])
    pang = ks * ang[None]
    pw_r, pw_i = pmag * jnp.cos(pang), pmag * jnp.sin(pang)
    ca_r = cre[None] * pw_r[:, :, None, :] - cim[None] * pw_i[:, :, None, :]
    ca_i = cre[None] * pw_i[:, :, None, :] + cim[None] * pw_r[:, :, None, :]
    kk = (jnp.einsum('kgcn,gnd->kgcd', ca_r[:L], bb_r, precision=lax.Precision.HIGHEST)
          - jnp.einsum('kgcn,gnd->kgcd', ca_i[:L], bb_i, precision=lax.Precision.HIGHEST))
    eye = jnp.eye(per, dtype=F32)
    kk5 = kk.reshape(L, nsg, per, nc, nc)
    kbd = (kk5.transpose(1, 0, 2, 4, 3)[:, :, :, :, None, :]
           * eye[None, None, :, None, :, None]).reshape(nsg, L, per * nc, per * nc)
    ks_rev = (L - 1) - jnp.arange(L, dtype=F32)[:, None, None]
    rmag = jnp.exp(ks_rev * (lr * dt)[None])
    rang = ks_rev * ang[None]
    rev_r, rev_i = rmag * jnp.cos(rang), rmag * jnp.sin(rang)
    ab_r = rev_r[..., None] * bb_r[None] - rev_i[..., None] * bb_i[None]
    ab_i = rev_r[..., None] * bb_i[None] + rev_i[..., None] * bb_r[None]

    def _w_half(ab):
        ab6 = ab.reshape(L, nsg, per, ns, nc)
        return (ab6.transpose(1, 0, 2, 4, 3)[:, :, :, :, None, :]
                * eye[None, None, :, None, :, None]).reshape(nsg, L * per * nc, per * ns)

    wmat = jnp.concatenate([_w_half(ab_r), _w_half(ab_i)], axis=2)

    def _v_half(ca):
        ca6 = ca.reshape(L, nsg, per, nc, ns)
        return (ca6.transpose(1, 2, 4, 0, 3)[:, :, :, :, None, :]
                * eye[None, :, None, None, :, None]).reshape(nsg, per * ns, L * per * nc)

    vmat = jnp.concatenate([_v_half(ca_r[1:]), -_v_half(ca_i[1:])], axis=1)
    a_chunk = jnp.concatenate([pw_r[L].reshape(nsg, 1, per * ns),
                               pw_i[L].reshape(nsg, 1, per * ns)], axis=2)
    return kbd.astype(BF16), wmat.astype(BF16), vmat.astype(BF16), a_chunk


def _s5_kernel(u_ref, kbd_ref, w_ref, v_ref, ach_ref, dsk_ref, o_ref,
               toep_ref, x_ref, sp_ref, carry_ref):
    jb = pl.program_id(1)
    nl, nj, cw = u_ref.shape
    half = carry_ref.shape[1] // 2

    @pl.when(jb == 0)
    def _():
        carry_ref[...] = jnp.zeros(carry_ref.shape, F32)
        toep_ref[...] = jnp.zeros(toep_ref.shape, toep_ref.dtype)
        for s_in in range(nl):
            for s_out in range(s_in, nl):
                toep_ref[pl.ds(s_in * cw, cw), pl.ds(s_out * cw, cw)] = kbd_ref[s_out - s_in]

    lhs = jnp.concatenate([u_ref[s] for s in range(nl)], axis=1)
    x_ref[...] = jnp.dot(lhs, w_ref[...], preferred_element_type=F32)

    a_re = ach_ref[:, :half]
    a_im = ach_ref[:, half:]

    def step(j, carry):
        s_re, s_im = carry
        sp_ref[pl.ds(j, 1), :half] = s_re
        sp_ref[pl.ds(j, 1), half:] = s_im
        xr = x_ref[pl.ds(j, 1), :half]
        xi = x_ref[pl.ds(j, 1), half:]
        return (a_re * s_re - a_im * s_im + xr, a_re * s_im + a_im * s_re + xi)

    s_re, s_im = lax.fori_loop(0, nj, step, (carry_ref[:, :half], carry_ref[:, half:]), unroll=8)
    carry_ref[:, :half] = s_re
    carry_ref[:, half:] = s_im

    y_state = jnp.dot(sp_ref[...].astype(BF16), v_ref[...], preferred_element_type=F32)
    dsk = dsk_ref[...]
    pair = 2 * cw
    for tp in range(nl // 2):
        kdim = pair * (tp + 1)
        y = jnp.dot(lhs[:, :kdim], toep_ref[pl.ds(0, kdim), pl.ds(tp * pair, pair)],
                    preferred_element_type=F32)
        y = y + y_state[:, tp * pair:(tp + 1) * pair]
        for h in range(2):
            s = 2 * tp + h
            o_ref[s] = (y[:, h * cw:(h + 1) * cw] + dsk * u_ref[s].astype(F32)).astype(o_ref.dtype)


def _s5(u_steps, kbd, wmat, vmat, a_chunk, d_skip, tj):
    nl, nj, width = u_steps.shape
    nsg = width // S5_SUPER
    nstate = a_chunk.shape[2]
    return pl.pallas_call(
        _s5_kernel,
        grid=(nsg, nj // tj),
        in_specs=[pl.BlockSpec((nl, tj, S5_SUPER), lambda g, j: (0, j, g)),
                  pl.BlockSpec((None, nl, S5_SUPER, S5_SUPER), lambda g, j: (g, 0, 0, 0)),
                  pl.BlockSpec((None, nl * S5_SUPER, nstate), lambda g, j: (g, 0, 0)),
                  pl.BlockSpec((None, nstate, nl * S5_SUPER), lambda g, j: (g, 0, 0)),
                  pl.BlockSpec((None, 1, nstate), lambda g, j: (g, 0, 0)),
                  pl.BlockSpec((None, 1, S5_SUPER), lambda g, j: (g, 0, 0))],
        out_specs=pl.BlockSpec((nl, tj, S5_SUPER), lambda g, j: (0, j, g)),
        out_shape=jax.ShapeDtypeStruct((nl, nj, width), BF16),
        scratch_shapes=[pltpu.VMEM((nl * S5_SUPER, nl * S5_SUPER), BF16),
                        pltpu.VMEM((tj, nstate), F32),
                        pltpu.VMEM((tj, nstate), F32),
                        pltpu.VMEM((1, nstate), F32)],
        compiler_params=_cparams(("parallel", "arbitrary")),
        name="s5_scan",
    )(u_steps, kbd, wmat, vmat, a_chunk, d_skip.reshape(nsg, 1, S5_SUPER))


def _glu_up_kernel(y_ref, wg_ref, wu_ref, o_ref):
    y_steps = jnp.concatenate([y_ref[s] for s in range(y_ref.shape[0])], axis=0)
    y = jnp.dot(_step_major_perm(y_steps.shape[0]), y_steps, preferred_element_type=F32)
    v = _gelu_tanh(y)
    gate = _sigmoid(jnp.dot(v.astype(BF16), wg_ref[...], preferred_element_type=F32))
    o_ref[...] = jnp.dot((v * gate).astype(BF16), wu_ref[...],
                         preferred_element_type=F32).astype(o_ref.dtype)


def _glu_up(y_steps, w_glu, w_up):
    nl, nj, width = y_steps.shape
    n = w_up.shape[1]
    tm = S5_CHUNK * S5_CHUNK
    return pl.pallas_call(
        _glu_up_kernel,
        grid=(nj // S5_CHUNK,),
        in_specs=[pl.BlockSpec((nl, S5_CHUNK, width), lambda i: (0, i, 0)),
                  pl.BlockSpec((width, width), lambda i: (0, 0)),
                  pl.BlockSpec((width, n), lambda i: (0, 0))],
        out_specs=pl.BlockSpec((tm, n), lambda i: (i, 0)),
        out_shape=jax.ShapeDtypeStruct((nj * nl, n), BF16),
        compiler_params=_cparams(("parallel",)),
        name="s5_glu_up",
    )(y_steps, w_glu, w_up)


def _merge_out_kernel(ga_ref, gb_ref, ya_ref, yb_ref, x_ref, w_ref, nw_ref, rhi_ref, rlo_ref, rb_ref,
                      x1_ref, h2_ref, lg_ref):
    merged = (ga_ref[...].astype(F32) * ya_ref[...].astype(F32)
              + gb_ref[...].astype(F32) * yb_ref[...].astype(F32))
    x1 = x_ref[...] + jnp.dot(merged.astype(BF16), w_ref[...], preferred_element_type=F32)
    x1_ref[...] = x1
    h2 = _rms(x1, nw_ref[...])
    h2_ref[...] = h2
    hi = h2.astype(BF16)
    lo = (h2 - hi.astype(F32)).astype(BF16)
    lg = (jnp.dot(hi, rhi_ref[...], preferred_element_type=F32)
          + jnp.dot(lo, rhi_ref[...], preferred_element_type=F32)
          + jnp.dot(hi, rlo_ref[...], preferred_element_type=F32))
    lg_ref[...] = lg + rb_ref[...]


def _merge_out(gates, ya, yb, x, w_out, norm_w, r_hi, r_lo, r_b, tm):
    t, d = x.shape
    row = lambda i: (i, 0)
    full = lambda i: (0, 0)
    return pl.pallas_call(
        _merge_out_kernel,
        grid=(t // tm,),
        in_specs=[pl.BlockSpec((tm, d), row),
                  pl.BlockSpec((tm, d), lambda i: (i, 1)),
                  pl.BlockSpec((tm, d), row),
                  pl.BlockSpec((tm, d), row),
                  pl.BlockSpec((tm, d), row),
                  pl.BlockSpec((d, d), full),
                  pl.BlockSpec((1, d), full),
                  pl.BlockSpec((d, LANES), full),
                  pl.BlockSpec((d, LANES), full),
                  pl.BlockSpec((1, LANES), full)],
        out_specs=[pl.BlockSpec((tm, d), row),
                   pl.BlockSpec((tm, d), row),
                   pl.BlockSpec((tm, LANES), row)],
        out_shape=[jax.ShapeDtypeStruct((t, d), F32),
                   jax.ShapeDtypeStruct((t, d), F32),
                   jax.ShapeDtypeStruct((t, LANES), F32)],
        compiler_params=_cparams(("parallel",)),
        name="merge_out",
    )(gates, gates, ya, yb, x, w_out, norm_w.reshape(1, d), r_hi, r_lo, r_b)


def _route_kernel(lg_ref, o_ref):
    lg = lg_ref[...]
    lane = lax.broadcasted_iota(jnp.int32, lg.shape, 1)
    neg = -jnp.inf
    big = LANES
    is_g = lane < N_EXPERT_GROUPS
    gl = jnp.where(is_g, lg, neg)
    gmax = jnp.max(gl, axis=1, keepdims=True)
    grp = jnp.min(jnp.where(gl == gmax, lane, big), axis=1, keepdims=True)
    pg_sel = 1.0 / jnp.sum(jnp.where(is_g, jnp.exp(lg - gmax), 0.0), axis=1, keepdims=True)
    e_lo = SUBLANES + grp * EXPERTS_PER_GROUP
    in_grp = (lane >= e_lo) & (lane < e_lo + EXPERTS_PER_GROUP)
    el = jnp.where(in_grp, lg, neg)
    v0 = jnp.max(el, axis=1, keepdims=True)
    i0 = jnp.min(jnp.where(el == v0, lane, big), axis=1, keepdims=True)
    el1 = jnp.where(lane == i0, neg, el)
    v1 = jnp.max(el1, axis=1, keepdims=True)
    i1 = jnp.min(jnp.where(el1 == v1, lane, big), axis=1, keepdims=True)
    e1w = jnp.exp(v1 - v0)
    w0 = pg_sel / (1.0 + e1w)
    w1 = pg_sel * e1w / (1.0 + e1w)
    out = jnp.where(lane == 0, (i0 - SUBLANES).astype(F32),
          jnp.where(lane == 1, (i1 - SUBLANES).astype(F32),
          jnp.where(lane == 2, w0, jnp.where(lane == 3, w1, 0.0))))
    o_ref[...] = out


def _route(logits, tm):
    t = logits.shape[0]
    return pl.pallas_call(
        _route_kernel,
        grid=(t // tm,),
        in_specs=[pl.BlockSpec((tm, LANES), lambda i: (i, 0))],
        out_specs=pl.BlockSpec((tm, LANES), lambda i: (i, 0)),
        out_shape=jax.ShapeDtypeStruct((t, LANES), F32),
        compiler_params=_cparams(("parallel",)),
        name="route",
    )(logits)


def _moe_kernel(bexp_ref, nused_ref, tok_ref, wgt_ref, h_hbm, wg_ref, wu_ref, wd_ref, o_ref,
                xbuf, sem):
    del bexp_ref
    b = pl.program_id(0)
    nb = pl.num_programs(0)
    bm = xbuf.shape[1]
    n_used = nused_ref[0]

    def row_copy(slot, r, tok):
        return pltpu.make_async_copy(h_hbm.at[pl.ds(tok, 1), :], xbuf.at[slot, pl.ds(r, 1), :],
                                     sem.at[slot])

    def start_block(slot, which):
        def body(r, carry):
            row_copy(slot, r, tok_ref[which, 0, r]).start()
            return carry
        lax.fori_loop(0, bm, body, 0)

    def wait_block(slot):
        def body(r, carry):
            row_copy(slot, r, 0).wait()
            return carry
        lax.fori_loop(0, bm, body, 0)

    slot = b % 2

    @pl.when((b == 0) & (n_used > 0))
    def _():
        start_block(0, 0)

    @pl.when(b + 1 < jnp.minimum(n_used, nb))
    def _():
        start_block(1 - slot, 1)

    @pl.when(b < n_used)
    def _():
        wait_block(slot)
        xb = xbuf[slot].astype(BF16)
        hg = jnp.dot(xb, wg_ref[...], preferred_element_type=F32)
        hu = jnp.dot(xb, wu_ref[...], preferred_element_type=F32)
        act = (_silu(hg) * hu).astype(BF16)
        y = jnp.dot(act, wd_ref[...], preferred_element_type=F32)
        o_ref[...] = y * wgt_ref[...]

    @pl.when(b >= n_used)
    def _():
        o_ref[...] = jnp.zeros(o_ref.shape, o_ref.dtype)


def _moe(h2, block_expert, n_used, row_token, row_weight, w_g, w_u, w_d, bm):
    t, d = h2.shape
    ff = w_g.shape[2]
    n_rows = row_token.shape[0]
    nb = n_rows // bm
    tok_b = row_token.reshape(nb, 1, bm)
    tok_next = jnp.concatenate([tok_b[1:], tok_b[-1:]], axis=0)
    tok2 = jnp.stack([tok_b, tok_next], axis=1).reshape(nb, 2, 1, bm)
    grid_spec = pltpu.PrefetchScalarGridSpec(
        num_scalar_prefetch=2,
        grid=(nb,),
        in_specs=[pl.BlockSpec((None, 2, 1, bm), lambda b, be, nu: (b, 0, 0, 0), memory_space=pltpu.SMEM),
                  pl.BlockSpec((bm, 1), lambda b, be, nu: (b, 0)),
                  pl.BlockSpec(memory_space=pl.ANY),
                  pl.BlockSpec((None, d, ff), lambda b, be, nu: (be[b], 0, 0)),
                  pl.BlockSpec((None, d, ff), lambda b, be, nu: (be[b], 0, 0)),
                  pl.BlockSpec((None, ff, d), lambda b, be, nu: (be[b], 0, 0))],
        out_specs=pl.BlockSpec((bm, d), lambda b, be, nu: (b, 0)),
        scratch_shapes=[pltpu.VMEM((2, bm, d), F32),
                        pltpu.SemaphoreType.DMA((2,))],
    )
    return pl.pallas_call(
        _moe_kernel,
        grid_spec=grid_spec,
        out_shape=jax.ShapeDtypeStruct((n_rows, d), F32),
        compiler_params=_cparams(("arbitrary",)),
        name="moe_experts",
    )(block_expert, n_used, tok2, row_weight.reshape(n_rows, 1), h2, w_g, w_u, w_d)


def _combine_kernel(pos_ref, x1_ref, nw_ref, y_hbm, o_ref, ybuf, sem, *, normalize):
    i = pl.program_id(0)
    n = pl.num_programs(0)
    tm = x1_ref.shape[0]

    def row_copy(slot, k, r, pos):
        return pltpu.make_async_copy(y_hbm.at[pl.ds(pos, 1), :], ybuf.at[slot, k, pl.ds(r, 1), :],
                                     sem.at[slot])

    def start_tile(slot, which):
        def body(r, carry):
            for k in range(TOP_K_INNER):
                row_copy(slot, k, r, pos_ref[which, k, r]).start()
            return carry
        lax.fori_loop(0, tm, body, 0)

    def wait_tile(slot):
        def body(r, carry):
            for k in range(TOP_K_INNER):
                row_copy(slot, k, r, 0).wait()
            return carry
        lax.fori_loop(0, tm, body, 0)

    slot = i % 2

    @pl.when(i == 0)
    def _():
        start_tile(0, 0)

    @pl.when(i + 1 < n)
    def _():
        start_tile(1 - slot, 1)

    wait_tile(slot)
    acc = x1_ref[...]
    for k in range(TOP_K_INNER):
        acc = acc + ybuf[slot, k]
    o_ref[...] = _rms(acc, nw_ref[...]) if normalize else acc


def _combine(x1, y_rows, pos, norm_w, tm, normalize):
    t, d = x1.shape
    nt = t // tm
    pos_t = pos.reshape(nt, tm, TOP_K_INNER).transpose(0, 2, 1)
    pos_next = jnp.concatenate([pos_t[1:], pos_t[-1:]], axis=0)
    pos2 = jnp.stack([pos_t, pos_next], axis=1)
    return pl.pallas_call(
        functools.partial(_combine_kernel, normalize=normalize),
        grid=(nt,),
        in_specs=[pl.BlockSpec((None, 2, TOP_K_INNER, tm), lambda i: (i, 0, 0, 0), memory_space=pltpu.SMEM),
                  pl.BlockSpec((tm, d), lambda i: (i, 0)),
                  pl.BlockSpec((1, d), lambda i: (0, 0)),
                  pl.BlockSpec(memory_space=pl.ANY)],
        out_specs=pl.BlockSpec((tm, d), lambda i: (i, 0)),
        out_shape=jax.ShapeDtypeStruct((t, d), F32),
        scratch_shapes=[pltpu.VMEM((2, TOP_K_INNER, tm, d), F32),
                        pltpu.SemaphoreType.DMA((2,))],
        compiler_params=_cparams(("arbitrary",)),
        name="moe_combine",
    )(pos2, x1, norm_w.reshape(1, d), y_rows)


def _sort_plan(expert_ids, weights, n_tokens, bm):
    n_assign = n_tokens * TOP_K_INNER
    eid = expert_ids.reshape(n_assign)
    onehot = (eid[:, None] == jnp.arange(N_EXPERTS, dtype=jnp.int32)[None, :]).astype(jnp.int32)
    csum = jnp.cumsum(onehot, axis=0)
    counts = csum[-1]
    rank = jnp.take_along_axis(csum, eid[:, None], axis=1)[:, 0] - 1
    padded = ((counts + bm - 1) // bm) * bm
    ends_padded = jnp.cumsum(padded)
    start_padded = ends_padded - padded
    dest = start_padded[eid] + rank
    n_rows = n_assign + N_EXPERTS * bm
    nb = n_rows // bm
    token_idx = jnp.repeat(jnp.arange(n_tokens, dtype=jnp.int32), TOP_K_INNER)
    row_token = jnp.zeros((n_rows,), jnp.int32).at[dest].set(token_idx)
    row_weight = jnp.zeros((n_rows,), F32).at[dest].set(weights.reshape(n_assign))
    block_start = jnp.arange(nb, dtype=jnp.int32) * bm
    block_expert = jnp.clip(jnp.searchsorted(ends_padded, block_start, side='right'),
                            0, N_EXPERTS - 1).astype(jnp.int32)
    n_used = (ends_padded[-1] // bm).astype(jnp.int32).reshape(1)
    return block_expert, n_used, row_token, row_weight, dest.reshape(n_tokens, TOP_K_INNER)


def _layer(x, p):
    t, d = x.shape
    inner = p["w_a_up"].shape[0]
    n_heads = p["a_log"].shape[0]
    s5_width = p["w_glu"].shape[0]
    xbc_dim = inner + 2 * SSD_GROUPS * SSD_STATE
    sizes = (inner, xbc_dim, n_heads, s5_width, 2 * d)
    offs = [0]
    for s in sizes:
        offs.append(offs[-1] + s)
    w_in = p["w_in"]
    w_z = w_in[:, offs[0]:offs[1]].astype(BF16)
    w_xbc = w_in[:, offs[1]:offs[2]].astype(BF16)
    w_dt = jnp.zeros((d, LANES), BF16).at[:, :n_heads].set(w_in[:, offs[2]:offs[3]].astype(BF16))
    w_u = w_in[:, offs[3]:offs[4]].astype(BF16)
    w_gate = w_in[:, offs[4]:offs[5]].astype(BF16)
    dt_b = jnp.zeros((1, LANES), F32).at[0, :n_heads].set(p["dt_bias"].astype(F32))

    tm = min(ROW_TILE, t)
    tmm = min(MM_ROW_TILE, t)

    h = _rmsnorm(x, p["norm_mix_w"], tm)
    zs = _proj(h, w_z, None, "silu", tmm, MM_COL_TILE)
    xbc = _conv_proj(h, w_xbc, p["conv_w"], p["conv_b"], tmm, MM_COL_TILE)
    dt, dtt = _dt_proj(h, w_dt, dt_b, tm)
    u_steps = _chunk_proj(h, w_u)
    gates = _proj(h, w_gate, p["gate_b"], "sigmoid_bias", tmm, MM_COL_TILE)

    y = _ssd(xbc, dt, dtt, p["a_log"].astype(F32), p["d_ssd"].astype(F32), n_heads, min(SSD_CHUNK, t))
    ya = _gated_up(y, zs, p["norm_ssd_w"], p["w_a_up"].astype(BF16), tm)

    kbd, wmat, vmat, a_chunk = _s5_operators(p["s5_lambda_re"], p["s5_lambda_im"], p["s5_log_dt"],
                                             p["s5_b_re"], p["s5_b_im"], p["s5_c_re"], p["s5_c_im"])
    y5 = _s5(u_steps, kbd, wmat, vmat, a_chunk, p["s5_d"].astype(F32), min(S5_ROWS, t // S5_CHUNK))
    yb = _glu_up(y5, p["w_glu"].astype(BF16), p["w_b_up"].astype(BF16))

    w_router = jnp.zeros((d, LANES), F32)
    w_router = w_router.at[:, :N_EXPERT_GROUPS].set(p["w_route_group"].astype(F32))
    w_router = w_router.at[:, SUBLANES:SUBLANES + N_EXPERTS].set(p["w_route_expert"].astype(F32))
    r_hi = w_router.astype(BF16)
    r_lo = (w_router - r_hi.astype(F32)).astype(BF16)
    r_b = jnp.zeros((1, LANES), F32)
    r_b = r_b.at[0, :N_EXPERT_GROUPS].set(p["b_route_group"].astype(F32))
    r_b = r_b.at[0, SUBLANES:SUBLANES + N_EXPERTS].set(p["b_route_expert"].astype(F32))
    x1, h2, logits = _merge_out(gates, ya, yb, x, p["w_out"].astype(BF16), p["norm_ffn_w"],
                                r_hi, r_lo, r_b, tm)

    route = _route(logits, tm)
    expert_ids = route[:, :TOP_K_INNER].astype(jnp.int32)
    weights = route[:, TOP_K_INNER:2 * TOP_K_INNER]
    bm = MOE_BLOCK
    block_expert, n_used, row_token, row_weight, pos = _sort_plan(expert_ids, weights, t, bm)
    y_rows = _moe(h2, block_expert, n_used, row_token, row_weight,
                  p["w_exp_gate"].astype(BF16), p["w_exp_up"].astype(BF16),
                  p["w_exp_down"].astype(BF16), bm)
    return x1, y_rows, pos


def kernel(x, norm_mix_w, w_in, conv_w, conv_b, dt_bias, a_log, d_ssd, norm_ssd_w, w_a_up,
           s5_lambda_re, s5_lambda_im, s5_log_dt, s5_b_re, s5_b_im, s5_c_re, s5_c_im, s5_d,
           w_glu, w_b_up, gate_b, w_out, norm_ffn_w, w_route_group, b_route_group,
           w_route_expert, b_route_expert, w_exp_gate, w_exp_up, w_exp_down, norm_final_w):
    b, seq, d = x.shape
    assert b == 1, "the scans carry state along the flattened token axis"
    depth = w_in.shape[0]
    per_layer = dict(norm_mix_w=norm_mix_w, w_in=w_in, conv_w=conv_w, conv_b=conv_b, dt_bias=dt_bias,
                     a_log=a_log, d_ssd=d_ssd, norm_ssd_w=norm_ssd_w, w_a_up=w_a_up,
                     s5_lambda_re=s5_lambda_re, s5_lambda_im=s5_lambda_im, s5_log_dt=s5_log_dt,
                     s5_b_re=s5_b_re, s5_b_im=s5_b_im, s5_c_re=s5_c_re, s5_c_im=s5_c_im, s5_d=s5_d,
                     w_glu=w_glu, w_b_up=w_b_up, gate_b=gate_b, w_out=w_out, norm_ffn_w=norm_ffn_w,
                     w_route_group=w_route_group, b_route_group=b_route_group,
                     w_route_expert=w_route_expert, b_route_expert=b_route_expert,
                     w_exp_gate=w_exp_gate, w_exp_up=w_exp_up, w_exp_down=w_exp_down)
    xt = x.reshape(b * seq, d)
    tg = min(GATHER_TILE, b * seq)
    for i in range(depth):
        p = {k: v[i] for k, v in per_layer.items()}
        x1, y_rows, pos = _layer(xt, p)
        xt = _combine(x1, y_rows, pos, norm_final_w, tg, normalize=(i == depth - 1))
    return xt.reshape(b, seq, d)
```

```python
import functools
import math

import jax
import jax.numpy as jnp
from jax import lax
from jax.experimental import pallas as pl
from jax.experimental.pallas import tpu as pltpu

F32 = jnp.float32
BF16 = jnp.bfloat16

SSD_HEAD_DIM = 64
SSD_GROUPS = 8
SSD_STATE = 128
CONV_WIDTH = 4
S5_GROUP_CH = 16
S5_STATE = 64
N_EXPERT_GROUPS = 4
EXPERTS_PER_GROUP = 8
N_EXPERTS = N_EXPERT_GROUPS * EXPERTS_PER_GROUP
TOP_K_INNER = 2
RMS_EPS = 1e-6

LANES = 128
SUBLANES = 8
VMEM_LIMIT_BYTES = 52 * 1024 * 1024

ROW_TILE = 512
MM_ROW_TILE = 1024
MM_COL_TILE = 1024
SSD_CHUNK = 128
S5_CHUNK = 16
S5_SUPER = S5_GROUP_CH * 8
S5_ROWS = 256
MOE_BLOCK = 256
GATHER_TILE = 256
DMA_UNROLL = 8


def _cparams(sem, vmem=VMEM_LIMIT_BYTES):
    return pltpu.CompilerParams(dimension_semantics=sem, vmem_limit_bytes=vmem)


def _sigmoid(v):
    return 1.0 / (1.0 + jnp.exp(-v))


def _silu(v):
    return v * _sigmoid(v)


def _softplus(v):
    return jnp.maximum(v, 0.0) + jnp.log(1.0 + jnp.exp(-jnp.abs(v)))


def _gelu_tanh(v):
    c = math.sqrt(2.0 / math.pi)
    return 0.5 * v * (1.0 + jnp.tanh(c * (v + 0.044715 * (v * v * v))))


def _rms(v, w):
    ms = jnp.mean(v * v, axis=-1, keepdims=True)
    return v * lax.rsqrt(ms + RMS_EPS) * w


def _rmsnorm_kernel(x_ref, w_ref, o_ref):
    o_ref[...] = _rms(x_ref[...], w_ref[...]).astype(o_ref.dtype)


def _rmsnorm(x, w, tm):
    t, d = x.shape
    return pl.pallas_call(
        _rmsnorm_kernel,
        grid=(t // tm,),
        in_specs=[pl.BlockSpec((tm, d), lambda i: (i, 0)),
                  pl.BlockSpec((1, d), lambda i: (0, 0))],
        out_specs=pl.BlockSpec((tm, d), lambda i: (i, 0)),
        out_shape=jax.ShapeDtypeStruct((t, d), BF16),
        compiler_params=_cparams(("parallel",)),
        name="rmsnorm",
    )(x, w.reshape(1, d))


def _proj_kernel(a_ref, w_ref, b_ref, o_ref, *, act):
    p = jnp.dot(a_ref[...], w_ref[...], preferred_element_type=F32)
    if act == "silu":
        p = _silu(p)
    elif act == "sigmoid_bias":
        p = _sigmoid(p + b_ref[...])
    o_ref[...] = p.astype(o_ref.dtype)


def _proj(h, w, col0, n, b, act, tm, tn):
    t, k = h.shape
    tn = min(tn, n)
    assert col0 % tn == 0 and n % tn == 0
    jb = col0 // tn
    if b is None:
        b = jnp.zeros((1, n), F32)
    return pl.pallas_call(
        functools.partial(_proj_kernel, act=act),
        grid=(n // tn, t // tm),
        in_specs=[pl.BlockSpec((tm, k), lambda j, i: (i, 0)),
                  pl.BlockSpec((k, tn), lambda j, i: (0, jb + j)),
                  pl.BlockSpec((1, tn), lambda j, i: (0, j))],
        out_specs=pl.BlockSpec((tm, tn), lambda j, i: (i, j)),
        out_shape=jax.ShapeDtypeStruct((t, n), BF16),
        compiler_params=_cparams(("parallel", "parallel")),
        name="proj_" + act,
    )(h, w, b.reshape(1, n))


def _conv_proj_kernel(a_ref, w_ref, cw_ref, cb_ref, o_ref, ext_ref):
    tm = a_ref.shape[0]
    halo = SUBLANES

    @pl.when(pl.program_id(1) == 0)
    def _():
        ext_ref[pl.ds(0, halo), :] = jnp.zeros((halo, ext_ref.shape[1]), F32)

    p = jnp.dot(a_ref[...], w_ref[...], preferred_element_type=F32)
    ext_ref[pl.ds(halo, tm), :] = p
    acc = cb_ref[...] + cw_ref[CONV_WIDTH - 1:CONV_WIDTH, :] * p
    for k in range(CONV_WIDTH - 1):
        back = CONV_WIDTH - 1 - k
        acc = acc + cw_ref[k:k + 1, :] * ext_ref[pl.ds(halo - back, tm), :]
    o_ref[...] = _silu(acc).astype(o_ref.dtype)
    ext_ref[pl.ds(0, halo), :] = p[tm - halo:, :]


def _conv_proj(h, w, col0, n, conv_w, conv_b, tm, tn):
    t, k = h.shape
    assert col0 % tn == 0 and n % tn == 0
    jb = col0 // tn
    return pl.pallas_call(
        _conv_proj_kernel,
        grid=(n // tn, t // tm),
        in_specs=[pl.BlockSpec((tm, k), lambda j, i: (i, 0)),
                  pl.BlockSpec((k, tn), lambda j, i: (0, jb + j)),
                  pl.BlockSpec((CONV_WIDTH, tn), lambda j, i: (0, j)),
                  pl.BlockSpec((1, tn), lambda j, i: (0, j))],
        out_specs=pl.BlockSpec((tm, tn), lambda j, i: (i, j)),
        out_shape=jax.ShapeDtypeStruct((t, n), BF16),
        scratch_shapes=[pltpu.VMEM((tm + SUBLANES, tn), F32)],
        compiler_params=_cparams(("parallel", "arbitrary")),
        name="proj_conv",
    )(h, w, conv_w, conv_b.reshape(1, n))


def _dt_proj_kernel(a_ref, w_ref, b_ref, dt_ref, dtt_ref):
    p = jnp.dot(a_ref[...], w_ref[...], preferred_element_type=F32) + b_ref[...]
    dt = _softplus(p)
    dt_ref[...] = dt
    dtt_ref[...] = dt.T


def _dt_proj(h, w, col0, b_pad, tm):
    t, k = h.shape
    jb = col0 // LANES
    return pl.pallas_call(
        _dt_proj_kernel,
        grid=(t // tm,),
        in_specs=[pl.BlockSpec((tm, k), lambda i: (i, 0)),
                  pl.BlockSpec((k, LANES), lambda i: (0, jb)),
                  pl.BlockSpec((1, LANES), lambda i: (0, 0))],
        out_specs=[pl.BlockSpec((tm, LANES), lambda i: (i, 0)),
                   pl.BlockSpec((LANES, tm), lambda i: (0, i))],
        out_shape=[jax.ShapeDtypeStruct((t, LANES), F32),
                   jax.ShapeDtypeStruct((LANES, t), F32)],
        compiler_params=_cparams(("parallel",)),
        name="proj_dt",
    )(h, w, b_pad)


def _step_major_perm(n_rows):
    assert n_rows == S5_CHUNK * S5_CHUNK
    shift = S5_CHUNK.bit_length() - 1
    row = lax.broadcasted_iota(jnp.int32, (n_rows, n_rows), 0)
    col = lax.broadcasted_iota(jnp.int32, (n_rows, n_rows), 1)
    swapped = ((row & (S5_CHUNK - 1)) << shift) | (row >> shift)
    return jnp.where(col == swapped, 1.0, 0.0).astype(BF16)


def _chunk_proj_kernel(a_ref, w_ref, o_ref):
    u = jnp.dot(a_ref[...], w_ref[...], preferred_element_type=F32).astype(BF16)
    u_steps = jnp.dot(_step_major_perm(u.shape[0]), u, preferred_element_type=F32).astype(o_ref.dtype)
    nj = o_ref.shape[1]
    for s in range(o_ref.shape[0]):
        o_ref[s] = u_steps[s * nj:(s + 1) * nj, :]


def _chunk_proj(h, w, col0, n):
    t, k = h.shape
    nj = t // S5_CHUNK
    tm = S5_CHUNK * S5_CHUNK
    assert col0 % n == 0
    jb = col0 // n
    return pl.pallas_call(
        _chunk_proj_kernel,
        grid=(t // tm,),
        in_specs=[pl.BlockSpec((tm, k), lambda i: (i, 0)),
                  pl.BlockSpec((k, n), lambda i: (0, jb))],
        out_specs=pl.BlockSpec((S5_CHUNK, S5_CHUNK, n), lambda i: (0, i, 0)),
        out_shape=jax.ShapeDtypeStruct((S5_CHUNK, nj, n), BF16),
        compiler_params=_cparams(("parallel",)),
        name="proj_u",
    )(h, w)


def _cumsum_rows(v):
    n = v.shape[0]
    idx = lax.broadcasted_iota(jnp.int32, v.shape, 0)
    k = 1
    while k < n:
        v = v + jnp.where(idx >= k, pltpu.roll(v, k, 0), 0.0)
        k *= 2
    return v


def _cumsum_lanes(v):
    n = v.shape[1]
    idx = lax.broadcasted_iota(jnp.int32, v.shape, 1)
    k = 1
    while k < n:
        v = v + jnp.where(idx >= k, pltpu.roll(v, k, 1), 0.0)
        k *= 2
    return v


def _ssd_kernel(xbc_ref, dt_ref, dtt_ref, alog_r_ref, alog_c_ref, dskip_ref, o_ref, state_ref, *,
                n_heads):
    q = dt_ref.shape[0]
    n = SSD_STATE
    p_dim = SSD_HEAD_DIM
    r_heads = n_heads // SSD_GROUPS
    gw = r_heads * p_dim
    inner = n_heads * p_dim

    @pl.when(pl.program_id(0) == 0)
    def _():
        state_ref[...] = jnp.zeros(state_ref.shape, F32)

    dtt = dtt_ref[...]
    cs_col = _cumsum_rows(dt_ref[...] * -jnp.exp(alog_r_ref[...]))
    cs_row = _cumsum_lanes(dtt * -jnp.exp(alog_c_ref[...]))
    causal = lax.broadcasted_iota(jnp.int32, (q, q), 0) >= lax.broadcasted_iota(jnp.int32, (q, q), 1)
    head_of_lane = lax.broadcasted_iota(jnp.int32, (1, gw), 1) // p_dim
    eye = jnp.where(lax.broadcasted_iota(jnp.int32, (n, n), 0) == lax.broadcasted_iota(jnp.int32, (n, n), 1),
                    1.0, 0.0).astype(BF16)
    nt = (((1,), (1,)), ((), ()))

    for g in range(SSD_GROUPS):
        x_g = xbc_ref[:, g * gw:(g + 1) * gw]
        b_g = xbc_ref[:, inner + g * n:inner + (g + 1) * n]
        c_g = xbc_ref[:, inner + (SSD_GROUPS + g) * n:inner + (SSD_GROUPS + g + 1) * n]
        cb = lax.dot_general(c_g, b_g, nt, preferred_element_type=F32)
        b_t = lax.dot_general(eye, b_g, nt, preferred_element_type=F32)
        c_f = c_g.astype(F32)
        s_g = state_ref[g]
        lhs_parts, bt_parts, cd = [], [], jnp.zeros((1, gw), F32)
        for r in range(r_heads):
            h = g * r_heads + r
            csb = jnp.broadcast_to(cs_col[:, h:h + 1], (q, n))
            csr = cs_row[h:h + 1, :]
            dtr = dtt[h:h + 1, :]
            cs_last = csr[:, q - 1:q]
            decay = jnp.exp(jnp.where(causal, csb - csr, -1e30))
            lhs_parts.append(jnp.concatenate([cb * decay * dtr, c_f * jnp.exp(csb)], axis=1).astype(BF16))
            bt_parts.append((b_t * (dtr * jnp.exp(cs_last - csr))).astype(BF16))
            cd = jnp.where(head_of_lane == r, jnp.exp(cs_last), cd)
        rhs = jnp.concatenate([x_g, s_g.astype(BF16)], axis=0)
        y_all = jnp.dot(jnp.concatenate(lhs_parts, axis=0), rhs, preferred_element_type=F32)
        s_all = jnp.dot(jnp.concatenate(bt_parts, axis=0), x_g, preferred_element_type=F32)
        y_g = jnp.zeros((q, gw), F32)
        s_new = jnp.zeros((n, gw), F32)
        for r in range(r_heads):
            mine = head_of_lane == r
            y_g = jnp.where(mine, y_all[r * q:(r + 1) * q, :], y_g)
            s_new = jnp.where(mine, s_all[r * n:(r + 1) * n, :], s_new)
        state_ref[g] = s_g * cd + s_new
        y_g = y_g + x_g.astype(F32) * dskip_ref[:, g * gw:(g + 1) * gw]
        o_ref[:, g * gw:(g + 1) * gw] = y_g.astype(o_ref.dtype)


def _ssd(xbc, dt, dtt, a_log, d_skip, n_heads, q):
    t, width = xbc.shape
    inner = n_heads * SSD_HEAD_DIM
    gw = inner // SSD_GROUPS
    assert n_heads % SUBLANES == 0 and n_heads <= LANES
    alog_r = jnp.zeros((1, LANES), F32).at[0, :n_heads].set(a_log)
    alog_c = a_log.reshape(n_heads, 1)
    dskip = jnp.repeat(d_skip, SSD_HEAD_DIM).reshape(1, inner)
    return pl.pallas_call(
        functools.partial(_ssd_kernel, n_heads=n_heads),
        grid=(t // q,),
        in_specs=[pl.BlockSpec((q, width), lambda c: (c, 0)),
                  pl.BlockSpec((q, LANES), lambda c: (c, 0)),
                  pl.BlockSpec((n_heads, q), lambda c: (0, c)),
                  pl.BlockSpec((1, LANES), lambda c: (0, 0)),
                  pl.BlockSpec((n_heads, 1), lambda c: (0, 0)),
                  pl.BlockSpec((1, inner), lambda c: (0, 0))],
        out_specs=pl.BlockSpec((q, inner), lambda c: (c, 0)),
        out_shape=jax.ShapeDtypeStruct((t, inner), BF16),
        scratch_shapes=[pltpu.VMEM((SSD_GROUPS, SSD_STATE, gw), F32)],
        compiler_params=_cparams(("arbitrary",)),
        name="ssd_scan",
    )(xbc, dt, dtt, alog_r, alog_c, dskip)


def _gated_up_kernel(y_ref, z_ref, nw_ref, w_ref, o_ref):
    v = y_ref[...].astype(F32) * z_ref[...].astype(F32)
    na = _rms(v, nw_ref[...]).astype(BF16)
    o_ref[...] = jnp.dot(na, w_ref[...], preferred_element_type=F32).astype(o_ref.dtype)


def _gated_up(y, zs, norm_w, w, tm):
    t, d = y.shape
    n = w.shape[1]
    return pl.pallas_call(
        _gated_up_kernel,
        grid=(t // tm,),
        in_specs=[pl.BlockSpec((tm, d), lambda i: (i, 0)),
                  pl.BlockSpec((tm, d), lambda i: (i, 0)),
                  pl.BlockSpec((1, d), lambda i: (0, 0)),
                  pl.BlockSpec((d, n), lambda i: (0, 0))],
        out_specs=pl.BlockSpec((tm, n), lambda i: (i, 0)),
        out_shape=jax.ShapeDtypeStruct((t, n), BF16),
        compiler_params=_cparams(("parallel",)),
        name="ssd_gated_up",
    )(y, zs, norm_w.reshape(1, d), w)


def _s5_operators(lam_re, lam_im, log_dt, b_re, b_im, c_re, c_im):
    ng, ns = lam_re.shape
    nc = S5_GROUP_CH
    L = S5_CHUNK
    per = S5_SUPER // nc
    nsg = ng // per
    lr, li = lam_re.astype(F32), lam_im.astype(F32)
    dt = jnp.exp(log_dt.astype(F32))[:, None]
    mag = jnp.exp(lr * dt)
    ang = li * dt
    abar_r, abar_i = mag * jnp.cos(ang), mag * jnp.sin(ang)
    den = lr * lr + li * li
    nr, ni = abar_r - 1.0, abar_i
    coef_r = (nr * lr + ni * li) / den
    coef_i = (ni * lr - nr * li) / den
    bre, bim = b_re.astype(F32), b_im.astype(F32)
    bb_r = coef_r[..., None] * bre - coef_i[..., None] * bim
    bb_i = coef_r[..., None] * bim + coef_i[..., None] * bre
    cre, cim = c_re.astype(F32), c_im.astype(F32)
    ks = jnp.arange(L + 1, dtype=F32)[:, None, None]
    pmag = jnp.exp(ks * (lr * dt)[None])
    pang = ks * ang[None]
    pw_r, pw_i = pmag * jnp.cos(pang), pmag * jnp.sin(pang)
    ca_r = cre[None] * pw_r[:, :, None, :] - cim[None] * pw_i[:, :, None, :]
    ca_i = cre[None] * pw_i[:, :, None, :] + cim[None] * pw_r[:, :, None, :]
    kk = (jnp.einsum('kgcn,gnd->kgcd', ca_r[:L], bb_r, precision=lax.Precision.HIGHEST)
          - jnp.einsum('kgcn,gnd->kgcd', ca_i[:L], bb_i, precision=lax.Precision.HIGHEST))
    eye = jnp.eye(per, dtype=F32)
    kk5 = kk.reshape(L, nsg, per, nc, nc)
    kbd = (kk5.transpose(1, 0, 2, 4, 3)[:, :, :, :, None, :]
           * eye[None, None, :, None, :, None]).reshape(nsg, L, per * nc, per * nc)
    ks_rev = (L - 1) - jnp.arange(L, dtype=F32)[:, None, None]
    rmag = jnp.exp(ks_rev * (lr * dt)[None])
    rang = ks_rev * ang[None]
    rev_r, rev_i = rmag * jnp.cos(rang), rmag * jnp.sin(rang)
    ab_r = rev_r[..., None] * bb_r[None] - rev_i[..., None] * bb_i[None]
    ab_i = rev_r[..., None] * bb_i[None] + rev_i[..., None] * bb_r[None]

    def _w_half(ab):
        ab6 = ab.reshape(L, nsg, per, ns, nc)
        return (ab6.transpose(1, 0, 2, 4, 3)[:, :, :, :, None, :]
                * eye[None, None, :, None, :, None]).reshape(nsg, L * per * nc, per * ns)

    wmat = jnp.concatenate([_w_half(ab_r), _w_half(ab_i)], axis=2)

    def _v_half(ca):
        ca6 = ca.reshape(L, nsg, per, nc, ns)
        return (ca6.transpose(1, 2, 4, 0, 3)[:, :, :, :, None, :]
                * eye[None, :, None, None, :, None]).reshape(nsg, per * ns, L * per * nc)

    vmat = jnp.concatenate([_v_half(ca_r[1:]), -_v_half(ca_i[1:])], axis=1)
    a_chunk = jnp.concatenate([pw_r[L].reshape(nsg, 1, per * ns),
                               pw_i[L].reshape(nsg, 1, per * ns)], axis=2)
    return kbd.astype(BF16), wmat.astype(BF16), vmat.astype(BF16), a_chunk


def _s5_kernel(u_ref, kbd_ref, w_ref, v_ref, ach_ref, dsk_ref, o_ref,
               toep_ref, x_ref, sp_ref, carry_ref):
    jb = pl.program_id(1)
    nl, nj, cw = u_ref.shape
    half = carry_ref.shape[1] // 2

    @pl.when(jb == 0)
    def _():
        carry_ref[...] = jnp.zeros(carry_ref.shape, F32)
        toep_ref[...] = jnp.zeros(toep_ref.shape, toep_ref.dtype)
        for s_in in range(nl):
            for s_out in range(s_in, nl):
                toep_ref[pl.ds(s_in * cw, cw), pl.ds(s_out * cw, cw)] = kbd_ref[s_out - s_in]

    lhs = jnp.concatenate([u_ref[s] for s in range(nl)], axis=1)
    x_ref[...] = jnp.dot(lhs, w_ref[...], preferred_element_type=F32)

    a_re = ach_ref[:, :half]
    a_im = ach_ref[:, half:]

    def step(j, carry):
        s_re, s_im = carry
        sp_ref[pl.ds(j, 1), :half] = s_re
        sp_ref[pl.ds(j, 1), half:] = s_im
        xr = x_ref[pl.ds(j, 1), :half]
        xi = x_ref[pl.ds(j, 1), half:]
        return (a_re * s_re - a_im * s_im + xr, a_re * s_im + a_im * s_re + xi)

    s_re, s_im = lax.fori_loop(0, nj, step, (carry_ref[:, :half], carry_ref[:, half:]), unroll=8)
    carry_ref[:, :half] = s_re
    carry_ref[:, half:] = s_im

    y_state = jnp.dot(sp_ref[...].astype(BF16), v_ref[...], preferred_element_type=F32)
    dsk = dsk_ref[...]
    pair = 2 * cw
    for tp in range(nl // 2):
        kdim = pair * (tp + 1)
        y = jnp.dot(lhs[:, :kdim], toep_ref[pl.ds(0, kdim), pl.ds(tp * pair, pair)],
                    preferred_element_type=F32)
        y = y + y_state[:, tp * pair:(tp + 1) * pair]
        for h in range(2):
            s = 2 * tp + h
            o_ref[s] = (y[:, h * cw:(h + 1) * cw] + dsk * u_ref[s].astype(F32)).astype(o_ref.dtype)


def _s5(u_steps, kbd, wmat, vmat, a_chunk, d_skip, tj):
    nl, nj, width = u_steps.shape
    nsg = width // S5_SUPER
    nstate = a_chunk.shape[2]
    return pl.pallas_call(
        _s5_kernel,
        grid=(nsg, nj // tj),
        in_specs=[pl.BlockSpec((nl, tj, S5_SUPER), lambda g, j: (0, j, g)),
                  pl.BlockSpec((None, nl, S5_SUPER, S5_SUPER), lambda g, j: (g, 0, 0, 0)),
                  pl.BlockSpec((None, nl * S5_SUPER, nstate), lambda g, j: (g, 0, 0)),
                  pl.BlockSpec((None, nstate, nl * S5_SUPER), lambda g, j: (g, 0, 0)),
                  pl.BlockSpec((None, 1, nstate), lambda g, j: (g, 0, 0)),
                  pl.BlockSpec((None, 1, S5_SUPER), lambda g, j: (g, 0, 0))],
        out_specs=pl.BlockSpec((nl, tj, S5_SUPER), lambda g, j: (0, j, g)),
        out_shape=jax.ShapeDtypeStruct((nl, nj, width), BF16),
        scratch_shapes=[pltpu.VMEM((nl * S5_SUPER, nl * S5_SUPER), BF16),
                        pltpu.VMEM((tj, nstate), F32),
                        pltpu.VMEM((tj, nstate), F32),
                        pltpu.VMEM((1, nstate), F32)],
        compiler_params=_cparams(("parallel", "arbitrary")),
        name="s5_scan",
    )(u_steps, kbd, wmat, vmat, a_chunk, d_skip.reshape(nsg, 1, S5_SUPER))


def _glu_up_kernel(y_ref, wg_ref, wu_ref, o_ref):
    y_steps = jnp.concatenate([y_ref[s] for s in range(y_ref.shape[0])], axis=0)
    y = jnp.dot(_step_major_perm(y_steps.shape[0]), y_steps, preferred_element_type=F32)
    v = _gelu_tanh(y)
    gate = _sigmoid(jnp.dot(v.astype(BF16), wg_ref[...], preferred_element_type=F32))
    o_ref[...] = jnp.dot((v * gate).astype(BF16), wu_ref[...],
                         preferred_element_type=F32).astype(o_ref.dtype)


def _glu_up(y_steps, w_glu, w_up):
    nl, nj, width = y_steps.shape
    n = w_up.shape[1]
    tm = S5_CHUNK * S5_CHUNK
    return pl.pallas_call(
        _glu_up_kernel,
        grid=(nj // S5_CHUNK,),
        in_specs=[pl.BlockSpec((nl, S5_CHUNK, width), lambda i: (0, i, 0)),
                  pl.BlockSpec((width, width), lambda i: (0, 0)),
                  pl.BlockSpec((width, n), lambda i: (0, 0))],
        out_specs=pl.BlockSpec((tm, n), lambda i: (i, 0)),
        out_shape=jax.ShapeDtypeStruct((nj * nl, n), BF16),
        compiler_params=_cparams(("parallel",)),
        name="s5_glu_up",
    )(y_steps, w_glu, w_up)


def _merge_out_kernel(ga_ref, gb_ref, ya_ref, yb_ref, x_ref, w_ref, nw_ref, rhi_ref, rlo_ref, rb_ref,
                      x1_ref, h2_ref, lg_ref):
    merged = (ga_ref[...].astype(F32) * ya_ref[...].astype(F32)
              + gb_ref[...].astype(F32) * yb_ref[...].astype(F32))
    x1 = x_ref[...] + jnp.dot(merged.astype(BF16), w_ref[...], preferred_element_type=F32)
    x1_ref[...] = x1
    h2 = _rms(x1, nw_ref[...])
    h2_ref[...] = h2
    hi = h2.astype(BF16)
    lo = (h2 - hi.astype(F32)).astype(BF16)
    lg = (jnp.dot(hi, rhi_ref[...], preferred_element_type=F32)
          + jnp.dot(lo, rhi_ref[...], preferred_element_type=F32)
          + jnp.dot(hi, rlo_ref[...], preferred_element_type=F32))
    lg_ref[...] = lg + rb_ref[...]


def _merge_out(gates, ya, yb, x, w_out, norm_w, r_hi, r_lo, r_b, tm):
    t, d = x.shape
    row = lambda i: (i, 0)
    full = lambda i: (0, 0)
    return pl.pallas_call(
        _merge_out_kernel,
        grid=(t // tm,),
        in_specs=[pl.BlockSpec((tm, d), row),
                  pl.BlockSpec((tm, d), lambda i: (i, 1)),
                  pl.BlockSpec((tm, d), row),
                  pl.BlockSpec((tm, d), row),
                  pl.BlockSpec((tm, d), row),
                  pl.BlockSpec((d, d), full),
                  pl.BlockSpec((1, d), full),
                  pl.BlockSpec((d, LANES), full),
                  pl.BlockSpec((d, LANES), full),
                  pl.BlockSpec((1, LANES), full)],
        out_specs=[pl.BlockSpec((tm, d), row),
                   pl.BlockSpec((tm, d), row),
                   pl.BlockSpec((tm, LANES), row)],
        out_shape=[jax.ShapeDtypeStruct((t, d), F32),
                   jax.ShapeDtypeStruct((t, d), F32),
                   jax.ShapeDtypeStruct((t, LANES), F32)],
        compiler_params=_cparams(("parallel",)),
        name="merge_out",
    )(gates, gates, ya, yb, x, w_out, norm_w.reshape(1, d), r_hi, r_lo, r_b)


ROUTE_ID_LANE = 0
ROUTE_W_LANE = TOP_K_INNER
ROUTE_EXPERT_LANE0 = SUBLANES


def _route_kernel(lg_ref, o_ref):
    lg = lg_ref[...]
    lane = lax.broadcasted_iota(jnp.int32, lg.shape, 1)
    neg = -jnp.inf
    big = LANES
    is_g = lane < N_EXPERT_GROUPS
    gl = jnp.where(is_g, lg, neg)
    gmax = jnp.max(gl, axis=1, keepdims=True)
    grp = jnp.min(jnp.where(gl == gmax, lane, big), axis=1, keepdims=True)
    pg_sel = 1.0 / jnp.sum(jnp.where(is_g, jnp.exp(lg - gmax), 0.0), axis=1, keepdims=True)
    e_lo = ROUTE_EXPERT_LANE0 + grp * EXPERTS_PER_GROUP
    in_grp = (lane >= e_lo) & (lane < e_lo + EXPERTS_PER_GROUP)
    el = jnp.where(in_grp, lg, neg)
    v0 = jnp.max(el, axis=1, keepdims=True)
    i0 = jnp.min(jnp.where(el == v0, lane, big), axis=1, keepdims=True)
    el1 = jnp.where(lane == i0, neg, el)
    v1 = jnp.max(el1, axis=1, keepdims=True)
    i1 = jnp.min(jnp.where(el1 == v1, lane, big), axis=1, keepdims=True)
    e1w = jnp.exp(v1 - v0)
    w0 = pg_sel / (1.0 + e1w)
    w1 = pg_sel * e1w / (1.0 + e1w)
    out = jnp.where(lane == ROUTE_ID_LANE, (i0 - ROUTE_EXPERT_LANE0).astype(F32),
          jnp.where(lane == ROUTE_ID_LANE + 1, (i1 - ROUTE_EXPERT_LANE0).astype(F32),
          jnp.where(lane == ROUTE_W_LANE, w0, jnp.where(lane == ROUTE_W_LANE + 1, w1, 0.0))))
    o_ref[...] = out


def _route(logits, tm):
    t = logits.shape[0]
    return pl.pallas_call(
        _route_kernel,
        grid=(t // tm,),
        in_specs=[pl.BlockSpec((tm, LANES), lambda i: (i, 0))],
        out_specs=pl.BlockSpec((tm, LANES), lambda i: (i, 0)),
        out_shape=jax.ShapeDtypeStruct((t, LANES), F32),
        compiler_params=_cparams(("parallel",)),
        name="route",
    )(logits)


def _dispatch_plan(expert_ids, bm):
    n_tokens = expert_ids.shape[0]
    n_assign = n_tokens * TOP_K_INNER
    eid = expert_ids.reshape(n_assign)
    experts = jnp.arange(N_EXPERTS, dtype=jnp.int32)
    onehot = (eid[:, None] == experts[None, :]).astype(jnp.int32)
    csum = jnp.cumsum(onehot, axis=0)
    counts = csum[-1]
    rank = jnp.sum(onehot * csum, axis=1) - 1
    padded = ((counts + bm - 1) // bm) * bm
    ends = jnp.cumsum(padded)
    starts = ends - padded
    dest = jnp.sum(onehot * starts[None, :], axis=1) + rank
    n_rows = n_assign + N_EXPERTS * bm
    nb = n_rows // bm
    block_start = jnp.arange(nb, dtype=jnp.int32) * bm
    block_expert = jnp.minimum(jnp.sum((ends[None, :] <= block_start[:, None]).astype(jnp.int32), axis=1),
                               N_EXPERTS - 1)
    n_used = (ends[-1] // bm).astype(jnp.int32).reshape(1)
    return (dest.reshape(n_tokens, TOP_K_INNER), block_expert, n_used,
            (starts + counts).astype(jnp.int32), (padded - counts).astype(jnp.int32), n_rows)


def _dispatch_kernel(pad_start_ref, pad_count_ref, nused_ref, dest_ref, h_ref, xs_hbm, stage, zrow, sem, zsem):
    i = pl.program_id(0)
    n_steps = pl.num_programs(0)
    tm = h_ref.shape[0]
    bm = zrow.shape[0]
    nb = xs_hbm.shape[0] // bm
    slot = i % 2

    def tile_wait(s):
        for _ in range(TOP_K_INNER):
            pltpu.make_async_copy(stage.at[s], xs_hbm.at[pl.ds(0, tm), :], sem.at[s]).wait()

    @pl.when(i >= 2)
    def _():
        tile_wait(slot)

    stage[slot] = h_ref[...]

    def issue(r, carry):
        for k in range(TOP_K_INNER):
            pltpu.make_async_copy(stage.at[slot, pl.ds(r, 1), :],
                                  xs_hbm.at[pl.ds(dest_ref[k, r], 1), :], sem.at[slot]).start()
        return carry
    lax.fori_loop(0, tm, issue, 0, unroll=DMA_UNROLL)

    @pl.when(i == n_steps - 1)
    def _():
        tile_wait(slot)

        @pl.when(n_steps >= 2)
        def _():
            tile_wait(1 - slot)

        zrow[...] = jnp.zeros(zrow.shape, zrow.dtype)

        def pad_copy(e, r):
            return pltpu.make_async_copy(zrow.at[pl.ds(0, 1), :],
                                         xs_hbm.at[pl.ds(pad_start_ref[e] + r, 1), :], zsem)
        def block_copy(b):
            return pltpu.make_async_copy(zrow, xs_hbm.at[pl.ds(b * bm, bm), :], zsem)
        for e in range(N_EXPERTS):
            lax.fori_loop(0, pad_count_ref[e], lambda r, c, e=e: (pad_copy(e, r).start(), c)[1], 0)
        lax.fori_loop(nused_ref[0], nb, lambda b, c: (block_copy(b).start(), c)[1], 0)
        for e in range(N_EXPERTS):
            lax.fori_loop(0, pad_count_ref[e], lambda r, c, e=e: (pad_copy(e, r).wait(), c)[1], 0)
        lax.fori_loop(nused_ref[0], nb, lambda b, c: (block_copy(b).wait(), c)[1], 0)


def _dispatch(h2, dest, pad_start, pad_count, n_used, n_rows, tm, bm):
    t, d = h2.shape
    nt = t // tm
    dest_t = dest.reshape(nt, tm, TOP_K_INNER).transpose(0, 2, 1)
    grid_spec = pltpu.PrefetchScalarGridSpec(
        num_scalar_prefetch=3,
        grid=(nt,),
        in_specs=[pl.BlockSpec((None, TOP_K_INNER, tm), lambda i, *_: (i, 0, 0), memory_space=pltpu.SMEM),
                  pl.BlockSpec((tm, d), lambda i, *_: (i, 0))],
        out_specs=pl.BlockSpec(memory_space=pl.ANY),
        scratch_shapes=[pltpu.VMEM((2, tm, d), F32),
                        pltpu.VMEM((bm, d), F32),
                        pltpu.SemaphoreType.DMA((2,)),
                        pltpu.SemaphoreType.DMA(())],
    )
    return pl.pallas_call(
        _dispatch_kernel,
        grid_spec=grid_spec,
        out_shape=jax.ShapeDtypeStruct((n_rows, d), F32),
        compiler_params=_cparams(("arbitrary",)),
        name="moe_dispatch",
    )(pad_start, pad_count, n_used, dest_t, h2)


def _experts_kernel(bexp_ref, nused_ref, xs_ref, wg_ref, wu_ref, wd_ref, o_ref, wg_s, wu_s, wd_s):
    b = pl.program_id(0)
    n_used = nused_ref[0]
    new_expert = (b == 0) | (bexp_ref[b] != bexp_ref[jnp.maximum(b - 1, 0)])

    @pl.when((b < n_used) & new_expert)
    def _():
        wg_s[...] = wg_ref[...].astype(BF16)
        wu_s[...] = wu_ref[...].astype(BF16)
        wd_s[...] = wd_ref[...].astype(BF16)

    @pl.when(b < n_used)
    def _():
        xb = xs_ref[...].astype(BF16)
        hg = jnp.dot(xb, wg_s[...], preferred_element_type=F32)
        hu = jnp.dot(xb, wu_s[...], preferred_element_type=F32)
        act = (_silu(hg) * hu).astype(BF16)
        o_ref[...] = jnp.dot(act, wd_s[...], preferred_element_type=F32)

    @pl.when(b >= n_used)
    def _():
        o_ref[...] = jnp.zeros(o_ref.shape, o_ref.dtype)


def _experts(xs, block_expert, n_used, w_g, w_u, w_d, bm):
    n_rows, d = xs.shape
    ff = w_g.shape[2]
    nb = n_rows // bm

    def used(b, nu):
        return jnp.minimum(b, jnp.maximum(nu[0] - 1, 0))

    grid_spec = pltpu.PrefetchScalarGridSpec(
        num_scalar_prefetch=2,
        grid=(nb,),
        in_specs=[pl.BlockSpec((bm, d), lambda b, be, nu: (used(b, nu), 0)),
                  pl.BlockSpec((None, d, ff), lambda b, be, nu: (be[used(b, nu)], 0, 0)),
                  pl.BlockSpec((None, d, ff), lambda b, be, nu: (be[used(b, nu)], 0, 0)),
                  pl.BlockSpec((None, ff, d), lambda b, be, nu: (be[used(b, nu)], 0, 0))],
        out_specs=pl.BlockSpec((bm, d), lambda b, be, nu: (b, 0)),
        scratch_shapes=[pltpu.VMEM((d, ff), BF16),
                        pltpu.VMEM((d, ff), BF16),
                        pltpu.VMEM((ff, d), BF16)],
    )
    return pl.pallas_call(
        _experts_kernel,
        grid_spec=grid_spec,
        out_shape=jax.ShapeDtypeStruct((n_rows, d), F32),
        compiler_params=_cparams(("arbitrary",)),
        name="moe_experts",
    )(block_expert, n_used, xs, w_g, w_u, w_d)


def _combine_kernel(pos_ref, x1_ref, rt_ref, nw_ref, y_hbm, o_ref, ybuf, sem, *, normalize):
    i = pl.program_id(0)
    n = pl.num_programs(0)
    tm = x1_ref.shape[0]

    def start_tile(slot, which):
        def body(r, carry):
            for k in range(TOP_K_INNER):
                pltpu.make_async_copy(y_hbm.at[pl.ds(pos_ref[which, k, r], 1), :],
                                      ybuf.at[slot, k, pl.ds(r, 1), :], sem.at[slot]).start()
            return carry
        lax.fori_loop(0, tm, body, 0, unroll=DMA_UNROLL)

    def wait_tile(slot):
        for k in range(TOP_K_INNER):
            pltpu.make_async_copy(y_hbm.at[pl.ds(0, tm), :], ybuf.at[slot, k], sem.at[slot]).wait()

    slot = i % 2

    @pl.when(i == 0)
    def _():
        start_tile(0, 0)

    @pl.when(i + 1 < n)
    def _():
        start_tile(1 - slot, 1)

    wait_tile(slot)
    acc = x1_ref[...]
    for k in range(TOP_K_INNER):
        acc = acc + rt_ref[:, ROUTE_W_LANE + k:ROUTE_W_LANE + k + 1] * ybuf[slot, k]
    o_ref[...] = _rms(acc, nw_ref[...]) if normalize else acc


def _combine(x1, y_rows, pos, route, norm_w, tm, normalize):
    t, d = x1.shape
    nt = t // tm
    pos_t = pos.reshape(nt, tm, TOP_K_INNER).transpose(0, 2, 1)
    pos_next = jnp.concatenate([pos_t[1:], pos_t[-1:]], axis=0)
    pos2 = jnp.stack([pos_t, pos_next], axis=1)
    return pl.pallas_call(
        functools.partial(_combine_kernel, normalize=normalize),
        grid=(nt,),
        in_specs=[pl.BlockSpec((None, 2, TOP_K_INNER, tm), lambda i: (i, 0, 0, 0), memory_space=pltpu.SMEM),
                  pl.BlockSpec((tm, d), lambda i: (i, 0)),
                  pl.BlockSpec((tm, LANES), lambda i: (i, 0)),
                  pl.BlockSpec((1, d), lambda i: (0, 0)),
                  pl.BlockSpec(memory_space=pl.ANY)],
        out_specs=pl.BlockSpec((tm, d), lambda i: (i, 0)),
        out_shape=jax.ShapeDtypeStruct((t, d), F32),
        scratch_shapes=[pltpu.VMEM((2, TOP_K_INNER, tm, d), F32),
                        pltpu.SemaphoreType.DMA((2,))],
        compiler_params=_cparams(("arbitrary",)),
        name="moe_combine",
    )(pos2, x1, route, norm_w.reshape(1, d), y_rows)


def _layer(x, p):
    t, d = x.shape
    inner = p["w_a_up"].shape[0]
    n_heads = p["a_log"].shape[0]
    s5_width = p["w_glu"].shape[0]
    xbc_dim = inner + 2 * SSD_GROUPS * SSD_STATE
    sizes = (inner, xbc_dim, n_heads, s5_width, 2 * d)
    offs = [0]
    for s in sizes:
        offs.append(offs[-1] + s)
    w_in = p["w_in"]
    dt_pad = jnp.zeros((d, LANES - n_heads), w_in.dtype)
    w_cat = jnp.concatenate([w_in[:, offs[0]:offs[1]], w_in[:, offs[1]:offs[2]], w_in[:, offs[3]:offs[4]],
                             w_in[:, offs[4]:offs[5]], w_in[:, offs[2]:offs[3]], dt_pad], axis=1).astype(BF16)
    col_z, col_xbc = 0, inner
    col_u = col_xbc + xbc_dim
    col_gate = col_u + s5_width
    col_dt = col_gate + 2 * d
    dt_b = jnp.zeros((1, LANES), F32).at[0, :n_heads].set(p["dt_bias"].astype(F32))

    tm = min(ROW_TILE, t)
    tmm = min(MM_ROW_TILE, t)

    h = _rmsnorm(x, p["norm_mix_w"], tm)
    zs = _proj(h, w_cat, col_z, inner, None, "silu", tmm, MM_COL_TILE)
    xbc = _conv_proj(h, w_cat, col_xbc, xbc_dim, p["conv_w"], p["conv_b"], tmm, MM_COL_TILE)
    dt, dtt = _dt_proj(h, w_cat, col_dt, dt_b, tm)
    u_steps = _chunk_proj(h, w_cat, col_u, s5_width)
    gates = _proj(h, w_cat, col_gate, 2 * d, p["gate_b"], "sigmoid_bias", tmm, MM_COL_TILE)

    y = _ssd(xbc, dt, dtt[:n_heads], p["a_log"].astype(F32), p["d_ssd"].astype(F32), n_heads,
             min(SSD_CHUNK, t))
    ya = _gated_up(y, zs, p["norm_ssd_w"], p["w_a_up"].astype(BF16), tm)

    kbd, wmat, vmat, a_chunk = _s5_operators(p["s5_lambda_re"], p["s5_lambda_im"], p["s5_log_dt"],
                                             p["s5_b_re"], p["s5_b_im"], p["s5_c_re"], p["s5_c_im"])
    y5 = _s5(u_steps, kbd, wmat, vmat, a_chunk, p["s5_d"].astype(F32), min(S5_ROWS, t // S5_CHUNK))
    yb = _glu_up(y5, p["w_glu"].astype(BF16), p["w_b_up"].astype(BF16))

    w_router = jnp.zeros((d, LANES), F32)
    w_router = w_router.at[:, :N_EXPERT_GROUPS].set(p["w_route_group"].astype(F32))
    w_router = w_router.at[:, ROUTE_EXPERT_LANE0:ROUTE_EXPERT_LANE0 + N_EXPERTS].set(
        p["w_route_expert"].astype(F32))
    r_hi = w_router.astype(BF16)
    r_lo = (w_router - r_hi.astype(F32)).astype(BF16)
    r_b = jnp.zeros((1, LANES), F32)
    r_b = r_b.at[0, :N_EXPERT_GROUPS].set(p["b_route_group"].astype(F32))
    r_b = r_b.at[0, ROUTE_EXPERT_LANE0:ROUTE_EXPERT_LANE0 + N_EXPERTS].set(p["b_route_expert"].astype(F32))
    x1, h2, logits = _merge_out(gates, ya, yb, x, p["w_out"].astype(BF16), p["norm_ffn_w"],
                                r_hi, r_lo, r_b, tm)

    route = _route(logits, tm)
    expert_ids = route[:, ROUTE_ID_LANE:ROUTE_ID_LANE + TOP_K_INNER].astype(jnp.int32)
    bm = MOE_BLOCK
    tg = min(GATHER_TILE, t)
    pos, block_expert, n_used, pad_start, pad_count, n_rows = _dispatch_plan(expert_ids, bm)
    xs = _dispatch(h2, pos, pad_start, pad_count, n_used, n_rows, tg, bm)
    y_rows = _experts(xs, block_expert, n_used, p["w_exp_gate"], p["w_exp_up"], p["w_exp_down"], bm)
    return x1, y_rows, pos, route


def kernel(x, norm_mix_w, w_in, conv_w, conv_b, dt_bias, a_log, d_ssd, norm_ssd_w, w_a_up,
           s5_lambda_re, s5_lambda_im, s5_log_dt, s5_b_re, s5_b_im, s5_c_re, s5_c_im, s5_d,
           w_glu, w_b_up, gate_b, w_out, norm_ffn_w, w_route_group, b_route_group,
           w_route_expert, b_route_expert, w_exp_gate, w_exp_up, w_exp_down, norm_final_w):
    b, seq, d = x.shape
    assert b == 1, "the scans carry state along the flattened token axis"
    depth = w_in.shape[0]
    per_layer = dict(norm_mix_w=norm_mix_w, w_in=w_in, conv_w=conv_w, conv_b=conv_b, dt_bias=dt_bias,
                     a_log=a_log, d_ssd=d_ssd, norm_ssd_w=norm_ssd_w, w_a_up=w_a_up,
                     s5_lambda_re=s5_lambda_re, s5_lambda_im=s5_lambda_im, s5_log_dt=s5_log_dt,
                     s5_b_re=s5_b_re, s5_b_im=s5_b_im, s5_c_re=s5_c_re, s5_c_im=s5_c_im, s5_d=s5_d,
                     w_glu=w_glu, w_b_up=w_b_up, gate_b=gate_b, w_out=w_out, norm_ffn_w=norm_ffn_w,
                     w_route_group=w_route_group, b_route_group=b_route_group,
                     w_route_expert=w_route_expert, b_route_expert=b_route_expert,
                     w_exp_gate=w_exp_gate, w_exp_up=w_exp_up, w_exp_down=w_exp_down)
    xt = x.reshape(b * seq, d)
    tg = min(GATHER_TILE, b * seq)
    for i in range(depth):
        p = {k: v[i] for k, v in per_layer.items()}
        x1, y_rows, pos, route = _layer(xt, p)
        xt = _combine(x1, y_rows, pos, route, norm_final_w, tg, normalize=(i == depth - 1))
    return xt.reshape(b, seq, d)
```

```python
import functools
import math

import jax
import jax.numpy as jnp
from jax import lax
from jax.experimental import pallas as pl
from jax.experimental.pallas import tpu as pltpu

F32 = jnp.float32
BF16 = jnp.bfloat16

SSD_HEAD_DIM = 64
SSD_GROUPS = 8
SSD_STATE = 128
CONV_WIDTH = 4
S5_GROUP_CH = 16
S5_STATE = 64
N_EXPERT_GROUPS = 4
EXPERTS_PER_GROUP = 8
N_EXPERTS = N_EXPERT_GROUPS * EXPERTS_PER_GROUP
TOP_K_INNER = 2
RMS_EPS = 1e-6

LANES = 128
SUBLANES = 8
VMEM_LIMIT_BYTES = 52 * 1024 * 1024

ROW_TILE = 512
MM_ROW_TILE = 1024
MM_COL_TILE = 1024
SSD_CHUNK = 128
S5_CHUNK = 16
S5_SUPER = S5_GROUP_CH * 8
S5_ROWS = 256
MOE_BLOCK = 256
GATHER_TILE = 256
DMA_UNROLL = 8


def _cparams(sem, vmem=VMEM_LIMIT_BYTES):
    return pltpu.CompilerParams(dimension_semantics=sem, vmem_limit_bytes=vmem)


def _sigmoid(v):
    return 1.0 / (1.0 + jnp.exp(-v))


def _silu(v):
    return v * _sigmoid(v)


def _softplus(v):
    return jnp.maximum(v, 0.0) + jnp.log(1.0 + jnp.exp(-jnp.abs(v)))


def _gelu_tanh(v):
    c = math.sqrt(2.0 / math.pi)
    return 0.5 * v * (1.0 + jnp.tanh(c * (v + 0.044715 * (v * v * v))))


def _rms(v, w):
    ms = jnp.mean(v * v, axis=-1, keepdims=True)
    return v * lax.rsqrt(ms + RMS_EPS) * w


def _pack_bf16_pairs(v):
    n = v.shape[1] // 2
    lo = pltpu.bitcast(v[:, :n].astype(BF16).astype(F32), jnp.uint32)
    hi = pltpu.bitcast(v[:, n:].astype(BF16).astype(F32), jnp.uint32)
    return hi | (lo >> 16)


def _unpack_bf16_pairs(w):
    lo = pltpu.bitcast(w << 16, F32)
    hi = pltpu.bitcast(w & jnp.uint32(0xFFFF0000), F32)
    return jnp.concatenate([lo, hi], axis=1)


def _rmsnorm_kernel(x_ref, w_ref, o_ref):
    o_ref[...] = _rms(x_ref[...], w_ref[...]).astype(o_ref.dtype)


def _rmsnorm(x, w, tm):
    t, d = x.shape
    return pl.pallas_call(
        _rmsnorm_kernel,
        grid=(t // tm,),
        in_specs=[pl.BlockSpec((tm, d), lambda i: (i, 0)),
                  pl.BlockSpec((1, d), lambda i: (0, 0))],
        out_specs=pl.BlockSpec((tm, d), lambda i: (i, 0)),
        out_shape=jax.ShapeDtypeStruct((t, d), BF16),
        compiler_params=_cparams(("parallel",)),
        name="rmsnorm",
    )(x, w.reshape(1, d))


def _resident_weight(w_ref, wbf_ref):
    @pl.when(pl.program_id(1) == 0)
    def _():
        wbf_ref[...] = w_ref[...].astype(BF16)
    return wbf_ref


def _proj_kernel(a_ref, w_ref, b_ref, o_ref, *scratch, act):
    w = _resident_weight(w_ref, scratch[0]) if scratch else w_ref
    p = jnp.dot(a_ref[...], w[...], preferred_element_type=F32)
    if act == "silu":
        p = _silu(p)
    elif act == "sigmoid_bias":
        p = _sigmoid(p + b_ref[...])
    o_ref[...] = p.astype(o_ref.dtype)


def _proj(h, w, col0, n, b, act, tm, tn):
    t, k = h.shape
    tn = min(tn, n)
    assert col0 % tn == 0 and n % tn == 0
    jb = col0 // tn
    if b is None:
        b = jnp.zeros((1, n), F32)
    scratch = [pltpu.VMEM((k, tn), BF16)] if w.dtype != BF16 else []
    return pl.pallas_call(
        functools.partial(_proj_kernel, act=act),
        grid=(n // tn, t // tm),
        in_specs=[pl.BlockSpec((tm, k), lambda j, i: (i, 0)),
                  pl.BlockSpec((k, tn), lambda j, i: (0, jb + j)),
                  pl.BlockSpec((1, tn), lambda j, i: (0, j))],
        out_specs=pl.BlockSpec((tm, tn), lambda j, i: (i, j)),
        out_shape=jax.ShapeDtypeStruct((t, n), BF16),
        scratch_shapes=scratch,
        compiler_params=_cparams(("parallel", "arbitrary")),
        name="proj_" + act,
    )(h, w, b.reshape(1, n))


def _conv_proj_kernel(a_ref, w_ref, cw_ref, cb_ref, o_ref, ext_ref, wbf_ref):
    tm = a_ref.shape[0]
    halo = SUBLANES

    @pl.when(pl.program_id(1) == 0)
    def _():
        ext_ref[pl.ds(0, halo), :] = jnp.zeros((halo, ext_ref.shape[1]), F32)

    w = _resident_weight(w_ref, wbf_ref)
    p = jnp.dot(a_ref[...], w[...], preferred_element_type=F32)
    ext_ref[pl.ds(halo, tm), :] = p
    acc = cb_ref[...] + cw_ref[CONV_WIDTH - 1:CONV_WIDTH, :] * p
    for k in range(CONV_WIDTH - 1):
        back = CONV_WIDTH - 1 - k
        acc = acc + cw_ref[k:k + 1, :] * ext_ref[pl.ds(halo - back, tm), :]
    o_ref[...] = _silu(acc).astype(o_ref.dtype)
    ext_ref[pl.ds(0, halo), :] = p[tm - halo:, :]


def _conv_proj(h, w, col0, n, conv_w, conv_b, tm, tn):
    t, k = h.shape
    assert col0 % tn == 0 and n % tn == 0
    jb = col0 // tn
    return pl.pallas_call(
        _conv_proj_kernel,
        grid=(n // tn, t // tm),
        in_specs=[pl.BlockSpec((tm, k), lambda j, i: (i, 0)),
                  pl.BlockSpec((k, tn), lambda j, i: (0, jb + j)),
                  pl.BlockSpec((CONV_WIDTH, tn), lambda j, i: (0, j)),
                  pl.BlockSpec((1, tn), lambda j, i: (0, j))],
        out_specs=pl.BlockSpec((tm, tn), lambda j, i: (i, j)),
        out_shape=jax.ShapeDtypeStruct((t, n), BF16),
        scratch_shapes=[pltpu.VMEM((tm + SUBLANES, tn), F32),
                        pltpu.VMEM((k, tn), BF16)],
        compiler_params=_cparams(("parallel", "arbitrary")),
        name="proj_conv",
    )(h, w, conv_w, conv_b.reshape(1, n))


def _dt_proj_kernel(a_ref, w_ref, b_ref, dt_ref, dtt_ref):
    p = jnp.dot(a_ref[...], w_ref[...].astype(BF16), preferred_element_type=F32) + b_ref[...]
    dt = _softplus(p)
    dt_ref[...] = dt
    dtt_ref[...] = dt.T


def _dt_proj(h, w, col0, b_pad, tm):
    t, k = h.shape
    jb = col0 // LANES
    return pl.pallas_call(
        _dt_proj_kernel,
        grid=(t // tm,),
        in_specs=[pl.BlockSpec((tm, k), lambda i: (i, 0)),
                  pl.BlockSpec((k, LANES), lambda i: (0, jb)),
                  pl.BlockSpec((1, LANES), lambda i: (0, 0))],
        out_specs=[pl.BlockSpec((tm, LANES), lambda i: (i, 0)),
                   pl.BlockSpec((LANES, tm), lambda i: (0, i))],
        out_shape=[jax.ShapeDtypeStruct((t, LANES), F32),
                   jax.ShapeDtypeStruct((LANES, t), F32)],
        compiler_params=_cparams(("parallel",)),
        name="proj_dt",
    )(h, w, b_pad)


def _step_major_perm(n_rows):
    assert n_rows == S5_CHUNK * S5_CHUNK
    shift = S5_CHUNK.bit_length() - 1
    row = lax.broadcasted_iota(jnp.int32, (n_rows, n_rows), 0)
    col = lax.broadcasted_iota(jnp.int32, (n_rows, n_rows), 1)
    swapped = ((row & (S5_CHUNK - 1)) << shift) | (row >> shift)
    return jnp.where(col == swapped, 1.0, 0.0).astype(BF16)


def _chunk_proj_kernel(a_ref, w_ref, o_ref):
    u = jnp.dot(a_ref[...], w_ref[...], preferred_element_type=F32).astype(BF16)
    u_steps = jnp.dot(_step_major_perm(u.shape[0]), u, preferred_element_type=F32).astype(o_ref.dtype)
    nj = o_ref.shape[1]
    for s in range(o_ref.shape[0]):
        o_ref[s] = u_steps[s * nj:(s + 1) * nj, :]


def _chunk_proj(h, w, col0, n):
    t, k = h.shape
    nj = t // S5_CHUNK
    tm = S5_CHUNK * S5_CHUNK
    assert col0 % n == 0
    jb = col0 // n
    return pl.pallas_call(
        _chunk_proj_kernel,
        grid=(t // tm,),
        in_specs=[pl.BlockSpec((tm, k), lambda i: (i, 0)),
                  pl.BlockSpec((k, n), lambda i: (0, jb))],
        out_specs=pl.BlockSpec((S5_CHUNK, S5_CHUNK, n), lambda i: (0, i, 0)),
        out_shape=jax.ShapeDtypeStruct((S5_CHUNK, nj, n), BF16),
        compiler_params=_cparams(("parallel",)),
        name="proj_u",
    )(h, w)


def _cumsum_rows(v):
    n = v.shape[0]
    idx = lax.broadcasted_iota(jnp.int32, v.shape, 0)
    k = 1
    while k < n:
        v = v + jnp.where(idx >= k, pltpu.roll(v, k, 0), 0.0)
        k *= 2
    return v


def _cumsum_lanes(v):
    n = v.shape[1]
    idx = lax.broadcasted_iota(jnp.int32, v.shape, 1)
    k = 1
    while k < n:
        v = v + jnp.where(idx >= k, pltpu.roll(v, k, 1), 0.0)
        k *= 2
    return v


def _ssd_kernel(xbc_ref, dt_ref, dtt_ref, alog_r_ref, alog_c_ref, dskip_ref, o_ref, state_ref, *,
                n_heads):
    q = dt_ref.shape[0]
    n = SSD_STATE
    p_dim = SSD_HEAD_DIM
    r_heads = n_heads // SSD_GROUPS
    gw = r_heads * p_dim
    inner = n_heads * p_dim

    @pl.when(pl.program_id(0) == 0)
    def _():
        state_ref[...] = jnp.zeros(state_ref.shape, F32)

    dtt = dtt_ref[...]
    cs_col = _cumsum_rows(dt_ref[...] * -jnp.exp(alog_r_ref[...]))
    cs_row = _cumsum_lanes(dtt * -jnp.exp(alog_c_ref[...]))
    causal = lax.broadcasted_iota(jnp.int32, (q, q), 0) >= lax.broadcasted_iota(jnp.int32, (q, q), 1)
    head_of_lane = lax.broadcasted_iota(jnp.int32, (1, gw), 1) // p_dim
    eye = jnp.where(lax.broadcasted_iota(jnp.int32, (n, n), 0) == lax.broadcasted_iota(jnp.int32, (n, n), 1),
                    1.0, 0.0).astype(BF16)
    nt = (((1,), (1,)), ((), ()))

    for g in range(SSD_GROUPS):
        x_g = xbc_ref[:, g * gw:(g + 1) * gw]
        b_g = xbc_ref[:, inner + g * n:inner + (g + 1) * n]
        c_g = xbc_ref[:, inner + (SSD_GROUPS + g) * n:inner + (SSD_GROUPS + g + 1) * n]
        cb = lax.dot_general(c_g, b_g, nt, preferred_element_type=F32)
        b_t = lax.dot_general(eye, b_g, nt, preferred_element_type=F32)
        c_f = c_g.astype(F32)
        s_g = state_ref[g]
        lhs_parts, bt_parts, cd = [], [], jnp.zeros((1, gw), F32)
        for r in range(r_heads):
            h = g * r_heads + r
            csb = jnp.broadcast_to(cs_col[:, h:h + 1], (q, n))
            csr = cs_row[h:h + 1, :]
            dtr = dtt[h:h + 1, :]
            cs_last = csr[:, q - 1:q]
            decay = jnp.exp(jnp.where(causal, csb - csr, -1e30))
            lhs_parts.append(jnp.concatenate([cb * decay * dtr, c_f * jnp.exp(csb)], axis=1).astype(BF16))
            bt_parts.append((b_t * (dtr * jnp.exp(cs_last - csr))).astype(BF16))
            cd = jnp.where(head_of_lane == r, jnp.exp(cs_last), cd)
        rhs = jnp.concatenate([x_g, s_g.astype(BF16)], axis=0)
        y_all = jnp.dot(jnp.concatenate(lhs_parts, axis=0), rhs, preferred_element_type=F32)
        s_all = jnp.dot(jnp.concatenate(bt_parts, axis=0), x_g, preferred_element_type=F32)
        y_g = jnp.zeros((q, gw), F32)
        s_new = jnp.zeros((n, gw), F32)
        for r in range(r_heads):
            mine = head_of_lane == r
            y_g = jnp.where(mine, y_all[r * q:(r + 1) * q, :], y_g)
            s_new = jnp.where(mine, s_all[r * n:(r + 1) * n, :], s_new)
        state_ref[g] = s_g * cd + s_new
        y_g = y_g + x_g.astype(F32) * dskip_ref[:, g * gw:(g + 1) * gw]
        o_ref[:, g * gw:(g + 1) * gw] = y_g.astype(o_ref.dtype)


def _ssd(xbc, dt, dtt, a_log, d_skip, n_heads, q):
    t, width = xbc.shape
    inner = n_heads * SSD_HEAD_DIM
    gw = inner // SSD_GROUPS
    assert n_heads % SUBLANES == 0 and n_heads <= LANES
    alog_r = jnp.zeros((1, LANES), F32).at[0, :n_heads].set(a_log)
    alog_c = a_log.reshape(n_heads, 1)
    dskip = jnp.repeat(d_skip, SSD_HEAD_DIM).reshape(1, inner)
    return pl.pallas_call(
        functools.partial(_ssd_kernel, n_heads=n_heads),
        grid=(t // q,),
        in_specs=[pl.BlockSpec((q, width), lambda c: (c, 0)),
                  pl.BlockSpec((q, LANES), lambda c: (c, 0)),
                  pl.BlockSpec((n_heads, q), lambda c: (0, c)),
                  pl.BlockSpec((1, LANES), lambda c: (0, 0)),
                  pl.BlockSpec((n_heads, 1), lambda c: (0, 0)),
                  pl.BlockSpec((1, inner), lambda c: (0, 0))],
        out_specs=pl.BlockSpec((q, inner), lambda c: (c, 0)),
        out_shape=jax.ShapeDtypeStruct((t, inner), BF16),
        scratch_shapes=[pltpu.VMEM((SSD_GROUPS, SSD_STATE, gw), F32)],
        compiler_params=_cparams(("arbitrary",)),
        name="ssd_scan",
    )(xbc, dt, dtt, alog_r, alog_c, dskip)


def _gated_up_kernel(y_ref, z_ref, nw_ref, w_ref, o_ref):
    v = y_ref[...].astype(F32) * z_ref[...].astype(F32)
    na = _rms(v, nw_ref[...]).astype(BF16)
    o_ref[...] = jnp.dot(na, w_ref[...], preferred_element_type=F32).astype(o_ref.dtype)


def _gated_up(y, zs, norm_w, w, tm):
    t, d = y.shape
    n = w.shape[1]
    return pl.pallas_call(
        _gated_up_kernel,
        grid=(t // tm,),
        in_specs=[pl.BlockSpec((tm, d), lambda i: (i, 0)),
                  pl.BlockSpec((tm, d), lambda i: (i, 0)),
                  pl.BlockSpec((1, d), lambda i: (0, 0)),
                  pl.BlockSpec((d, n), lambda i: (0, 0))],
        out_specs=pl.BlockSpec((tm, n), lambda i: (i, 0)),
        out_shape=jax.ShapeDtypeStruct((t, n), BF16),
        compiler_params=_cparams(("parallel",)),
        name="ssd_gated_up",
    )(y, zs, norm_w.reshape(1, d), w)


def _s5_operators(lam_re, lam_im, log_dt, b_re, b_im, c_re, c_im):
    ng, ns = lam_re.shape
    nc = S5_GROUP_CH
    L = S5_CHUNK
    per = S5_SUPER // nc
    nsg = ng // per
    lr, li = lam_re.astype(F32), lam_im.astype(F32)
    dt = jnp.exp(log_dt.astype(F32))[:, None]
    mag = jnp.exp(lr * dt)
    ang = li * dt
    abar_r, abar_i = mag * jnp.cos(ang), mag * jnp.sin(ang)
    den = lr * lr + li * li
    nr, ni = abar_r - 1.0, abar_i
    coef_r = (nr * lr + ni * li) / den
    coef_i = (ni * lr - nr * li) / den
    bre, bim = b_re.astype(F32), b_im.astype(F32)
    bb_r = coef_r[..., None] * bre - coef_i[..., None] * bim
    bb_i = coef_r[..., None] * bim + coef_i[..., None] * bre
    cre, cim = c_re.astype(F32), c_im.astype(F32)
    ks = jnp.arange(L + 1, dtype=F32)[:, None, None]
    pmag = jnp.exp(ks * (lr * dt)[None])
    pang = ks * ang[None]
    pw_r, pw_i = pmag * jnp.cos(pang), pmag * jnp.sin(pang)
    ca_r = cre[None] * pw_r[:, :, None, :] - cim[None] * pw_i[:, :, None, :]
    ca_i = cre[None] * pw_i[:, :, None, :] + cim[None] * pw_r[:, :, None, :]
    def _rows_n(v, steps):
        return v.reshape(steps, nsg, per, nc, ns).transpose(1, 4, 0, 2, 3).reshape(nsg, ns, steps * per * nc)

    uc = jnp.concatenate([_rows_n(ca_r, L + 1), -_rows_n(ca_i, L + 1)], axis=1)
    ks_rev = (L - 1) - jnp.arange(L, dtype=F32)[:, None, None]
    rmag = jnp.exp(ks_rev * (lr * dt)[None])
    rang = ks_rev * ang[None]
    rev_r, rev_i = rmag * jnp.cos(rang), rmag * jnp.sin(rang)
    ab_r = rev_r[..., None] * bb_r[None] - rev_i[..., None] * bb_i[None]
    ab_i = rev_r[..., None] * bb_i[None] + rev_i[..., None] * bb_r[None]
    ab_rt, ab_it = ab_r.transpose(0, 1, 3, 2), ab_i.transpose(0, 1, 3, 2)
    wc = jnp.concatenate([_rows_n(ab_rt, L), _rows_n(ab_it, L)], axis=1)

    def _rows_gc(v):
        return v.reshape(nsg, per, ns, nc).transpose(0, 1, 3, 2).reshape(nsg, per * nc, ns)

    bbt = jnp.concatenate([_rows_gc(bb_r), _rows_gc(bb_i)], axis=2)
    a_chunk = jnp.concatenate([pw_r[L].reshape(nsg, 1, per * ns),
                               pw_i[L].reshape(nsg, 1, per * ns)], axis=2)
    return uc, wc, bbt, a_chunk


def _split_bf16(v):
    hi = v.astype(BF16)
    return hi, (v - hi.astype(F32)).astype(BF16)


def _s5_kernel(u_ref, uc_ref, wc_ref, bbt_ref, ach_ref, dsk_ref, o_ref,
               toep_ref, wt_ref, v_ref, x_ref, sp_ref, carry_ref):
    jb = pl.program_id(1)
    nl, nj, cw = u_ref.shape
    half = carry_ref.shape[1] // 2
    per = cw // S5_GROUP_CH
    ns = half // per
    nt = (((1,), (1,)), ((), ()))

    @pl.when(jb == 0)
    def _():
        carry_ref[...] = jnp.zeros(carry_ref.shape, F32)
        uc = uc_ref[...]
        wc = wc_ref[...]
        b_hi, b_lo = _split_bf16(bbt_ref[...])
        u_hi, u_lo = _split_bf16(uc[:, :nl * cw])
        kall = (jnp.dot(b_hi, u_hi, preferred_element_type=F32)
                + jnp.dot(b_lo, u_hi, preferred_element_type=F32)
                + jnp.dot(b_hi, u_lo, preferred_element_type=F32))
        row_g = lax.broadcasted_iota(jnp.int32, (cw, 1), 0) // S5_GROUP_CH
        col_g = (lax.broadcasted_iota(jnp.int32, (1, nl * cw), 1) // S5_GROUP_CH) % per
        kall = jnp.where(row_g == col_g, kall, 0.0).astype(BF16)
        toep_ref[...] = jnp.zeros(toep_ref.shape, toep_ref.dtype)
        for s_in in range(nl):
            for s_out in range(s_in, nl):
                k = s_out - s_in
                toep_ref[pl.ds(s_in * cw, cw), pl.ds(s_out * cw, cw)] = kall[:, k * cw:(k + 1) * cw]
        for gp in range(per):
            mine = col_g == gp
            for part in range(2):
                rows = pl.ds(part * half + gp * ns, ns)
                src = slice(part * ns, (part + 1) * ns)
                v_ref[rows, :] = jnp.where(mine, uc[src, cw:], 0.0).astype(BF16)
                wt_ref[rows, :] = jnp.where(mine, wc[src, :], 0.0).astype(BF16)

    lhs = jnp.concatenate([u_ref[s] for s in range(nl)], axis=1)
    x_ref[...] = lax.dot_general(lhs, wt_ref[...], nt, preferred_element_type=F32)

    a_re = ach_ref[:, :half]
    a_im = ach_ref[:, half:]

    def step(j, carry):
        s_re, s_im = carry
        sp_ref[pl.ds(j, 1), :half] = s_re
        sp_ref[pl.ds(j, 1), half:] = s_im
        xr = x_ref[pl.ds(j, 1), :half]
        xi = x_ref[pl.ds(j, 1), half:]
        return (a_re * s_re - a_im * s_im + xr, a_re * s_im + a_im * s_re + xi)

    s_re, s_im = lax.fori_loop(0, nj, step, (carry_ref[:, :half], carry_ref[:, half:]), unroll=8)
    carry_ref[:, :half] = s_re
    carry_ref[:, half:] = s_im

    y_state = jnp.dot(sp_ref[...].astype(BF16), v_ref[...], preferred_element_type=F32)
    dsk = dsk_ref[...]
    pair = 2 * cw
    for tp in range(nl // 2):
        kdim = pair * (tp + 1)
        y = jnp.dot(lhs[:, :kdim], toep_ref[pl.ds(0, kdim), pl.ds(tp * pair, pair)],
                    preferred_element_type=F32)
        y = y + y_state[:, tp * pair:(tp + 1) * pair]
        for h in range(2):
            s = 2 * tp + h
            o_ref[s] = (y[:, h * cw:(h + 1) * cw] + dsk * u_ref[s].astype(F32)).astype(o_ref.dtype)


def _s5(u_steps, uc, wc, bbt, a_chunk, d_skip, tj):
    nl, nj, width = u_steps.shape
    nsg = width // S5_SUPER
    nstate = a_chunk.shape[2]
    rows = uc.shape[1]
    return pl.pallas_call(
        _s5_kernel,
        grid=(nsg, nj // tj),
        in_specs=[pl.BlockSpec((nl, tj, S5_SUPER), lambda g, j: (0, j, g)),
                  pl.BlockSpec((None, rows, (nl + 1) * S5_SUPER), lambda g, j: (g, 0, 0)),
                  pl.BlockSpec((None, rows, nl * S5_SUPER), lambda g, j: (g, 0, 0)),
                  pl.BlockSpec((None, S5_SUPER, rows), lambda g, j: (g, 0, 0)),
                  pl.BlockSpec((None, 1, nstate), lambda g, j: (g, 0, 0)),
                  pl.BlockSpec((None, 1, S5_SUPER), lambda g, j: (g, 0, 0))],
        out_specs=pl.BlockSpec((nl, tj, S5_SUPER), lambda g, j: (0, j, g)),
        out_shape=jax.ShapeDtypeStruct((nl, nj, width), BF16),
        scratch_shapes=[pltpu.VMEM((nl * S5_SUPER, nl * S5_SUPER), BF16),
                        pltpu.VMEM((nstate, nl * S5_SUPER), BF16),
                        pltpu.VMEM((nstate, nl * S5_SUPER), BF16),
                        pltpu.VMEM((tj, nstate), F32),
                        pltpu.VMEM((tj, nstate), F32),
                        pltpu.VMEM((1, nstate), F32)],
        compiler_params=_cparams(("parallel", "arbitrary")),
        name="s5_scan",
    )(u_steps, uc, wc, bbt, a_chunk, d_skip.reshape(nsg, 1, S5_SUPER))


def _glu_up_kernel(y_ref, wg_ref, wu_ref, o_ref):
    y_steps = jnp.concatenate([y_ref[s] for s in range(y_ref.shape[0])], axis=0)
    y = jnp.dot(_step_major_perm(y_steps.shape[0]), y_steps, preferred_element_type=F32)
    v = _gelu_tanh(y)
    gate = _sigmoid(jnp.dot(v.astype(BF16), wg_ref[...], preferred_element_type=F32))
    o_ref[...] = jnp.dot((v * gate).astype(BF16), wu_ref[...],
                         preferred_element_type=F32).astype(o_ref.dtype)


def _glu_up(y_steps, w_glu, w_up):
    nl, nj, width = y_steps.shape
    n = w_up.shape[1]
    tm = S5_CHUNK * S5_CHUNK
    return pl.pallas_call(
        _glu_up_kernel,
        grid=(nj // S5_CHUNK,),
        in_specs=[pl.BlockSpec((nl, S5_CHUNK, width), lambda i: (0, i, 0)),
                  pl.BlockSpec((width, width), lambda i: (0, 0)),
                  pl.BlockSpec((width, n), lambda i: (0, 0))],
        out_specs=pl.BlockSpec((tm, n), lambda i: (i, 0)),
        out_shape=jax.ShapeDtypeStruct((nj * nl, n), BF16),
        compiler_params=_cparams(("parallel",)),
        name="s5_glu_up",
    )(y_steps, w_glu, w_up)


def _merge_out_kernel(ga_ref, gb_ref, ya_ref, yb_ref, x_ref, w_ref, nw_ref, rhi_ref, rlo_ref, rb_ref,
                      x1_ref, h2_ref, lg_ref):
    merged = (ga_ref[...].astype(F32) * ya_ref[...].astype(F32)
              + gb_ref[...].astype(F32) * yb_ref[...].astype(F32))
    x1 = x_ref[...] + jnp.dot(merged.astype(BF16), w_ref[...], preferred_element_type=F32)
    x1_ref[...] = x1
    h2 = _rms(x1, nw_ref[...])
    h2_ref[...] = _pack_bf16_pairs(h2)
    hi = h2.astype(BF16)
    lo = (h2 - hi.astype(F32)).astype(BF16)
    lg = (jnp.dot(hi, rhi_ref[...], preferred_element_type=F32)
          + jnp.dot(lo, rhi_ref[...], preferred_element_type=F32)
          + jnp.dot(hi, rlo_ref[...], preferred_element_type=F32))
    lg_ref[...] = lg + rb_ref[...]


def _merge_out(gates, ya, yb, x, w_out, norm_w, r_hi, r_lo, r_b, tm):
    t, d = x.shape
    row = lambda i: (i, 0)
    full = lambda i: (0, 0)
    return pl.pallas_call(
        _merge_out_kernel,
        grid=(t // tm,),
        in_specs=[pl.BlockSpec((tm, d), row),
                  pl.BlockSpec((tm, d), lambda i: (i, 1)),
                  pl.BlockSpec((tm, d), row),
                  pl.BlockSpec((tm, d), row),
                  pl.BlockSpec((tm, d), row),
                  pl.BlockSpec((d, d), full),
                  pl.BlockSpec((1, d), full),
                  pl.BlockSpec((d, LANES), full),
                  pl.BlockSpec((d, LANES), full),
                  pl.BlockSpec((1, LANES), full)],
        out_specs=[pl.BlockSpec((tm, d), row),
                   pl.BlockSpec((tm, d // 2), row),
                   pl.BlockSpec((tm, LANES), row)],
        out_shape=[jax.ShapeDtypeStruct((t, d), F32),
                   jax.ShapeDtypeStruct((t, d // 2), jnp.uint32),
                   jax.ShapeDtypeStruct((t, LANES), F32)],
        compiler_params=_cparams(("parallel",)),
        name="merge_out",
    )(gates, gates, ya, yb, x, w_out, norm_w.reshape(1, d), r_hi, r_lo, r_b)


ROUTE_ID_LANE = 0
ROUTE_W_LANE = TOP_K_INNER
ROUTE_EXPERT_LANE0 = SUBLANES


def _route_kernel(lg_ref, o_ref):
    lg = lg_ref[...]
    lane = lax.broadcasted_iota(jnp.int32, lg.shape, 1)
    neg = -jnp.inf
    big = LANES
    is_g = lane < N_EXPERT_GROUPS
    gl = jnp.where(is_g, lg, neg)
    gmax = jnp.max(gl, axis=1, keepdims=True)
    grp = jnp.min(jnp.where(gl == gmax, lane, big), axis=1, keepdims=True)
    pg_sel = 1.0 / jnp.sum(jnp.where(is_g, jnp.exp(lg - gmax), 0.0), axis=1, keepdims=True)
    e_lo = ROUTE_EXPERT_LANE0 + grp * EXPERTS_PER_GROUP
    in_grp = (lane >= e_lo) & (lane < e_lo + EXPERTS_PER_GROUP)
    el = jnp.where(in_grp, lg, neg)
    v0 = jnp.max(el, axis=1, keepdims=True)
    i0 = jnp.min(jnp.where(el == v0, lane, big), axis=1, keepdims=True)
    el1 = jnp.where(lane == i0, neg, el)
    v1 = jnp.max(el1, axis=1, keepdims=True)
    i1 = jnp.min(jnp.where(el1 == v1, lane, big), axis=1, keepdims=True)
    e1w = jnp.exp(v1 - v0)
    w0 = pg_sel / (1.0 + e1w)
    w1 = pg_sel * e1w / (1.0 + e1w)
    out = jnp.where(lane == ROUTE_ID_LANE, (i0 - ROUTE_EXPERT_LANE0).astype(F32),
          jnp.where(lane == ROUTE_ID_LANE + 1, (i1 - ROUTE_EXPERT_LANE0).astype(F32),
          jnp.where(lane == ROUTE_W_LANE, w0, jnp.where(lane == ROUTE_W_LANE + 1, w1, 0.0))))
    o_ref[...] = out


def _route(logits, tm):
    t = logits.shape[0]
    return pl.pallas_call(
        _route_kernel,
        grid=(t // tm,),
        in_specs=[pl.BlockSpec((tm, LANES), lambda i: (i, 0))],
        out_specs=pl.BlockSpec((tm, LANES), lambda i: (i, 0)),
        out_shape=jax.ShapeDtypeStruct((t, LANES), F32),
        compiler_params=_cparams(("parallel",)),
        name="route",
    )(logits)


def _dispatch_plan(expert_ids, bm):
    n_tokens = expert_ids.shape[0]
    n_assign = n_tokens * TOP_K_INNER
    eid = expert_ids.reshape(n_assign)
    experts = jnp.arange(N_EXPERTS, dtype=jnp.int32)
    onehot = (eid[:, None] == experts[None, :]).astype(jnp.int32)
    csum = jnp.cumsum(onehot, axis=0)
    counts = csum[-1]
    rank = jnp.sum(onehot * csum, axis=1) - 1
    padded = ((counts + bm - 1) // bm) * bm
    ends = jnp.cumsum(padded)
    starts = ends - padded
    dest = jnp.sum(onehot * starts[None, :], axis=1) + rank
    n_rows = n_assign + N_EXPERTS * bm
    nb = n_rows // bm
    block_start = jnp.arange(nb, dtype=jnp.int32) * bm
    block_expert = jnp.minimum(jnp.sum((ends[None, :] <= block_start[:, None]).astype(jnp.int32), axis=1),
                               N_EXPERTS - 1)
    n_used = (ends[-1] // bm).astype(jnp.int32).reshape(1)
    return (dest.reshape(n_tokens, TOP_K_INNER), block_expert, n_used,
            (starts + counts).astype(jnp.int32), (padded - counts).astype(jnp.int32), n_rows)


def _dispatch_kernel(pad_start_ref, pad_count_ref, nused_ref, dest_ref, h_ref, xs_hbm, stage, zrow, sem, zsem):
    i = pl.program_id(0)
    n_steps = pl.num_programs(0)
    tm = h_ref.shape[0]
    bm = zrow.shape[0]
    nb = xs_hbm.shape[0] // bm
    slot = i % 2

    def tile_wait(s):
        for _ in range(TOP_K_INNER):
            pltpu.make_async_copy(stage.at[s], xs_hbm.at[pl.ds(0, tm), :], sem.at[s]).wait()

    @pl.when(i >= 2)
    def _():
        tile_wait(slot)

    stage[slot] = h_ref[...]

    def issue(r, carry):
        for k in range(TOP_K_INNER):
            pltpu.make_async_copy(stage.at[slot, pl.ds(r, 1), :],
                                  xs_hbm.at[pl.ds(dest_ref[k, r], 1), :], sem.at[slot]).start()
        return carry
    lax.fori_loop(0, tm, issue, 0, unroll=DMA_UNROLL)

    @pl.when(i == n_steps - 1)
    def _():
        tile_wait(slot)

        @pl.when(n_steps >= 2)
        def _():
            tile_wait(1 - slot)

        zrow[...] = jnp.zeros(zrow.shape, zrow.dtype)

        def pad_copy(e, r):
            return pltpu.make_async_copy(zrow.at[pl.ds(0, 1), :],
                                         xs_hbm.at[pl.ds(pad_start_ref[e] + r, 1), :], zsem)
        def block_copy(b):
            return pltpu.make_async_copy(zrow, xs_hbm.at[pl.ds(b * bm, bm), :], zsem)
        for e in range(N_EXPERTS):
            lax.fori_loop(0, pad_count_ref[e], lambda r, c, e=e: (pad_copy(e, r).start(), c)[1], 0)
        lax.fori_loop(nused_ref[0], nb, lambda b, c: (block_copy(b).start(), c)[1], 0)
        for e in range(N_EXPERTS):
            lax.fori_loop(0, pad_count_ref[e], lambda r, c, e=e: (pad_copy(e, r).wait(), c)[1], 0)
        lax.fori_loop(nused_ref[0], nb, lambda b, c: (block_copy(b).wait(), c)[1], 0)


def _dispatch(h2, dest, pad_start, pad_count, n_used, n_rows, tm, bm):
    t, d = h2.shape
    nt = t // tm
    dest_t = dest.reshape(nt, tm, TOP_K_INNER).transpose(0, 2, 1)
    grid_spec = pltpu.PrefetchScalarGridSpec(
        num_scalar_prefetch=3,
        grid=(nt,),
        in_specs=[pl.BlockSpec((None, TOP_K_INNER, tm), lambda i, *_: (i, 0, 0), memory_space=pltpu.SMEM),
                  pl.BlockSpec((tm, d), lambda i, *_: (i, 0))],
        out_specs=pl.BlockSpec(memory_space=pl.ANY),
        scratch_shapes=[pltpu.VMEM((2, tm, d), h2.dtype),
                        pltpu.VMEM((bm, d), h2.dtype),
                        pltpu.SemaphoreType.DMA((2,)),
                        pltpu.SemaphoreType.DMA(())],
    )
    return pl.pallas_call(
        _dispatch_kernel,
        grid_spec=grid_spec,
        out_shape=jax.ShapeDtypeStruct((n_rows, d), h2.dtype),
        compiler_params=_cparams(("arbitrary",)),
        name="moe_dispatch",
    )(pad_start, pad_count, n_used, dest_t, h2)


def _experts_kernel(bexp_ref, nused_ref, xs_ref, wg_ref, wu_ref, wd_ref, o_ref, wg_s, wu_s, wd_s):
    b = pl.program_id(0)
    n_used = nused_ref[0]
    new_expert = (b == 0) | (bexp_ref[b] != bexp_ref[jnp.maximum(b - 1, 0)])

    @pl.when((b < n_used) & new_expert)
    def _():
        wg_s[...] = wg_ref[...].astype(BF16)
        wu_s[...] = wu_ref[...].astype(BF16)
        wd_s[...] = wd_ref[...].astype(BF16)

    @pl.when(b < n_used)
    def _():
        xb = _unpack_bf16_pairs(xs_ref[...]).astype(BF16)
        hg = jnp.dot(xb, wg_s[...], preferred_element_type=F32)
        hu = jnp.dot(xb, wu_s[...], preferred_element_type=F32)
        act = (_silu(hg) * hu).astype(BF16)
        o_ref[...] = _pack_bf16_pairs(jnp.dot(act, wd_s[...], preferred_element_type=F32))

    @pl.when(b >= n_used)
    def _():
        o_ref[...] = jnp.zeros(o_ref.shape, o_ref.dtype)


def _experts(xs, block_expert, n_used, w_g, w_u, w_d, bm):
    n_rows, dp = xs.shape
    d, ff = w_g.shape[1], w_g.shape[2]
    assert d == 2 * dp
    nb = n_rows // bm

    def used(b, nu):
        return jnp.minimum(b, jnp.maximum(nu[0] - 1, 0))

    grid_spec = pltpu.PrefetchScalarGridSpec(
        num_scalar_prefetch=2,
        grid=(nb,),
        in_specs=[pl.BlockSpec((bm, dp), lambda b, be, nu: (used(b, nu), 0)),
                  pl.BlockSpec((None, d, ff), lambda b, be, nu: (be[used(b, nu)], 0, 0)),
                  pl.BlockSpec((None, d, ff), lambda b, be, nu: (be[used(b, nu)], 0, 0)),
                  pl.BlockSpec((None, ff, d), lambda b, be, nu: (be[used(b, nu)], 0, 0))],
        out_specs=pl.BlockSpec((bm, dp), lambda b, be, nu: (b, 0)),
        scratch_shapes=[pltpu.VMEM((d, ff), BF16),
                        pltpu.VMEM((d, ff), BF16),
                        pltpu.VMEM((ff, d), BF16)],
    )
    return pl.pallas_call(
        _experts_kernel,
        grid_spec=grid_spec,
        out_shape=jax.ShapeDtypeStruct((n_rows, dp), xs.dtype),
        compiler_params=_cparams(("arbitrary",)),
        name="moe_experts",
    )(block_expert, n_used, xs, w_g, w_u, w_d)


def _combine_kernel(pos_ref, x1_ref, rt_ref, nw_ref, y_hbm, o_ref, ybuf, sem, *, normalize):
    i = pl.program_id(0)
    n = pl.num_programs(0)
    tm = x1_ref.shape[0]

    def start_tile(slot, which):
        def body(r, carry):
            for k in range(TOP_K_INNER):
                pltpu.make_async_copy(y_hbm.at[pl.ds(pos_ref[which, k, r], 1), :],
                                      ybuf.at[slot, k, pl.ds(r, 1), :], sem.at[slot]).start()
            return carry
        lax.fori_loop(0, tm, body, 0, unroll=DMA_UNROLL)

    def wait_tile(slot):
        for k in range(TOP_K_INNER):
            pltpu.make_async_copy(y_hbm.at[pl.ds(0, tm), :], ybuf.at[slot, k], sem.at[slot]).wait()

    slot = i % 2

    @pl.when(i == 0)
    def _():
        start_tile(0, 0)

    @pl.when(i + 1 < n)
    def _():
        start_tile(1 - slot, 1)

    wait_tile(slot)
    acc = x1_ref[...]
    for k in range(TOP_K_INNER):
        acc = acc + rt_ref[:, ROUTE_W_LANE + k:ROUTE_W_LANE + k + 1] * _unpack_bf16_pairs(ybuf[slot, k])
    o_ref[...] = _rms(acc, nw_ref[...]) if normalize else acc


def _combine(x1, y_rows, pos, route, norm_w, tm, normalize):
    t, d = x1.shape
    nt = t // tm
    pos_t = pos.reshape(nt, tm, TOP_K_INNER).transpose(0, 2, 1)
    pos_next = jnp.concatenate([pos_t[1:], pos_t[-1:]], axis=0)
    pos2 = jnp.stack([pos_t, pos_next], axis=1)
    return pl.pallas_call(
        functools.partial(_combine_kernel, normalize=normalize),
        grid=(nt,),
        in_specs=[pl.BlockSpec((None, 2, TOP_K_INNER, tm), lambda i: (i, 0, 0, 0), memory_space=pltpu.SMEM),
                  pl.BlockSpec((tm, d), lambda i: (i, 0)),
                  pl.BlockSpec((tm, LANES), lambda i: (i, 0)),
                  pl.BlockSpec((1, d), lambda i: (0, 0)),
                  pl.BlockSpec(memory_space=pl.ANY)],
        out_specs=pl.BlockSpec((tm, d), lambda i: (i, 0)),
        out_shape=jax.ShapeDtypeStruct((t, d), F32),
        scratch_shapes=[pltpu.VMEM((2, TOP_K_INNER, tm, y_rows.shape[1]), y_rows.dtype),
                        pltpu.SemaphoreType.DMA((2,))],
        compiler_params=_cparams(("arbitrary",)),
        name="moe_combine",
    )(pos2, x1, route, norm_w.reshape(1, d), y_rows)


def _layer(x, p):
    t, d = x.shape
    inner = p["w_a_up"].shape[0]
    n_heads = p["a_log"].shape[0]
    s5_width = p["w_glu"].shape[0]
    xbc_dim = inner + 2 * SSD_GROUPS * SSD_STATE
    sizes = (inner, xbc_dim, n_heads, s5_width, 2 * d)
    offs = [0]
    for s in sizes:
        offs.append(offs[-1] + s)
    w_in = p["w_in"].astype(F32)
    assert offs[2] % LANES == 0 and n_heads <= LANES
    w_tail = w_in[:, offs[3]:].astype(BF16)
    dt_b = jnp.zeros((1, LANES), F32).at[0, :n_heads].set(p["dt_bias"].astype(F32))

    tm = min(ROW_TILE, t)
    tmm = min(MM_ROW_TILE, t)

    h = _rmsnorm(x, p["norm_mix_w"], tm)
    zs = _proj(h, w_in, offs[0], inner, None, "silu", tmm, MM_COL_TILE)
    xbc = _conv_proj(h, w_in, offs[1], xbc_dim, p["conv_w"], p["conv_b"], tmm, MM_COL_TILE)
    dt, dtt = _dt_proj(h, w_in, offs[2], dt_b, tm)
    u_steps = _chunk_proj(h, w_tail, 0, s5_width)
    gates = _proj(h, w_tail, s5_width, 2 * d, p["gate_b"], "sigmoid_bias", tmm, MM_COL_TILE)

    y = _ssd(xbc, dt, dtt[:n_heads], p["a_log"].astype(F32), p["d_ssd"].astype(F32), n_heads,
             min(SSD_CHUNK, t))
    ya = _gated_up(y, zs, p["norm_ssd_w"], p["w_a_up"].astype(BF16), tm)

    uc, wc, bbt, a_chunk = _s5_operators(p["s5_lambda_re"], p["s5_lambda_im"], p["s5_log_dt"],
                                         p["s5_b_re"], p["s5_b_im"], p["s5_c_re"], p["s5_c_im"])
    y5 = _s5(u_steps, uc, wc, bbt, a_chunk, p["s5_d"].astype(F32), min(S5_ROWS, t // S5_CHUNK))
    yb = _glu_up(y5, p["w_glu"].astype(BF16), p["w_b_up"].astype(BF16))

    w_router = jnp.zeros((d, LANES), F32)
    w_router = w_router.at[:, :N_EXPERT_GROUPS].set(p["w_route_group"].astype(F32))
    w_router = w_router.at[:, ROUTE_EXPERT_LANE0:ROUTE_EXPERT_LANE0 + N_EXPERTS].set(
        p["w_route_expert"].astype(F32))
    r_hi = w_router.astype(BF16)
    r_lo = (w_router - r_hi.astype(F32)).astype(BF16)
    r_b = jnp.zeros((1, LANES), F32)
    r_b = r_b.at[0, :N_EXPERT_GROUPS].set(p["b_route_group"].astype(F32))
    r_b = r_b.at[0, ROUTE_EXPERT_LANE0:ROUTE_EXPERT_LANE0 + N_EXPERTS].set(p["b_route_expert"].astype(F32))
    x1, h2, logits = _merge_out(gates, ya, yb, x, p["w_out"].astype(BF16), p["norm_ffn_w"],
                                r_hi, r_lo, r_b, tm)

    route = _route(logits, tm)
    expert_ids = route[:, ROUTE_ID_LANE:ROUTE_ID_LANE + TOP_K_INNER].astype(jnp.int32)
    bm = MOE_BLOCK
    tg = min(GATHER_TILE, t)
    pos, block_expert, n_used, pad_start, pad_count, n_rows = _dispatch_plan(expert_ids, bm)
    xs = _dispatch(h2, pos, pad_start, pad_count, n_used, n_rows, tg, bm)
    y_rows = _experts(xs, block_expert, n_used, p["w_exp_gate"], p["w_exp_up"], p["w_exp_down"], bm)
    return x1, y_rows, pos, route


def kernel(x, norm_mix_w, w_in, conv_w, conv_b, dt_bias, a_log, d_ssd, norm_ssd_w, w_a_up,
           s5_lambda_re, s5_lambda_im, s5_log_dt, s5_b_re, s5_b_im, s5_c_re, s5_c_im, s5_d,
           w_glu, w_b_up, gate_b, w_out, norm_ffn_w, w_route_group, b_route_group,
           w_route_expert, b_route_expert, w_exp_gate, w_exp_up, w_exp_down, norm_final_w):
    b, seq, d = x.shape
    assert b == 1, "the scans carry state along the flattened token axis"
    depth = w_in.shape[0]
    per_layer = dict(norm_mix_w=norm_mix_w, w_in=w_in, conv_w=conv_w, conv_b=conv_b, dt_bias=dt_bias,
                     a_log=a_log, d_ssd=d_ssd, norm_ssd_w=norm_ssd_w, w_a_up=w_a_up,
                     s5_lambda_re=s5_lambda_re, s5_lambda_im=s5_lambda_im, s5_log_dt=s5_log_dt,
                     s5_b_re=s5_b_re, s5_b_im=s5_b_im, s5_c_re=s5_c_re, s5_c_im=s5_c_im, s5_d=s5_d,
                     w_glu=w_glu, w_b_up=w_b_up, gate_b=gate_b, w_out=w_out, norm_ffn_w=norm_ffn_w,
                     w_route_group=w_route_group, b_route_group=b_route_group,
                     w_route_expert=w_route_expert, b_route_expert=b_route_expert,
                     w_exp_gate=w_exp_gate, w_exp_up=w_exp_up, w_exp_down=w_exp_down)
    xt = x.reshape(b * seq, d)
    tg = min(GATHER_TILE, b * seq)
    for i in range(depth):
        p = {k: v[i] for k, v in per_layer.items()}
        x1, y_rows, pos, route = _layer(xt, p)
        xt = _combine(x1, y_rows, pos, route, norm_final_w, tg, normalize=(i == depth - 1))
    return xt.reshape(b, seq, d)
```

```python
import functools
import math

import jax
import jax.numpy as jnp
from jax import lax
from jax.experimental import pallas as pl
from jax.experimental.pallas import tpu as pltpu

F32 = jnp.float32
BF16 = jnp.bfloat16

SSD_HEAD_DIM = 64
SSD_GROUPS = 8
SSD_STATE = 128
CONV_WIDTH = 4
S5_GROUP_CH = 16
S5_STATE = 64
N_EXPERT_GROUPS = 4
EXPERTS_PER_GROUP = 8
N_EXPERTS = N_EXPERT_GROUPS * EXPERTS_PER_GROUP
TOP_K_INNER = 2
RMS_EPS = 1e-6

LANES = 128
SUBLANES = 8
VMEM_LIMIT_BYTES = 52 * 1024 * 1024

ROW_TILE = 512
MM_ROW_TILE = 1024
MM_COL_TILE = 1024
SSD_CHUNK = 128
S5_CHUNK = 16
S5_SUPER = S5_GROUP_CH * 8
S5_ROWS = 256
MOE_BLOCK = 256
GATHER_TILE = 256
DMA_UNROLL = 8


def _cparams(sem, vmem=VMEM_LIMIT_BYTES):
    return pltpu.CompilerParams(dimension_semantics=sem, vmem_limit_bytes=vmem)


def _sigmoid(v):
    return 1.0 / (1.0 + jnp.exp(-v))


def _silu(v):
    return v * _sigmoid(v)


def _softplus(v):
    return jnp.maximum(v, 0.0) + jnp.log(1.0 + jnp.exp(-jnp.abs(v)))


def _gelu_tanh(v):
    c = math.sqrt(2.0 / math.pi)
    return 0.5 * v * (1.0 + jnp.tanh(c * (v + 0.044715 * (v * v * v))))


def _rms(v, w):
    ms = jnp.mean(v * v, axis=-1, keepdims=True)
    return v * lax.rsqrt(ms + RMS_EPS) * w


def _pack_bf16_pairs(v):
    n = v.shape[1] // 2
    lo = pltpu.bitcast(v[:, :n].astype(BF16).astype(F32), jnp.uint32)
    hi = pltpu.bitcast(v[:, n:].astype(BF16).astype(F32), jnp.uint32)
    return hi | (lo >> 16)


def _unpack_bf16_pairs(w):
    lo = pltpu.bitcast(w << 16, F32)
    hi = pltpu.bitcast(w & jnp.uint32(0xFFFF0000), F32)
    return jnp.concatenate([lo, hi], axis=1)


def _store_token_tiles(ref, rows):
    for a in range(ref.shape[-2]):
        ref[:, a, :] = rows[:, a * LANES:(a + 1) * LANES]


def _load_token_tiles(ref):
    return jnp.concatenate([ref[:, a, :] for a in range(ref.shape[-2])], axis=1)


def _rmsnorm_kernel(x_ref, w_ref, o_ref):
    o_ref[...] = _rms(x_ref[...], w_ref[...]).astype(o_ref.dtype)


def _rmsnorm(x, w, tm):
    t, d = x.shape
    return pl.pallas_call(
        _rmsnorm_kernel,
        grid=(t // tm,),
        in_specs=[pl.BlockSpec((tm, d), lambda i: (i, 0)),
                  pl.BlockSpec((1, d), lambda i: (0, 0))],
        out_specs=pl.BlockSpec((tm, d), lambda i: (i, 0)),
        out_shape=jax.ShapeDtypeStruct((t, d), BF16),
        compiler_params=_cparams(("parallel",)),
        name="rmsnorm",
    )(x, w.reshape(1, d))


def _resident_weight(w_ref, wbf_ref, row_axis=1):
    @pl.when(pl.program_id(row_axis) == 0)
    def _():
        wbf_ref[...] = w_ref[...].astype(BF16)
    return wbf_ref


def _proj_kernel(a_ref, w_ref, b_ref, o_ref, *scratch, act):
    w = _resident_weight(w_ref, scratch[0]) if scratch else w_ref
    p = jnp.dot(a_ref[...], w[...], preferred_element_type=F32)
    if act == "silu":
        p = _silu(p)
    elif act == "sigmoid_bias":
        p = _sigmoid(p + b_ref[...])
    o_ref[...] = p.astype(o_ref.dtype)


def _proj(h, w, col0, n, b, act, tm, tn):
    t, k = h.shape
    tn = min(tn, n)
    assert col0 % tn == 0 and n % tn == 0
    jb = col0 // tn
    if b is None:
        b = jnp.zeros((1, n), F32)
    scratch = [pltpu.VMEM((k, tn), BF16)] if w.dtype != BF16 else []
    return pl.pallas_call(
        functools.partial(_proj_kernel, act=act),
        grid=(n // tn, t // tm),
        in_specs=[pl.BlockSpec((tm, k), lambda j, i: (i, 0)),
                  pl.BlockSpec((k, tn), lambda j, i: (0, jb + j)),
                  pl.BlockSpec((1, tn), lambda j, i: (0, j))],
        out_specs=pl.BlockSpec((tm, tn), lambda j, i: (i, j)),
        out_shape=jax.ShapeDtypeStruct((t, n), BF16),
        scratch_shapes=scratch,
        compiler_params=_cparams(("parallel", "arbitrary")),
        name="proj_" + act,
    )(h, w, b.reshape(1, n))


def _conv_proj_kernel(a_ref, w_ref, cw_ref, cb_ref, o_ref, ext_ref, wbf_ref):
    tm = a_ref.shape[0]
    halo = SUBLANES

    @pl.when(pl.program_id(1) == 0)
    def _():
        ext_ref[pl.ds(0, halo), :] = jnp.zeros((halo, ext_ref.shape[1]), F32)

    w = _resident_weight(w_ref, wbf_ref)
    p = jnp.dot(a_ref[...], w[...], preferred_element_type=F32)
    ext_ref[pl.ds(halo, tm), :] = p
    acc = cb_ref[...] + cw_ref[CONV_WIDTH - 1:CONV_WIDTH, :] * p
    for k in range(CONV_WIDTH - 1):
        back = CONV_WIDTH - 1 - k
        acc = acc + cw_ref[k:k + 1, :] * ext_ref[pl.ds(halo - back, tm), :]
    o_ref[...] = _silu(acc).astype(o_ref.dtype)
    ext_ref[pl.ds(0, halo), :] = p[tm - halo:, :]


def _conv_proj(h, w, col0, n, conv_w, conv_b, tm, tn):
    t, k = h.shape
    assert col0 % tn == 0 and n % tn == 0
    jb = col0 // tn
    return pl.pallas_call(
        _conv_proj_kernel,
        grid=(n // tn, t // tm),
        in_specs=[pl.BlockSpec((tm, k), lambda j, i: (i, 0)),
                  pl.BlockSpec((k, tn), lambda j, i: (0, jb + j)),
                  pl.BlockSpec((CONV_WIDTH, tn), lambda j, i: (0, j)),
                  pl.BlockSpec((1, tn), lambda j, i: (0, j))],
        out_specs=pl.BlockSpec((tm, tn), lambda j, i: (i, j)),
        out_shape=jax.ShapeDtypeStruct((t, n), BF16),
        scratch_shapes=[pltpu.VMEM((tm + SUBLANES, tn), F32),
                        pltpu.VMEM((k, tn), BF16)],
        compiler_params=_cparams(("parallel", "arbitrary")),
        name="proj_conv",
    )(h, w, conv_w, conv_b.reshape(1, n))


def _dt_proj_kernel(a_ref, w_ref, b_ref, dt_ref, dtt_ref):
    p = jnp.dot(a_ref[...], w_ref[...].astype(BF16), preferred_element_type=F32) + b_ref[...]
    dt = _softplus(p)
    dt_ref[...] = dt
    dtt_ref[...] = dt.T


def _dt_proj(h, w, col0, b_pad, tm):
    t, k = h.shape
    jb = col0 // LANES
    return pl.pallas_call(
        _dt_proj_kernel,
        grid=(t // tm,),
        in_specs=[pl.BlockSpec((tm, k), lambda i: (i, 0)),
                  pl.BlockSpec((k, LANES), lambda i: (0, jb)),
                  pl.BlockSpec((1, LANES), lambda i: (0, 0))],
        out_specs=[pl.BlockSpec((tm, LANES), lambda i: (i, 0)),
                   pl.BlockSpec((LANES, tm), lambda i: (0, i))],
        out_shape=[jax.ShapeDtypeStruct((t, LANES), F32),
                   jax.ShapeDtypeStruct((LANES, t), F32)],
        compiler_params=_cparams(("parallel",)),
        name="proj_dt",
    )(h, w, b_pad)


def _step_major_perm(n_rows):
    assert n_rows == S5_CHUNK * S5_CHUNK
    shift = S5_CHUNK.bit_length() - 1
    row = lax.broadcasted_iota(jnp.int32, (n_rows, n_rows), 0)
    col = lax.broadcasted_iota(jnp.int32, (n_rows, n_rows), 1)
    swapped = ((row & (S5_CHUNK - 1)) << shift) | (row >> shift)
    return jnp.where(col == swapped, 1.0, 0.0).astype(BF16)


def _chunk_proj_kernel(a_ref, w_ref, o_ref, wbf_ref):
    w = _resident_weight(w_ref, wbf_ref, row_axis=0)
    u = jnp.dot(a_ref[...], w[...], preferred_element_type=F32).astype(BF16)
    u_steps = jnp.dot(_step_major_perm(u.shape[0]), u, preferred_element_type=F32).astype(o_ref.dtype)
    nj = o_ref.shape[1]
    for s in range(o_ref.shape[0]):
        o_ref[s] = u_steps[s * nj:(s + 1) * nj, :]


def _chunk_proj(h, w, col0, n):
    t, k = h.shape
    nj = t // S5_CHUNK
    tm = S5_CHUNK * S5_CHUNK
    assert col0 % n == 0
    jb = col0 // n
    return pl.pallas_call(
        _chunk_proj_kernel,
        grid=(t // tm,),
        in_specs=[pl.BlockSpec((tm, k), lambda i: (i, 0)),
                  pl.BlockSpec((k, n), lambda i: (0, jb))],
        out_specs=pl.BlockSpec((S5_CHUNK, S5_CHUNK, n), lambda i: (0, i, 0)),
        out_shape=jax.ShapeDtypeStruct((S5_CHUNK, nj, n), BF16),
        scratch_shapes=[pltpu.VMEM((k, n), BF16)],
        compiler_params=_cparams(("arbitrary",)),
        name="proj_u",
    )(h, w)


def _cumsum_rows(v):
    n = v.shape[0]
    idx = lax.broadcasted_iota(jnp.int32, v.shape, 0)
    k = 1
    while k < n:
        v = v + jnp.where(idx >= k, pltpu.roll(v, k, 0), 0.0)
        k *= 2
    return v


def _cumsum_lanes(v):
    n = v.shape[1]
    idx = lax.broadcasted_iota(jnp.int32, v.shape, 1)
    k = 1
    while k < n:
        v = v + jnp.where(idx >= k, pltpu.roll(v, k, 1), 0.0)
        k *= 2
    return v


def _ssd_kernel(xbc_ref, dt_ref, dtt_ref, alog_r_ref, alog_c_ref, dskip_ref, o_ref, state_ref, *,
                n_heads):
    q = dt_ref.shape[0]
    n = SSD_STATE
    p_dim = SSD_HEAD_DIM
    r_heads = n_heads // SSD_GROUPS
    gw = r_heads * p_dim
    inner = n_heads * p_dim

    @pl.when(pl.program_id(0) == 0)
    def _():
        state_ref[...] = jnp.zeros(state_ref.shape, F32)

    dtt = dtt_ref[...]
    cs_col = _cumsum_rows(dt_ref[...] * -jnp.exp(alog_r_ref[...]))
    cs_row = _cumsum_lanes(dtt * -jnp.exp(alog_c_ref[...]))
    causal = lax.broadcasted_iota(jnp.int32, (q, q), 0) >= lax.broadcasted_iota(jnp.int32, (q, q), 1)
    head_of_lane = lax.broadcasted_iota(jnp.int32, (1, gw), 1) // p_dim
    eye = jnp.where(lax.broadcasted_iota(jnp.int32, (n, n), 0) == lax.broadcasted_iota(jnp.int32, (n, n), 1),
                    1.0, 0.0).astype(BF16)
    nt = (((1,), (1,)), ((), ()))

    for g in range(SSD_GROUPS):
        x_g = xbc_ref[:, g * gw:(g + 1) * gw]
        b_g = xbc_ref[:, inner + g * n:inner + (g + 1) * n]
        c_g = xbc_ref[:, inner + (SSD_GROUPS + g) * n:inner + (SSD_GROUPS + g + 1) * n]
        cb = lax.dot_general(c_g, b_g, nt, preferred_element_type=F32)
        b_t = lax.dot_general(eye, b_g, nt, preferred_element_type=F32)
        c_f = c_g.astype(F32)
        s_g = state_ref[g]
        lhs_parts, bt_parts, cd = [], [], jnp.zeros((1, gw), F32)
        for r in range(r_heads):
            h = g * r_heads + r
            csb = jnp.broadcast_to(cs_col[:, h:h + 1], (q, n))
            csr = cs_row[h:h + 1, :]
            dtr = dtt[h:h + 1, :]
            cs_last = csr[:, q - 1:q]
            decay = jnp.exp(jnp.where(causal, csb - csr, -1e30))
            lhs_parts.append(jnp.concatenate([cb * decay * dtr, c_f * jnp.exp(csb)], axis=1).astype(BF16))
            bt_parts.append((b_t * (dtr * jnp.exp(cs_last - csr))).astype(BF16))
            cd = jnp.where(head_of_lane == r, jnp.exp(cs_last), cd)
        rhs = jnp.concatenate([x_g, s_g.astype(BF16)], axis=0)
        y_all = jnp.dot(jnp.concatenate(lhs_parts, axis=0), rhs, preferred_element_type=F32)
        s_all = jnp.dot(jnp.concatenate(bt_parts, axis=0), x_g, preferred_element_type=F32)
        y_g = jnp.zeros((q, gw), F32)
        s_new = jnp.zeros((n, gw), F32)
        for r in range(r_heads):
            mine = head_of_lane == r
            y_g = jnp.where(mine, y_all[r * q:(r + 1) * q, :], y_g)
            s_new = jnp.where(mine, s_all[r * n:(r + 1) * n, :], s_new)
        state_ref[g] = s_g * cd + s_new
        y_g = y_g + x_g.astype(F32) * dskip_ref[:, g * gw:(g + 1) * gw]
        o_ref[:, g * gw:(g + 1) * gw] = y_g.astype(o_ref.dtype)


def _ssd(xbc, dt, dtt, a_log, d_skip, n_heads, q):
    t, width = xbc.shape
    inner = n_heads * SSD_HEAD_DIM
    gw = inner // SSD_GROUPS
    assert n_heads % SUBLANES == 0 and n_heads <= LANES
    alog_r = jnp.zeros((1, LANES), F32).at[0, :n_heads].set(a_log)
    alog_c = a_log.reshape(n_heads, 1)
    dskip = jnp.repeat(d_skip, SSD_HEAD_DIM).reshape(1, inner)
    return pl.pallas_call(
        functools.partial(_ssd_kernel, n_heads=n_heads),
        grid=(t // q,),
        in_specs=[pl.BlockSpec((q, width), lambda c: (c, 0)),
                  pl.BlockSpec((q, LANES), lambda c: (c, 0)),
                  pl.BlockSpec((n_heads, q), lambda c: (0, c)),
                  pl.BlockSpec((1, LANES), lambda c: (0, 0)),
                  pl.BlockSpec((n_heads, 1), lambda c: (0, 0)),
                  pl.BlockSpec((1, inner), lambda c: (0, 0))],
        out_specs=pl.BlockSpec((q, inner), lambda c: (c, 0)),
        out_shape=jax.ShapeDtypeStruct((t, inner), BF16),
        scratch_shapes=[pltpu.VMEM((SSD_GROUPS, SSD_STATE, gw), F32)],
        compiler_params=_cparams(("arbitrary",)),
        name="ssd_scan",
    )(xbc, dt, dtt, alog_r, alog_c, dskip)


def _gated_up_kernel(y_ref, z_ref, nw_ref, w_ref, o_ref):
    v = y_ref[...].astype(F32) * z_ref[...].astype(F32)
    na = _rms(v, nw_ref[...]).astype(BF16)
    o_ref[...] = jnp.dot(na, w_ref[...], preferred_element_type=F32).astype(o_ref.dtype)


def _gated_up(y, zs, norm_w, w, tm):
    t, d = y.shape
    n = w.shape[1]
    return pl.pallas_call(
        _gated_up_kernel,
        grid=(t // tm,),
        in_specs=[pl.BlockSpec((tm, d), lambda i: (i, 0)),
                  pl.BlockSpec((tm, d), lambda i: (i, 0)),
                  pl.BlockSpec((1, d), lambda i: (0, 0)),
                  pl.BlockSpec((d, n), lambda i: (0, 0))],
        out_specs=pl.BlockSpec((tm, n), lambda i: (i, 0)),
        out_shape=jax.ShapeDtypeStruct((t, n), BF16),
        compiler_params=_cparams(("parallel",)),
        name="ssd_gated_up",
    )(y, zs, norm_w.reshape(1, d), w)


def _s5_operators(lam_re, lam_im, log_dt, b_re, b_im, c_re, c_im):
    ng, ns = lam_re.shape
    nc = S5_GROUP_CH
    L = S5_CHUNK
    per = S5_SUPER // nc
    nsg = ng // per
    lr, li = lam_re.astype(F32), lam_im.astype(F32)
    dt = jnp.exp(log_dt.astype(F32))[:, None]
    mag = jnp.exp(lr * dt)
    ang = li * dt
    abar_r, abar_i = mag * jnp.cos(ang), mag * jnp.sin(ang)
    den = lr * lr + li * li
    nr, ni = abar_r - 1.0, abar_i
    coef_r = (nr * lr + ni * li) / den
    coef_i = (ni * lr - nr * li) / den
    bre, bim = b_re.astype(F32), b_im.astype(F32)
    bb_r = coef_r[..., None] * bre - coef_i[..., None] * bim
    bb_i = coef_r[..., None] * bim + coef_i[..., None] * bre
    cre, cim = c_re.astype(F32), c_im.astype(F32)
    ks = jnp.arange(L + 1, dtype=F32)[:, None, None]
    pmag = jnp.exp(ks * (lr * dt)[None])
    pang = ks * ang[None]
    pw_r, pw_i = pmag * jnp.cos(pang), pmag * jnp.sin(pang)
    ca_r = cre[None] * pw_r[:, :, None, :] - cim[None] * pw_i[:, :, None, :]
    ca_i = cre[None] * pw_i[:, :, None, :] + cim[None] * pw_r[:, :, None, :]
    def _rows_n(v, steps):
        return v.reshape(steps, nsg, per, nc, ns).transpose(1, 4, 0, 2, 3).reshape(nsg, ns, steps * per * nc)

    uc = jnp.concatenate([_rows_n(ca_r, L + 1), -_rows_n(ca_i, L + 1)], axis=1)
    ks_rev = (L - 1) - jnp.arange(L, dtype=F32)[:, None, None]
    rmag = jnp.exp(ks_rev * (lr * dt)[None])
    rang = ks_rev * ang[None]
    rev_r, rev_i = rmag * jnp.cos(rang), rmag * jnp.sin(rang)
    ab_r = rev_r[..., None] * bb_r[None] - rev_i[..., None] * bb_i[None]
    ab_i = rev_r[..., None] * bb_i[None] + rev_i[..., None] * bb_r[None]
    ab_rt, ab_it = ab_r.transpose(0, 1, 3, 2), ab_i.transpose(0, 1, 3, 2)
    wc = jnp.concatenate([_rows_n(ab_rt, L), _rows_n(ab_it, L)], axis=1)

    def _rows_gc(v):
        return v.reshape(nsg, per, ns, nc).transpose(0, 1, 3, 2).reshape(nsg, per * nc, ns)

    bbt = jnp.concatenate([_rows_gc(bb_r), _rows_gc(bb_i)], axis=2)
    a_chunk = jnp.concatenate([pw_r[L].reshape(nsg, 1, per * ns),
                               pw_i[L].reshape(nsg, 1, per * ns)], axis=2)
    return uc, wc, bbt, a_chunk


def _split_bf16(v):
    hi = v.astype(BF16)
    return hi, (v - hi.astype(F32)).astype(BF16)


def _s5_kernel(u_ref, uc_ref, wc_ref, bbt_ref, ach_ref, dsk_ref, o_ref,
               toep_ref, wt_ref, v_ref, x_ref, sp_ref, carry_ref):
    jb = pl.program_id(1)
    nl, nj, cw = u_ref.shape
    half = carry_ref.shape[1] // 2
    per = cw // S5_GROUP_CH
    ns = half // per
    nt = (((1,), (1,)), ((), ()))

    @pl.when(jb == 0)
    def _():
        carry_ref[...] = jnp.zeros(carry_ref.shape, F32)
        uc = uc_ref[...]
        wc = wc_ref[...]
        b_hi, b_lo = _split_bf16(bbt_ref[...])
        u_hi, u_lo = _split_bf16(uc[:, :nl * cw])
        kall = (jnp.dot(b_hi, u_hi, preferred_element_type=F32)
                + jnp.dot(b_lo, u_hi, preferred_element_type=F32)
                + jnp.dot(b_hi, u_lo, preferred_element_type=F32))
        row_g = lax.broadcasted_iota(jnp.int32, (cw, 1), 0) // S5_GROUP_CH
        col_g = (lax.broadcasted_iota(jnp.int32, (1, nl * cw), 1) // S5_GROUP_CH) % per
        kall = jnp.where(row_g == col_g, kall, 0.0).astype(BF16)
        toep_ref[...] = jnp.zeros(toep_ref.shape, toep_ref.dtype)
        for s_in in range(nl):
            for s_out in range(s_in, nl):
                k = s_out - s_in
                toep_ref[pl.ds(s_in * cw, cw), pl.ds(s_out * cw, cw)] = kall[:, k * cw:(k + 1) * cw]
        for gp in range(per):
            mine = col_g == gp
            for part in range(2):
                rows = pl.ds(part * half + gp * ns, ns)
                src = slice(part * ns, (part + 1) * ns)
                v_ref[rows, :] = jnp.where(mine, uc[src, cw:], 0.0).astype(BF16)
                wt_ref[rows, :] = jnp.where(mine, wc[src, :], 0.0).astype(BF16)

    lhs = jnp.concatenate([u_ref[s] for s in range(nl)], axis=1)
    x_ref[...] = lax.dot_general(lhs, wt_ref[...], nt, preferred_element_type=F32)

    a_re = ach_ref[:, :half]
    a_im = ach_ref[:, half:]

    def step(j, carry):
        s_re, s_im = carry
        sp_ref[pl.ds(j, 1), :half] = s_re
        sp_ref[pl.ds(j, 1), half:] = s_im
        xr = x_ref[pl.ds(j, 1), :half]
        xi = x_ref[pl.ds(j, 1), half:]
        return (a_re * s_re - a_im * s_im + xr, a_re * s_im + a_im * s_re + xi)

    s_re, s_im = lax.fori_loop(0, nj, step, (carry_ref[:, :half], carry_ref[:, half:]), unroll=8)
    carry_ref[:, :half] = s_re
    carry_ref[:, half:] = s_im

    y_state = jnp.dot(sp_ref[...].astype(BF16), v_ref[...], preferred_element_type=F32)
    dsk = dsk_ref[...]
    pair = 2 * cw
    for tp in range(nl // 2):
        kdim = pair * (tp + 1)
        y = jnp.dot(lhs[:, :kdim], toep_ref[pl.ds(0, kdim), pl.ds(tp * pair, pair)],
                    preferred_element_type=F32)
        y = y + y_state[:, tp * pair:(tp + 1) * pair]
        for h in range(2):
            s = 2 * tp + h
            o_ref[s] = (y[:, h * cw:(h + 1) * cw] + dsk * u_ref[s].astype(F32)).astype(o_ref.dtype)


def _s5(u_steps, uc, wc, bbt, a_chunk, d_skip, tj):
    nl, nj, width = u_steps.shape
    nsg = width // S5_SUPER
    nstate = a_chunk.shape[2]
    rows = uc.shape[1]
    return pl.pallas_call(
        _s5_kernel,
        grid=(nsg, nj // tj),
        in_specs=[pl.BlockSpec((nl, tj, S5_SUPER), lambda g, j: (0, j, g)),
                  pl.BlockSpec((None, rows, (nl + 1) * S5_SUPER), lambda g, j: (g, 0, 0)),
                  pl.BlockSpec((None, rows, nl * S5_SUPER), lambda g, j: (g, 0, 0)),
                  pl.BlockSpec((None, S5_SUPER, rows), lambda g, j: (g, 0, 0)),
                  pl.BlockSpec((None, 1, nstate), lambda g, j: (g, 0, 0)),
                  pl.BlockSpec((None, 1, S5_SUPER), lambda g, j: (g, 0, 0))],
        out_specs=pl.BlockSpec((nl, tj, S5_SUPER), lambda g, j: (0, j, g)),
        out_shape=jax.ShapeDtypeStruct((nl, nj, width), BF16),
        scratch_shapes=[pltpu.VMEM((nl * S5_SUPER, nl * S5_SUPER), BF16),
                        pltpu.VMEM((nstate, nl * S5_SUPER), BF16),
                        pltpu.VMEM((nstate, nl * S5_SUPER), BF16),
                        pltpu.VMEM((tj, nstate), F32),
                        pltpu.VMEM((tj, nstate), F32),
                        pltpu.VMEM((1, nstate), F32)],
        compiler_params=_cparams(("parallel", "arbitrary")),
        name="s5_scan",
    )(u_steps, uc, wc, bbt, a_chunk, d_skip.reshape(nsg, 1, S5_SUPER))


def _glu_up_kernel(y_ref, wg_ref, wu_ref, o_ref):
    y_steps = jnp.concatenate([y_ref[s] for s in range(y_ref.shape[0])], axis=0)
    y = jnp.dot(_step_major_perm(y_steps.shape[0]), y_steps, preferred_element_type=F32)
    v = _gelu_tanh(y)
    gate = _sigmoid(jnp.dot(v.astype(BF16), wg_ref[...], preferred_element_type=F32))
    o_ref[...] = jnp.dot((v * gate).astype(BF16), wu_ref[...],
                         preferred_element_type=F32).astype(o_ref.dtype)


def _glu_up(y_steps, w_glu, w_up):
    nl, nj, width = y_steps.shape
    n = w_up.shape[1]
    tm = S5_CHUNK * S5_CHUNK
    return pl.pallas_call(
        _glu_up_kernel,
        grid=(nj // S5_CHUNK,),
        in_specs=[pl.BlockSpec((nl, S5_CHUNK, width), lambda i: (0, i, 0)),
                  pl.BlockSpec((width, width), lambda i: (0, 0)),
                  pl.BlockSpec((width, n), lambda i: (0, 0))],
        out_specs=pl.BlockSpec((tm, n), lambda i: (i, 0)),
        out_shape=jax.ShapeDtypeStruct((nj * nl, n), BF16),
        compiler_params=_cparams(("parallel",)),
        name="s5_glu_up",
    )(y_steps, w_glu, w_up)


def _merge_out_kernel(ga_ref, gb_ref, ya_ref, yb_ref, x_ref, w_ref, nw_ref, rhi_ref, rlo_ref, rb_ref,
                      x1_ref, h2_ref, lg_ref):
    merged = (ga_ref[...].astype(F32) * ya_ref[...].astype(F32)
              + gb_ref[...].astype(F32) * yb_ref[...].astype(F32))
    x1 = x_ref[...] + jnp.dot(merged.astype(BF16), w_ref[...], preferred_element_type=F32)
    x1_ref[...] = x1
    h2 = _rms(x1, nw_ref[...])
    _store_token_tiles(h2_ref, _pack_bf16_pairs(h2))
    hi = h2.astype(BF16)
    lo = (h2 - hi.astype(F32)).astype(BF16)
    lg = (jnp.dot(hi, rhi_ref[...], preferred_element_type=F32)
          + jnp.dot(lo, rhi_ref[...], preferred_element_type=F32)
          + jnp.dot(hi, rlo_ref[...], preferred_element_type=F32))
    lg_ref[...] = lg + rb_ref[...]


def _merge_out(gates, ya, yb, x, w_out, norm_w, r_hi, r_lo, r_b, tm):
    t, d = x.shape
    row = lambda i: (i, 0)
    full = lambda i: (0, 0)
    return pl.pallas_call(
        _merge_out_kernel,
        grid=(t // tm,),
        in_specs=[pl.BlockSpec((tm, d), row),
                  pl.BlockSpec((tm, d), lambda i: (i, 1)),
                  pl.BlockSpec((tm, d), row),
                  pl.BlockSpec((tm, d), row),
                  pl.BlockSpec((tm, d), row),
                  pl.BlockSpec((d, d), full),
                  pl.BlockSpec((1, d), full),
                  pl.BlockSpec((d, LANES), full),
                  pl.BlockSpec((d, LANES), full),
                  pl.BlockSpec((1, LANES), full)],
        out_specs=[pl.BlockSpec((tm, d), row),
                   pl.BlockSpec((tm, d // 2 // LANES, LANES), lambda i: (i, 0, 0)),
                   pl.BlockSpec((tm, LANES), row)],
        out_shape=[jax.ShapeDtypeStruct((t, d), F32),
                   jax.ShapeDtypeStruct((t, d // 2 // LANES, LANES), jnp.uint32),
                   jax.ShapeDtypeStruct((t, LANES), F32)],
        compiler_params=_cparams(("parallel",)),
        name="merge_out",
    )(gates, gates, ya, yb, x, w_out, norm_w.reshape(1, d), r_hi, r_lo, r_b)


ROUTE_ID_LANE = 0
ROUTE_W_LANE = TOP_K_INNER
ROUTE_EXPERT_LANE0 = SUBLANES


def _route_kernel(lg_ref, o_ref):
    lg = lg_ref[...]
    lane = lax.broadcasted_iota(jnp.int32, lg.shape, 1)
    neg = -jnp.inf
    big = LANES
    is_g = lane < N_EXPERT_GROUPS
    gl = jnp.where(is_g, lg, neg)
    gmax = jnp.max(gl, axis=1, keepdims=True)
    grp = jnp.min(jnp.where(gl == gmax, lane, big), axis=1, keepdims=True)
    pg_sel = 1.0 / jnp.sum(jnp.where(is_g, jnp.exp(lg - gmax), 0.0), axis=1, keepdims=True)
    e_lo = ROUTE_EXPERT_LANE0 + grp * EXPERTS_PER_GROUP
    in_grp = (lane >= e_lo) & (lane < e_lo + EXPERTS_PER_GROUP)
    el = jnp.where(in_grp, lg, neg)
    v0 = jnp.max(el, axis=1, keepdims=True)
    i0 = jnp.min(jnp.where(el == v0, lane, big), axis=1, keepdims=True)
    el1 = jnp.where(lane == i0, neg, el)
    v1 = jnp.max(el1, axis=1, keepdims=True)
    i1 = jnp.min(jnp.where(el1 == v1, lane, big), axis=1, keepdims=True)
    e1w = jnp.exp(v1 - v0)
    w0 = pg_sel / (1.0 + e1w)
    w1 = pg_sel * e1w / (1.0 + e1w)
    out = jnp.where(lane == ROUTE_ID_LANE, (i0 - ROUTE_EXPERT_LANE0).astype(F32),
          jnp.where(lane == ROUTE_ID_LANE + 1, (i1 - ROUTE_EXPERT_LANE0).astype(F32),
          jnp.where(lane == ROUTE_W_LANE, w0, jnp.where(lane == ROUTE_W_LANE + 1, w1, 0.0))))
    o_ref[...] = out


def _route(logits, tm):
    t = logits.shape[0]
    return pl.pallas_call(
        _route_kernel,
        grid=(t // tm,),
        in_specs=[pl.BlockSpec((tm, LANES), lambda i: (i, 0))],
        out_specs=pl.BlockSpec((tm, LANES), lambda i: (i, 0)),
        out_shape=jax.ShapeDtypeStruct((t, LANES), F32),
        compiler_params=_cparams(("parallel",)),
        name="route",
    )(logits)


def _dispatch_plan(expert_ids, bm):
    n_tokens = expert_ids.shape[0]
    n_assign = n_tokens * TOP_K_INNER
    eid = expert_ids.reshape(n_assign)
    experts = jnp.arange(N_EXPERTS, dtype=jnp.int32)
    onehot = (eid[:, None] == experts[None, :]).astype(jnp.int32)
    csum = jnp.cumsum(onehot, axis=0)
    counts = csum[-1]
    rank = jnp.sum(onehot * csum, axis=1) - 1
    padded = ((counts + bm - 1) // bm) * bm
    ends = jnp.cumsum(padded)
    starts = ends - padded
    dest = jnp.sum(onehot * starts[None, :], axis=1) + rank
    n_rows = n_assign + N_EXPERTS * bm
    nb = n_rows // bm
    block_start = jnp.arange(nb, dtype=jnp.int32) * bm
    block_expert = jnp.minimum(jnp.sum((ends[None, :] <= block_start[:, None]).astype(jnp.int32), axis=1),
                               N_EXPERTS - 1)
    n_used = (ends[-1] // bm).astype(jnp.int32).reshape(1)
    return (dest.reshape(n_tokens, TOP_K_INNER), block_expert, n_used,
            (starts + counts).astype(jnp.int32), (padded - counts).astype(jnp.int32), n_rows)


def _dispatch_kernel(pad_start_ref, pad_count_ref, nused_ref, dest_ref, h_ref, xs_hbm, stage, zrow, sem, zsem):
    i = pl.program_id(0)
    n_steps = pl.num_programs(0)
    tm = h_ref.shape[0]
    bm = zrow.shape[0]
    nb = xs_hbm.shape[0] // bm
    slot = i % 2

    def tile_wait(s):
        for _ in range(TOP_K_INNER):
            pltpu.make_async_copy(stage.at[s], xs_hbm.at[pl.ds(0, tm)], sem.at[s]).wait()

    @pl.when(i >= 2)
    def _():
        tile_wait(slot)

    stage[slot] = h_ref[...]

    def issue(r, carry):
        for k in range(TOP_K_INNER):
            pltpu.make_async_copy(stage.at[slot, r], xs_hbm.at[dest_ref[k, r]], sem.at[slot]).start()
        return carry
    lax.fori_loop(0, tm, issue, 0, unroll=DMA_UNROLL)

    @pl.when(i == n_steps - 1)
    def _():
        tile_wait(slot)

        @pl.when(n_steps >= 2)
        def _():
            tile_wait(1 - slot)

        zrow[...] = jnp.zeros(zrow.shape, zrow.dtype)

        def pad_copy(e, r):
            return pltpu.make_async_copy(zrow.at[0], xs_hbm.at[pad_start_ref[e] + r], zsem)

        def block_copy(b):
            return pltpu.make_async_copy(zrow, xs_hbm.at[pl.ds(b * bm, bm)], zsem)
        for e in range(N_EXPERTS):
            lax.fori_loop(0, pad_count_ref[e], lambda r, c, e=e: (pad_copy(e, r).start(), c)[1], 0)
        lax.fori_loop(nused_ref[0], nb, lambda b, c: (block_copy(b).start(), c)[1], 0)
        for e in range(N_EXPERTS):
            lax.fori_loop(0, pad_count_ref[e], lambda r, c, e=e: (pad_copy(e, r).wait(), c)[1], 0)
        lax.fori_loop(nused_ref[0], nb, lambda b, c: (block_copy(b).wait(), c)[1], 0)


def _dispatch(h2, dest, pad_start, pad_count, n_used, n_rows, tm, bm):
    t = h2.shape[0]
    tile = h2.shape[1:]
    nt = t // tm
    dest_t = dest.reshape(nt, tm, TOP_K_INNER).transpose(0, 2, 1)
    grid_spec = pltpu.PrefetchScalarGridSpec(
        num_scalar_prefetch=3,
        grid=(nt,),
        in_specs=[pl.BlockSpec((None, TOP_K_INNER, tm), lambda i, *_: (i, 0, 0), memory_space=pltpu.SMEM),
                  pl.BlockSpec((tm,) + tile, lambda i, *_: (i, 0, 0))],
        out_specs=pl.BlockSpec(memory_space=pl.ANY),
        scratch_shapes=[pltpu.VMEM((2, tm) + tile, h2.dtype),
                        pltpu.VMEM((bm,) + tile, h2.dtype),
                        pltpu.SemaphoreType.DMA((2,)),
                        pltpu.SemaphoreType.DMA(())],
    )
    return pl.pallas_call(
        _dispatch_kernel,
        grid_spec=grid_spec,
        out_shape=jax.ShapeDtypeStruct((n_rows,) + tile, h2.dtype),
        compiler_params=_cparams(("arbitrary",)),
        name="moe_dispatch",
    )(pad_start, pad_count, n_used, dest_t, h2)


def _experts_kernel(bexp_ref, nused_ref, first_ref, next_ref, slot_ref, xs_ref, wg_hbm, wu_hbm, wd_hbm,
                    o_ref, wg_f, wu_f, wd_f, wg_s, wu_s, wd_s, sem):
    b = pl.program_id(0)
    n_used = nused_ref[0]

    def weight_copies(e, s):
        return (pltpu.make_async_copy(wg_hbm.at[e], wg_f.at[s], sem.at[s, 0]),
                pltpu.make_async_copy(wu_hbm.at[e], wu_f.at[s], sem.at[s, 1]),
                pltpu.make_async_copy(wd_hbm.at[e], wd_f.at[s], sem.at[s, 2]))

    @pl.when((b == 0) & (n_used > 0))
    def _():
        for c in weight_copies(bexp_ref[0], 0):
            c.start()

    @pl.when((b < n_used) & (first_ref[b] == 1))
    def _():
        s = slot_ref[b]
        for c in weight_copies(bexp_ref[b], s):
            c.wait()

        @pl.when(next_ref[b] >= 0)
        def _():
            for c in weight_copies(next_ref[b], 1 - s):
                c.start()

        wg_s[...] = wg_f[s].astype(BF16)
        wu_s[...] = wu_f[s].astype(BF16)
        wd_s[...] = wd_f[s].astype(BF16)

    @pl.when(b < n_used)
    def _():
        xb = _unpack_bf16_pairs(_load_token_tiles(xs_ref)).astype(BF16)
        hg = jnp.dot(xb, wg_s[...], preferred_element_type=F32)
        hu = jnp.dot(xb, wu_s[...], preferred_element_type=F32)
        act = (_silu(hg) * hu).astype(BF16)
        _store_token_tiles(o_ref, _pack_bf16_pairs(jnp.dot(act, wd_s[...], preferred_element_type=F32)))

    @pl.when(b >= n_used)
    def _():
        o_ref[...] = jnp.zeros(o_ref.shape, o_ref.dtype)


def _experts(xs, block_expert, n_used, w_g, w_u, w_d, bm):
    n_rows = xs.shape[0]
    tile = xs.shape[1:]
    d, ff = w_g.shape[1], w_g.shape[2]
    assert d == 2 * tile[0] * tile[1]
    nb = n_rows // bm
    blk = jnp.arange(nb, dtype=jnp.int32)
    valid = blk < n_used[0]
    first = (valid & ((blk == 0) | (block_expert != jnp.roll(block_expert, 1)))).astype(jnp.int32)
    later = valid[None, :] & (block_expert[None, :] > block_expert[:, None])
    nxt = jnp.min(jnp.where(later, block_expert[None, :], N_EXPERTS), axis=1)
    nxt = jnp.where(nxt == N_EXPERTS, -1, nxt).astype(jnp.int32)
    slot = ((jnp.cumsum(first) - 1) % 2).astype(jnp.int32)

    def used(b, nu):
        return jnp.minimum(b, jnp.maximum(nu[0] - 1, 0))

    grid_spec = pltpu.PrefetchScalarGridSpec(
        num_scalar_prefetch=5,
        grid=(nb,),
        in_specs=[pl.BlockSpec((bm,) + tile, lambda b, be, nu, *_: (used(b, nu), 0, 0)),
                  pl.BlockSpec(memory_space=pl.ANY),
                  pl.BlockSpec(memory_space=pl.ANY),
                  pl.BlockSpec(memory_space=pl.ANY)],
        out_specs=pl.BlockSpec((bm,) + tile, lambda b, *_: (b, 0, 0)),
        scratch_shapes=[pltpu.VMEM((2, d, ff), w_g.dtype),
                        pltpu.VMEM((2, d, ff), w_u.dtype),
                        pltpu.VMEM((2, ff, d), w_d.dtype),
                        pltpu.VMEM((d, ff), BF16),
                        pltpu.VMEM((d, ff), BF16),
                        pltpu.VMEM((ff, d), BF16),
                        pltpu.SemaphoreType.DMA((2, 3))],
    )
    return pl.pallas_call(
        _experts_kernel,
        grid_spec=grid_spec,
        out_shape=jax.ShapeDtypeStruct((n_rows,) + tile, xs.dtype),
        compiler_params=_cparams(("arbitrary",)),
        name="moe_experts",
    )(block_expert, n_used, first, nxt, slot, xs, w_g, w_u, w_d)


def _combine_kernel(pos_ref, x1_ref, rt_ref, nw_ref, y_hbm, o_ref, ybuf, sem, *, normalize):
    i = pl.program_id(0)
    n = pl.num_programs(0)
    tm = x1_ref.shape[0]

    def start_tile(slot, which):
        def body(r, carry):
            for k in range(TOP_K_INNER):
                pltpu.make_async_copy(y_hbm.at[pos_ref[which, k, r]], ybuf.at[slot, k, r],
                                      sem.at[slot]).start()
            return carry
        lax.fori_loop(0, tm, body, 0, unroll=DMA_UNROLL)

    def wait_tile(slot):
        for k in range(TOP_K_INNER):
            pltpu.make_async_copy(y_hbm.at[pl.ds(0, tm)], ybuf.at[slot, k], sem.at[slot]).wait()

    slot = i % 2

    @pl.when(i == 0)
    def _():
        start_tile(0, 0)

    @pl.when(i + 1 < n)
    def _():
        start_tile(1 - slot, 1)

    wait_tile(slot)
    acc = x1_ref[...]
    for k in range(TOP_K_INNER):
        y_k = _unpack_bf16_pairs(_load_token_tiles(ybuf.at[slot, k]))
        acc = acc + rt_ref[:, ROUTE_W_LANE + k:ROUTE_W_LANE + k + 1] * y_k
    o_ref[...] = _rms(acc, nw_ref[...]) if normalize else acc


def _combine(x1, y_rows, pos, route, norm_w, tm, normalize):
    t, d = x1.shape
    nt = t // tm
    pos_t = pos.reshape(nt, tm, TOP_K_INNER).transpose(0, 2, 1)
    pos_next = jnp.concatenate([pos_t[1:], pos_t[-1:]], axis=0)
    pos2 = jnp.stack([pos_t, pos_next], axis=1)
    return pl.pallas_call(
        functools.partial(_combine_kernel, normalize=normalize),
        grid=(nt,),
        in_specs=[pl.BlockSpec((None, 2, TOP_K_INNER, tm), lambda i: (i, 0, 0, 0), memory_space=pltpu.SMEM),
                  pl.BlockSpec((tm, d), lambda i: (i, 0)),
                  pl.BlockSpec((tm, LANES), lambda i: (i, 0)),
                  pl.BlockSpec((1, d), lambda i: (0, 0)),
                  pl.BlockSpec(memory_space=pl.ANY)],
        out_specs=pl.BlockSpec((tm, d), lambda i: (i, 0)),
        out_shape=jax.ShapeDtypeStruct((t, d), F32),
        scratch_shapes=[pltpu.VMEM((2, TOP_K_INNER, tm) + y_rows.shape[1:], y_rows.dtype),
                        pltpu.SemaphoreType.DMA((2,))],
        compiler_params=_cparams(("arbitrary",)),
        name="moe_combine",
    )(pos2, x1, route, norm_w.reshape(1, d), y_rows)


def _layer(x, p):
    t, d = x.shape
    inner = p["w_a_up"].shape[0]
    n_heads = p["a_log"].shape[0]
    s5_width = p["w_glu"].shape[0]
    xbc_dim = inner + 2 * SSD_GROUPS * SSD_STATE
    sizes = (inner, xbc_dim, n_heads, s5_width, 2 * d)
    offs = [0]
    for s in sizes:
        offs.append(offs[-1] + s)
    w_in = p["w_in"].astype(F32)
    assert offs[2] % LANES == 0 and n_heads <= LANES
    w_tail = lax.optimization_barrier(w_in[:, offs[3]:])
    dt_b = jnp.zeros((1, LANES), F32).at[0, :n_heads].set(p["dt_bias"].astype(F32))

    tm = min(ROW_TILE, t)
    tmm = min(MM_ROW_TILE, t)

    h = _rmsnorm(x, p["norm_mix_w"], tm)
    zs = _proj(h, w_in, offs[0], inner, None, "silu", tmm, MM_COL_TILE)
    xbc = _conv_proj(h, w_in, offs[1], xbc_dim, p["conv_w"], p["conv_b"], tmm, MM_COL_TILE)
    dt, dtt = _dt_proj(h, w_in, offs[2], dt_b, tm)
    u_steps = _chunk_proj(h, w_tail, 0, s5_width)
    gates = _proj(h, w_tail, s5_width, 2 * d, p["gate_b"], "sigmoid_bias", tmm, MM_COL_TILE)

    y = _ssd(xbc, dt, dtt[:n_heads], p["a_log"].astype(F32), p["d_ssd"].astype(F32), n_heads,
             min(SSD_CHUNK, t))
    ya = _gated_up(y, zs, p["norm_ssd_w"], p["w_a_up"].astype(BF16), tm)

    uc, wc, bbt, a_chunk = _s5_operators(p["s5_lambda_re"], p["s5_lambda_im"], p["s5_log_dt"],
                                         p["s5_b_re"], p["s5_b_im"], p["s5_c_re"], p["s5_c_im"])
    y5 = _s5(u_steps, uc, wc, bbt, a_chunk, p["s5_d"].astype(F32), min(S5_ROWS, t // S5_CHUNK))
    yb = _glu_up(y5, p["w_glu"].astype(BF16), p["w_b_up"].astype(BF16))

    w_router = jnp.zeros((d, LANES), F32)
    w_router = w_router.at[:, :N_EXPERT_GROUPS].set(p["w_route_group"].astype(F32))
    w_router = w_router.at[:, ROUTE_EXPERT_LANE0:ROUTE_EXPERT_LANE0 + N_EXPERTS].set(
        p["w_route_expert"].astype(F32))
    r_hi = w_router.astype(BF16)
    r_lo = (w_router - r_hi.astype(F32)).astype(BF16)
    r_b = jnp.zeros((1, LANES), F32)
    r_b = r_b.at[0, :N_EXPERT_GROUPS].set(p["b_route_group"].astype(F32))
    r_b = r_b.at[0, ROUTE_EXPERT_LANE0:ROUTE_EXPERT_LANE0 + N_EXPERTS].set(p["b_route_expert"].astype(F32))
    x1, h2, logits = _merge_out(gates, ya, yb, x, p["w_out"].astype(BF16), p["norm_ffn_w"],
                                r_hi, r_lo, r_b, tm)

    route = _route(logits, tm)
    expert_ids = route[:, ROUTE_ID_LANE:ROUTE_ID_LANE + TOP_K_INNER].astype(jnp.int32)
    bm = MOE_BLOCK
    tg = min(GATHER_TILE, t)
    pos, block_expert, n_used, pad_start, pad_count, n_rows = _dispatch_plan(expert_ids, bm)
    xs = _dispatch(h2, pos, pad_start, pad_count, n_used, n_rows, tg, bm)
    y_rows = _experts(xs, block_expert, n_used, p["w_exp_gate"], p["w_exp_up"], p["w_exp_down"], bm)
    return x1, y_rows, pos, route


def kernel(x, norm_mix_w, w_in, conv_w, conv_b, dt_bias, a_log, d_ssd, norm_ssd_w, w_a_up,
           s5_lambda_re, s5_lambda_im, s5_log_dt, s5_b_re, s5_b_im, s5_c_re, s5_c_im, s5_d,
           w_glu, w_b_up, gate_b, w_out, norm_ffn_w, w_route_group, b_route_group,
           w_route_expert, b_route_expert, w_exp_gate, w_exp_up, w_exp_down, norm_final_w):
    b, seq, d = x.shape
    assert b == 1, "the scans carry state along the flattened token axis"
    depth = w_in.shape[0]
    per_layer = dict(norm_mix_w=norm_mix_w, w_in=w_in, conv_w=conv_w, conv_b=conv_b, dt_bias=dt_bias,
                     a_log=a_log, d_ssd=d_ssd, norm_ssd_w=norm_ssd_w, w_a_up=w_a_up,
                     s5_lambda_re=s5_lambda_re, s5_lambda_im=s5_lambda_im, s5_log_dt=s5_log_dt,
                     s5_b_re=s5_b_re, s5_b_im=s5_b_im, s5_c_re=s5_c_re, s5_c_im=s5_c_im, s5_d=s5_d,
                     w_glu=w_glu, w_b_up=w_b_up, gate_b=gate_b, w_out=w_out, norm_ffn_w=norm_ffn_w,
                     w_route_group=w_route_group, b_route_group=b_route_group,
                     w_route_expert=w_route_expert, b_route_expert=b_route_expert,
                     w_exp_gate=w_exp_gate, w_exp_up=w_exp_up, w_exp_down=w_exp_down)
    xt = x.reshape(b * seq, d)
    tg = min(GATHER_TILE, b * seq)
    for i in range(depth):
        p = {k: v[i] for k, v in per_layer.items()}
        x1, y_rows, pos, route = _layer(xt, p)
        xt = _combine(x1, y_rows, pos, route, norm_final_w, tg, normalize=(i == depth - 1))
    return xt.reshape(b, seq, d)
```

```python
import functools
import math

import jax
import jax.numpy as jnp
from jax import lax
from jax.experimental import pallas as pl
from jax.experimental.pallas import tpu as pltpu

F32 = jnp.float32
BF16 = jnp.bfloat16

SSD_HEAD_DIM = 64
SSD_GROUPS = 8
SSD_STATE = 128
CONV_WIDTH = 4
S5_GROUP_CH = 16
S5_STATE = 64
N_EXPERT_GROUPS = 4
EXPERTS_PER_GROUP = 8
N_EXPERTS = N_EXPERT_GROUPS * EXPERTS_PER_GROUP
TOP_K_INNER = 2
RMS_EPS = 1e-6

LANES = 128
SUBLANES = 8
VMEM_LIMIT_BYTES = 52 * 1024 * 1024

ROW_TILE = 512
MM_ROW_TILE = 1024
MM_COL_TILE = 1024
SSD_CHUNK = 128
S5_CHUNK = 16
S5_SUPER = S5_GROUP_CH * 8
S5_ROWS = 256
MOE_BLOCK = 512
GATHER_TILE = 256


def _cparams(sem, vmem=VMEM_LIMIT_BYTES):
    return pltpu.CompilerParams(dimension_semantics=sem, vmem_limit_bytes=vmem)


def _sigmoid(v):
    return 1.0 / (1.0 + jnp.exp(-v))


def _silu(v):
    return v * _sigmoid(v)


def _softplus(v):
    return jnp.maximum(v, 0.0) + jnp.log(1.0 + jnp.exp(-jnp.abs(v)))


def _gelu_tanh(v):
    c = math.sqrt(2.0 / math.pi)
    return 0.5 * v * (1.0 + jnp.tanh(c * (v + 0.044715 * (v * v * v))))


def _rms(v, w):
    ms = jnp.mean(v * v, axis=-1, keepdims=True)
    return v * lax.rsqrt(ms + RMS_EPS) * w


def _pack_bf16_pairs(v):
    n = v.shape[1] // 2
    lo = pltpu.bitcast(v[:, :n].astype(BF16).astype(F32), jnp.uint32)
    hi = pltpu.bitcast(v[:, n:].astype(BF16).astype(F32), jnp.uint32)
    return hi | (lo >> 16)


def _unpack_bf16_pairs(w):
    lo = pltpu.bitcast(w << 16, F32)
    hi = pltpu.bitcast(w & jnp.uint32(0xFFFF0000), F32)
    return jnp.concatenate([lo, hi], axis=1)


def _store_token_tiles(ref, rows):
    for a in range(ref.shape[-2]):
        ref[:, a, :] = rows[:, a * LANES:(a + 1) * LANES]


def _load_token_tiles(ref):
    return jnp.concatenate([ref[:, a, :] for a in range(ref.shape[-2])], axis=1)


def _rmsnorm_kernel(x_ref, w_ref, o_ref):
    o_ref[...] = _rms(x_ref[...], w_ref[...]).astype(o_ref.dtype)


def _rmsnorm(x, w, tm):
    t, d = x.shape
    return pl.pallas_call(
        _rmsnorm_kernel,
        grid=(t // tm,),
        in_specs=[pl.BlockSpec((tm, d), lambda i: (i, 0)),
                  pl.BlockSpec((1, d), lambda i: (0, 0))],
        out_specs=pl.BlockSpec((tm, d), lambda i: (i, 0)),
        out_shape=jax.ShapeDtypeStruct((t, d), BF16),
        compiler_params=_cparams(("parallel",)),
        name="rmsnorm",
    )(x, w.reshape(1, d))


def _resident_weight(w_ref, wbf_ref, row_axis=1):
    @pl.when(pl.program_id(row_axis) == 0)
    def _():
        wbf_ref[...] = w_ref[...].astype(BF16)
    return wbf_ref


def _proj_kernel(a_ref, w_ref, b_ref, o_ref, *scratch, act):
    w = _resident_weight(w_ref, scratch[0]) if scratch else w_ref
    p = jnp.dot(a_ref[...], w[...], preferred_element_type=F32)
    if act == "silu":
        p = _silu(p)
    elif act == "sigmoid_bias":
        p = _sigmoid(p + b_ref[...])
    o_ref[...] = p.astype(o_ref.dtype)


def _proj(h, w, col0, n, b, act, tm, tn):
    t, k = h.shape
    tn = min(tn, n)
    assert col0 % tn == 0 and n % tn == 0
    jb = col0 // tn
    if b is None:
        b = jnp.zeros((1, n), F32)
    scratch = [pltpu.VMEM((k, tn), BF16)] if w.dtype != BF16 else []
    return pl.pallas_call(
        functools.partial(_proj_kernel, act=act),
        grid=(n // tn, t // tm),
        in_specs=[pl.BlockSpec((tm, k), lambda j, i: (i, 0)),
                  pl.BlockSpec((k, tn), lambda j, i: (0, jb + j)),
                  pl.BlockSpec((1, tn), lambda j, i: (0, j))],
        out_specs=pl.BlockSpec((tm, tn), lambda j, i: (i, j)),
        out_shape=jax.ShapeDtypeStruct((t, n), BF16),
        scratch_shapes=scratch,
        compiler_params=_cparams(("parallel", "arbitrary")),
        name="proj_" + act,
    )(h, w, b.reshape(1, n))


def _conv_proj_kernel(a_ref, w_ref, cw_ref, cb_ref, o_ref, ext_ref, wbf_ref):
    tm = a_ref.shape[0]
    halo = SUBLANES

    @pl.when(pl.program_id(1) == 0)
    def _():
        ext_ref[pl.ds(0, halo), :] = jnp.zeros((halo, ext_ref.shape[1]), F32)

    w = _resident_weight(w_ref, wbf_ref)
    p = jnp.dot(a_ref[...], w[...], preferred_element_type=F32)
    ext_ref[pl.ds(halo, tm), :] = p
    acc = cb_ref[...] + cw_ref[CONV_WIDTH - 1:CONV_WIDTH, :] * p
    for k in range(CONV_WIDTH - 1):
        back = CONV_WIDTH - 1 - k
        acc = acc + cw_ref[k:k + 1, :] * ext_ref[pl.ds(halo - back, tm), :]
    o_ref[...] = _silu(acc).astype(o_ref.dtype)
    ext_ref[pl.ds(0, halo), :] = p[tm - halo:, :]


def _conv_proj(h, w, col0, n, conv_w, conv_b, tm, tn):
    t, k = h.shape
    assert col0 % tn == 0 and n % tn == 0
    jb = col0 // tn
    return pl.pallas_call(
        _conv_proj_kernel,
        grid=(n // tn, t // tm),
        in_specs=[pl.BlockSpec((tm, k), lambda j, i: (i, 0)),
                  pl.BlockSpec((k, tn), lambda j, i: (0, jb + j)),
                  pl.BlockSpec((CONV_WIDTH, tn), lambda j, i: (0, j)),
                  pl.BlockSpec((1, tn), lambda j, i: (0, j))],
        out_specs=pl.BlockSpec((tm, tn), lambda j, i: (i, j)),
        out_shape=jax.ShapeDtypeStruct((t, n), BF16),
        scratch_shapes=[pltpu.VMEM((tm + SUBLANES, tn), F32),
                        pltpu.VMEM((k, tn), BF16)],
        compiler_params=_cparams(("parallel", "arbitrary")),
        name="proj_conv",
    )(h, w, conv_w, conv_b.reshape(1, n))


def _dt_proj_kernel(a_ref, w_ref, b_ref, dt_ref, dtt_ref):
    p = jnp.dot(a_ref[...], w_ref[...].astype(BF16), preferred_element_type=F32) + b_ref[...]
    dt = _softplus(p)
    dt_ref[...] = dt
    dtt_ref[...] = dt.T


def _dt_proj(h, w, col0, b_pad, tm):
    t, k = h.shape
    jb = col0 // LANES
    return pl.pallas_call(
        _dt_proj_kernel,
        grid=(t // tm,),
        in_specs=[pl.BlockSpec((tm, k), lambda i: (i, 0)),
                  pl.BlockSpec((k, LANES), lambda i: (0, jb)),
                  pl.BlockSpec((1, LANES), lambda i: (0, 0))],
        out_specs=[pl.BlockSpec((tm, LANES), lambda i: (i, 0)),
                   pl.BlockSpec((LANES, tm), lambda i: (0, i))],
        out_shape=[jax.ShapeDtypeStruct((t, LANES), F32),
                   jax.ShapeDtypeStruct((LANES, t), F32)],
        compiler_params=_cparams(("parallel",)),
        name="proj_dt",
    )(h, w, b_pad)


def _step_major_perm(n_rows):
    assert n_rows == S5_CHUNK * S5_CHUNK
    shift = S5_CHUNK.bit_length() - 1
    row = lax.broadcasted_iota(jnp.int32, (n_rows, n_rows), 0)
    col = lax.broadcasted_iota(jnp.int32, (n_rows, n_rows), 1)
    swapped = ((row & (S5_CHUNK - 1)) << shift) | (row >> shift)
    return jnp.where(col == swapped, 1.0, 0.0).astype(BF16)


def _chunk_proj_kernel(a_ref, w_ref, o_ref, wbf_ref):
    w = _resident_weight(w_ref, wbf_ref, row_axis=0)
    u = jnp.dot(a_ref[...], w[...], preferred_element_type=F32).astype(BF16)
    u_steps = jnp.dot(_step_major_perm(u.shape[0]), u, preferred_element_type=F32).astype(o_ref.dtype)
    nj = o_ref.shape[1]
    for s in range(o_ref.shape[0]):
        o_ref[s] = u_steps[s * nj:(s + 1) * nj, :]


def _chunk_proj(h, w, col0, n):
    t, k = h.shape
    nj = t // S5_CHUNK
    tm = S5_CHUNK * S5_CHUNK
    assert col0 % n == 0
    jb = col0 // n
    return pl.pallas_call(
        _chunk_proj_kernel,
        grid=(t // tm,),
        in_specs=[pl.BlockSpec((tm, k), lambda i: (i, 0)),
                  pl.BlockSpec((k, n), lambda i: (0, jb))],
        out_specs=pl.BlockSpec((S5_CHUNK, S5_CHUNK, n), lambda i: (0, i, 0)),
        out_shape=jax.ShapeDtypeStruct((S5_CHUNK, nj, n), BF16),
        scratch_shapes=[pltpu.VMEM((k, n), BF16)],
        compiler_params=_cparams(("arbitrary",)),
        name="proj_u",
    )(h, w)


def _cumsum_rows(v):
    n = v.shape[0]
    idx = lax.broadcasted_iota(jnp.int32, v.shape, 0)
    k = 1
    while k < n:
        v = v + jnp.where(idx >= k, pltpu.roll(v, k, 0), 0.0)
        k *= 2
    return v


def _cumsum_lanes(v):
    n = v.shape[1]
    idx = lax.broadcasted_iota(jnp.int32, v.shape, 1)
    k = 1
    while k < n:
        v = v + jnp.where(idx >= k, pltpu.roll(v, k, 1), 0.0)
        k *= 2
    return v


def _ssd_kernel(xbc_ref, dt_ref, dtt_ref, alog_r_ref, alog_c_ref, dskip_ref, o_ref, state_ref, *,
                n_heads):
    q = dt_ref.shape[0]
    n = SSD_STATE
    p_dim = SSD_HEAD_DIM
    r_heads = n_heads // SSD_GROUPS
    gw = r_heads * p_dim
    inner = n_heads * p_dim

    @pl.when(pl.program_id(0) == 0)
    def _():
        state_ref[...] = jnp.zeros(state_ref.shape, F32)

    dtt = dtt_ref[...]
    cs_col = _cumsum_rows(dt_ref[...] * -jnp.exp(alog_r_ref[...]))
    cs_row = _cumsum_lanes(dtt * -jnp.exp(alog_c_ref[...]))
    causal = lax.broadcasted_iota(jnp.int32, (q, q), 0) >= lax.broadcasted_iota(jnp.int32, (q, q), 1)
    head_of_lane = lax.broadcasted_iota(jnp.int32, (1, gw), 1) // p_dim
    eye = jnp.where(lax.broadcasted_iota(jnp.int32, (n, n), 0) == lax.broadcasted_iota(jnp.int32, (n, n), 1),
                    1.0, 0.0).astype(BF16)
    nt = (((1,), (1,)), ((), ()))

    for g in range(SSD_GROUPS):
        x_g = xbc_ref[:, g * gw:(g + 1) * gw]
        b_g = xbc_ref[:, inner + g * n:inner + (g + 1) * n]
        c_g = xbc_ref[:, inner + (SSD_GROUPS + g) * n:inner + (SSD_GROUPS + g + 1) * n]
        cb = lax.dot_general(c_g, b_g, nt, preferred_element_type=F32)
        b_t = lax.dot_general(eye, b_g, nt, preferred_element_type=F32)
        c_f = c_g.astype(F32)
        s_g = state_ref[g]
        lhs_parts, bt_parts, cd = [], [], jnp.zeros((1, gw), F32)
        for r in range(r_heads):
            h = g * r_heads + r
            csb = jnp.broadcast_to(cs_col[:, h:h + 1], (q, n))
            csr = cs_row[h:h + 1, :]
            dtr = dtt[h:h + 1, :]
            cs_last = csr[:, q - 1:q]
            decay = jnp.exp(jnp.where(causal, csb - csr, -1e30))
            lhs_parts.append(jnp.concatenate([cb * decay * dtr, c_f * jnp.exp(csb)], axis=1).astype(BF16))
            bt_parts.append((b_t * (dtr * jnp.exp(cs_last - csr))).astype(BF16))
            cd = jnp.where(head_of_lane == r, jnp.exp(cs_last), cd)
        rhs = jnp.concatenate([x_g, s_g.astype(BF16)], axis=0)
        y_all = jnp.dot(jnp.concatenate(lhs_parts, axis=0), rhs, preferred_element_type=F32)
        s_all = jnp.dot(jnp.concatenate(bt_parts, axis=0), x_g, preferred_element_type=F32)
        y_g = jnp.zeros((q, gw), F32)
        s_new = jnp.zeros((n, gw), F32)
        for r in range(r_heads):
            mine = head_of_lane == r
            y_g = jnp.where(mine, y_all[r * q:(r + 1) * q, :], y_g)
            s_new = jnp.where(mine, s_all[r * n:(r + 1) * n, :], s_new)
        state_ref[g] = s_g * cd + s_new
        y_g = y_g + x_g.astype(F32) * dskip_ref[:, g * gw:(g + 1) * gw]
        o_ref[:, g * gw:(g + 1) * gw] = y_g.astype(o_ref.dtype)


def _ssd(xbc, dt, dtt, a_log, d_skip, n_heads, q):
    t, width = xbc.shape
    inner = n_heads * SSD_HEAD_DIM
    gw = inner // SSD_GROUPS
    assert n_heads % SUBLANES == 0 and n_heads <= LANES
    alog_r = jnp.zeros((1, LANES), F32).at[0, :n_heads].set(a_log)
    alog_c = a_log.reshape(n_heads, 1)
    dskip = jnp.repeat(d_skip, SSD_HEAD_DIM).reshape(1, inner)
    return pl.pallas_call(
        functools.partial(_ssd_kernel, n_heads=n_heads),
        grid=(t // q,),
        in_specs=[pl.BlockSpec((q, width), lambda c: (c, 0)),
                  pl.BlockSpec((q, LANES), lambda c: (c, 0)),
                  pl.BlockSpec((n_heads, q), lambda c: (0, c)),
                  pl.BlockSpec((1, LANES), lambda c: (0, 0)),
                  pl.BlockSpec((n_heads, 1), lambda c: (0, 0)),
                  pl.BlockSpec((1, inner), lambda c: (0, 0))],
        out_specs=pl.BlockSpec((q, inner), lambda c: (c, 0)),
        out_shape=jax.ShapeDtypeStruct((t, inner), BF16),
        scratch_shapes=[pltpu.VMEM((SSD_GROUPS, SSD_STATE, gw), F32)],
        compiler_params=_cparams(("arbitrary",)),
        name="ssd_scan",
    )(xbc, dt, dtt, alog_r, alog_c, dskip)


def _gated_up_kernel(y_ref, z_ref, nw_ref, w_ref, o_ref):
    v = y_ref[...].astype(F32) * z_ref[...].astype(F32)
    na = _rms(v, nw_ref[...]).astype(BF16)
    o_ref[...] = jnp.dot(na, w_ref[...], preferred_element_type=F32).astype(o_ref.dtype)


def _gated_up(y, zs, norm_w, w, tm):
    t, d = y.shape
    n = w.shape[1]
    return pl.pallas_call(
        _gated_up_kernel,
        grid=(t // tm,),
        in_specs=[pl.BlockSpec((tm, d), lambda i: (i, 0)),
                  pl.BlockSpec((tm, d), lambda i: (i, 0)),
                  pl.BlockSpec((1, d), lambda i: (0, 0)),
                  pl.BlockSpec((d, n), lambda i: (0, 0))],
        out_specs=pl.BlockSpec((tm, n), lambda i: (i, 0)),
        out_shape=jax.ShapeDtypeStruct((t, n), BF16),
        compiler_params=_cparams(("parallel",)),
        name="ssd_gated_up",
    )(y, zs, norm_w.reshape(1, d), w)


def _s5_operators(lam_re, lam_im, log_dt, b_re, b_im, c_re, c_im):
    ng, ns = lam_re.shape
    nc = S5_GROUP_CH
    L = S5_CHUNK
    per = S5_SUPER // nc
    nsg = ng // per
    lr, li = lam_re.astype(F32), lam_im.astype(F32)
    dt = jnp.exp(log_dt.astype(F32))[:, None]
    mag = jnp.exp(lr * dt)
    ang = li * dt
    abar_r, abar_i = mag * jnp.cos(ang), mag * jnp.sin(ang)
    den = lr * lr + li * li
    nr, ni = abar_r - 1.0, abar_i
    coef_r = (nr * lr + ni * li) / den
    coef_i = (ni * lr - nr * li) / den
    bre, bim = b_re.astype(F32), b_im.astype(F32)
    bb_r = coef_r[..., None] * bre - coef_i[..., None] * bim
    bb_i = coef_r[..., None] * bim + coef_i[..., None] * bre
    cre, cim = c_re.astype(F32), c_im.astype(F32)
    ks = jnp.arange(L + 1, dtype=F32)[:, None, None]
    pmag = jnp.exp(ks * (lr * dt)[None])
    pang = ks * ang[None]
    pw_r, pw_i = pmag * jnp.cos(pang), pmag * jnp.sin(pang)
    ca_r = cre[None] * pw_r[:, :, None, :] - cim[None] * pw_i[:, :, None, :]
    ca_i = cre[None] * pw_i[:, :, None, :] + cim[None] * pw_r[:, :, None, :]
    def _rows_n(v, steps):
        return v.reshape(steps, nsg, per, nc, ns).transpose(1, 4, 0, 2, 3).reshape(nsg, ns, steps * per * nc)

    uc = jnp.concatenate([_rows_n(ca_r, L + 1), -_rows_n(ca_i, L + 1)], axis=1)
    ks_rev = (L - 1) - jnp.arange(L, dtype=F32)[:, None, None]
    rmag = jnp.exp(ks_rev * (lr * dt)[None])
    rang = ks_rev * ang[None]
    rev_r, rev_i = rmag * jnp.cos(rang), rmag * jnp.sin(rang)
    ab_r = rev_r[..., None] * bb_r[None] - rev_i[..., None] * bb_i[None]
    ab_i = rev_r[..., None] * bb_i[None] + rev_i[..., None] * bb_r[None]
    ab_rt, ab_it = ab_r.transpose(0, 1, 3, 2), ab_i.transpose(0, 1, 3, 2)
    wc = jnp.concatenate([_rows_n(ab_rt, L), _rows_n(ab_it, L)], axis=1)

    def _rows_gc(v):
        return v.reshape(nsg, per, ns, nc).transpose(0, 1, 3, 2).reshape(nsg, per * nc, ns)

    bbt = jnp.concatenate([_rows_gc(bb_r), _rows_gc(bb_i)], axis=2)
    a_chunk = jnp.concatenate([pw_r[L].reshape(nsg, 1, per * ns),
                               pw_i[L].reshape(nsg, 1, per * ns)], axis=2)
    return uc, wc, bbt, a_chunk


def _split_bf16(v):
    hi = v.astype(BF16)
    return hi, (v - hi.astype(F32)).astype(BF16)


def _s5_kernel(u_ref, uc_ref, wc_ref, bbt_ref, ach_ref, dsk_ref, o_ref,
               toep_ref, wt_ref, v_ref, x_ref, sp_ref, carry_ref):
    jb = pl.program_id(1)
    nl, nj, cw = u_ref.shape
    half = carry_ref.shape[1] // 2
    per = cw // S5_GROUP_CH
    ns = half // per
    nt = (((1,), (1,)), ((), ()))

    @pl.when(jb == 0)
    def _():
        carry_ref[...] = jnp.zeros(carry_ref.shape, F32)
        uc = uc_ref[...]
        wc = wc_ref[...]
        b_hi, b_lo = _split_bf16(bbt_ref[...])
        u_hi, u_lo = _split_bf16(uc[:, :nl * cw])
        kall = (jnp.dot(b_hi, u_hi, preferred_element_type=F32)
                + jnp.dot(b_lo, u_hi, preferred_element_type=F32)
                + jnp.dot(b_hi, u_lo, preferred_element_type=F32))
        row_g = lax.broadcasted_iota(jnp.int32, (cw, 1), 0) // S5_GROUP_CH
        col_g = (lax.broadcasted_iota(jnp.int32, (1, nl * cw), 1) // S5_GROUP_CH) % per
        kall = jnp.where(row_g == col_g, kall, 0.0).astype(BF16)
        toep_ref[...] = jnp.zeros(toep_ref.shape, toep_ref.dtype)
        for s_in in range(nl):
            for s_out in range(s_in, nl):
                k = s_out - s_in
                toep_ref[pl.ds(s_in * cw, cw), pl.ds(s_out * cw, cw)] = kall[:, k * cw:(k + 1) * cw]
        for gp in range(per):
            mine = col_g == gp
            for part in range(2):
                rows = pl.ds(part * half + gp * ns, ns)
                src = slice(part * ns, (part + 1) * ns)
                v_ref[rows, :] = jnp.where(mine, uc[src, cw:], 0.0).astype(BF16)
                wt_ref[rows, :] = jnp.where(mine, wc[src, :], 0.0).astype(BF16)

    lhs = jnp.concatenate([u_ref[s] for s in range(nl)], axis=1)
    x_ref[...] = lax.dot_general(lhs, wt_ref[...], nt, preferred_element_type=F32)

    a_re = ach_ref[:, :half]
    a_im = ach_ref[:, half:]

    def step(j, carry):
        s_re, s_im = carry
        sp_ref[pl.ds(j, 1), :half] = s_re
        sp_ref[pl.ds(j, 1), half:] = s_im
        xr = x_ref[pl.ds(j, 1), :half]
        xi = x_ref[pl.ds(j, 1), half:]
        return (a_re * s_re - a_im * s_im + xr, a_re * s_im + a_im * s_re + xi)

    s_re, s_im = lax.fori_loop(0, nj, step, (carry_ref[:, :half], carry_ref[:, half:]), unroll=8)
    carry_ref[:, :half] = s_re
    carry_ref[:, half:] = s_im

    y_state = jnp.dot(sp_ref[...].astype(BF16), v_ref[...], preferred_element_type=F32)
    dsk = dsk_ref[...]
    pair = 2 * cw
    for tp in range(nl // 2):
        kdim = pair * (tp + 1)
        y = jnp.dot(lhs[:, :kdim], toep_ref[pl.ds(0, kdim), pl.ds(tp * pair, pair)],
                    preferred_element_type=F32)
        y = y + y_state[:, tp * pair:(tp + 1) * pair]
        for h in range(2):
            s = 2 * tp + h
            o_ref[s] = (y[:, h * cw:(h + 1) * cw] + dsk * u_ref[s].astype(F32)).astype(o_ref.dtype)


def _s5(u_steps, uc, wc, bbt, a_chunk, d_skip, tj):
    nl, nj, width = u_steps.shape
    nsg = width // S5_SUPER
    nstate = a_chunk.shape[2]
    rows = uc.shape[1]
    return pl.pallas_call(
        _s5_kernel,
        grid=(nsg, nj // tj),
        in_specs=[pl.BlockSpec((nl, tj, S5_SUPER), lambda g, j: (0, j, g)),
                  pl.BlockSpec((None, rows, (nl + 1) * S5_SUPER), lambda g, j: (g, 0, 0)),
                  pl.BlockSpec((None, rows, nl * S5_SUPER), lambda g, j: (g, 0, 0)),
                  pl.BlockSpec((None, S5_SUPER, rows), lambda g, j: (g, 0, 0)),
                  pl.BlockSpec((None, 1, nstate), lambda g, j: (g, 0, 0)),
                  pl.BlockSpec((None, 1, S5_SUPER), lambda g, j: (g, 0, 0))],
        out_specs=pl.BlockSpec((nl, tj, S5_SUPER), lambda g, j: (0, j, g)),
        out_shape=jax.ShapeDtypeStruct((nl, nj, width), BF16),
        scratch_shapes=[pltpu.VMEM((nl * S5_SUPER, nl * S5_SUPER), BF16),
                        pltpu.VMEM((nstate, nl * S5_SUPER), BF16),
                        pltpu.VMEM((nstate, nl * S5_SUPER), BF16),
                        pltpu.VMEM((tj, nstate), F32),
                        pltpu.VMEM((tj, nstate), F32),
                        pltpu.VMEM((1, nstate), F32)],
        compiler_params=_cparams(("parallel", "arbitrary")),
        name="s5_scan",
    )(u_steps, uc, wc, bbt, a_chunk, d_skip.reshape(nsg, 1, S5_SUPER))


def _glu_up_kernel(y_ref, wg_ref, wu_ref, o_ref):
    y_steps = jnp.concatenate([y_ref[s] for s in range(y_ref.shape[0])], axis=0)
    y = jnp.dot(_step_major_perm(y_steps.shape[0]), y_steps, preferred_element_type=F32)
    v = _gelu_tanh(y)
    gate = _sigmoid(jnp.dot(v.astype(BF16), wg_ref[...], preferred_element_type=F32))
    o_ref[...] = jnp.dot((v * gate).astype(BF16), wu_ref[...],
                         preferred_element_type=F32).astype(o_ref.dtype)


def _glu_up(y_steps, w_glu, w_up):
    nl, nj, width = y_steps.shape
    n = w_up.shape[1]
    tm = S5_CHUNK * S5_CHUNK
    return pl.pallas_call(
        _glu_up_kernel,
        grid=(nj // S5_CHUNK,),
        in_specs=[pl.BlockSpec((nl, S5_CHUNK, width), lambda i: (0, i, 0)),
                  pl.BlockSpec((width, width), lambda i: (0, 0)),
                  pl.BlockSpec((width, n), lambda i: (0, 0))],
        out_specs=pl.BlockSpec((tm, n), lambda i: (i, 0)),
        out_shape=jax.ShapeDtypeStruct((nj * nl, n), BF16),
        compiler_params=_cparams(("parallel",)),
        name="s5_glu_up",
    )(y_steps, w_glu, w_up)


def _merge_out_kernel(ga_ref, gb_ref, ya_ref, yb_ref, x_ref, w_ref, nw_ref, rhi_ref, rlo_ref, rb_ref,
                      x1_ref, h2_ref, lg_ref):
    merged = (ga_ref[...].astype(F32) * ya_ref[...].astype(F32)
              + gb_ref[...].astype(F32) * yb_ref[...].astype(F32))
    x1 = x_ref[...] + jnp.dot(merged.astype(BF16), w_ref[...], preferred_element_type=F32)
    x1_ref[...] = x1
    h2 = _rms(x1, nw_ref[...])
    h2_ref[...] = _pack_bf16_pairs(h2)
    hi = h2.astype(BF16)
    lo = (h2 - hi.astype(F32)).astype(BF16)
    lg = (jnp.dot(hi, rhi_ref[...], preferred_element_type=F32)
          + jnp.dot(lo, rhi_ref[...], preferred_element_type=F32)
          + jnp.dot(hi, rlo_ref[...], preferred_element_type=F32))
    lg_ref[...] = _route_math(lg + rb_ref[...])


def _merge_out(gates, ya, yb, x, w_out, norm_w, r_hi, r_lo, r_b, tm):
    t, d = x.shape
    row = lambda i: (i, 0)
    full = lambda i: (0, 0)
    return pl.pallas_call(
        _merge_out_kernel,
        grid=(t // tm,),
        in_specs=[pl.BlockSpec((tm, d), row),
                  pl.BlockSpec((tm, d), lambda i: (i, 1)),
                  pl.BlockSpec((tm, d), row),
                  pl.BlockSpec((tm, d), row),
                  pl.BlockSpec((tm, d), row),
                  pl.BlockSpec((d, d), full),
                  pl.BlockSpec((1, d), full),
                  pl.BlockSpec((d, LANES), full),
                  pl.BlockSpec((d, LANES), full),
                  pl.BlockSpec((1, LANES), full)],
        out_specs=[pl.BlockSpec((tm, d), row),
                   pl.BlockSpec((tm, d // 2), row),
                   pl.BlockSpec((tm, LANES), row)],
        out_shape=[jax.ShapeDtypeStruct((t, d), F32),
                   jax.ShapeDtypeStruct((t, d // 2), jnp.uint32),
                   jax.ShapeDtypeStruct((t, LANES), F32)],
        compiler_params=_cparams(("parallel",)),
        name="merge_out",
    )(gates, gates, ya, yb, x, w_out, norm_w.reshape(1, d), r_hi, r_lo, r_b)


ROUTE_ID_LANE = 0
ROUTE_W_LANE = TOP_K_INNER
ROUTE_EXPERT_LANE0 = SUBLANES


def _route_math(lg):
    lane = lax.broadcasted_iota(jnp.int32, lg.shape, 1)
    neg = -jnp.inf
    big = LANES
    is_g = lane < N_EXPERT_GROUPS
    gl = jnp.where(is_g, lg, neg)
    gmax = jnp.max(gl, axis=1, keepdims=True)
    grp = jnp.min(jnp.where(gl == gmax, lane, big), axis=1, keepdims=True)
    pg_sel = 1.0 / jnp.sum(jnp.where(is_g, jnp.exp(lg - gmax), 0.0), axis=1, keepdims=True)
    e_lo = ROUTE_EXPERT_LANE0 + grp * EXPERTS_PER_GROUP
    in_grp = (lane >= e_lo) & (lane < e_lo + EXPERTS_PER_GROUP)
    el = jnp.where(in_grp, lg, neg)
    v0 = jnp.max(el, axis=1, keepdims=True)
    i0 = jnp.min(jnp.where(el == v0, lane, big), axis=1, keepdims=True)
    el1 = jnp.where(lane == i0, neg, el)
    v1 = jnp.max(el1, axis=1, keepdims=True)
    i1 = jnp.min(jnp.where(el1 == v1, lane, big), axis=1, keepdims=True)
    e1w = jnp.exp(v1 - v0)
    w0 = pg_sel / (1.0 + e1w)
    w1 = pg_sel * e1w / (1.0 + e1w)
    return jnp.where(lane == ROUTE_ID_LANE, (i0 - ROUTE_EXPERT_LANE0).astype(F32),
           jnp.where(lane == ROUTE_ID_LANE + 1, (i1 - ROUTE_EXPERT_LANE0).astype(F32),
           jnp.where(lane == ROUTE_W_LANE, w0, jnp.where(lane == ROUTE_W_LANE + 1, w1, 0.0))))


def _dispatch_plan(expert_ids, bm):
    n_tokens = expert_ids.shape[0]
    n_assign = n_tokens * TOP_K_INNER
    eid = expert_ids.reshape(n_assign)
    experts = jnp.arange(N_EXPERTS, dtype=jnp.int32)
    onehot = (eid[:, None] == experts[None, :]).astype(jnp.int32)
    csum = jnp.cumsum(onehot, axis=0)
    counts = csum[-1]
    rank = jnp.sum(onehot * csum, axis=1) - 1
    padded = ((counts + bm - 1) // bm) * bm
    ends = jnp.cumsum(padded)
    starts = ends - padded
    dest = jnp.sum(onehot * starts[None, :], axis=1) + rank
    n_rows = n_assign + N_EXPERTS * bm
    nb = n_rows // bm
    block_start = jnp.arange(nb, dtype=jnp.int32) * bm
    block_expert = jnp.minimum(jnp.sum((ends[None, :] <= block_start[:, None]).astype(jnp.int32), axis=1),
                               N_EXPERTS - 1)
    n_used = (ends[-1] // bm).astype(jnp.int32).reshape(1)
    return (dest.reshape(n_tokens, TOP_K_INNER), block_expert, n_used,
            (starts + counts).astype(jnp.int32), (padded - counts).astype(jnp.int32), n_rows)


def _dispatch_kernel(pad_start_ref, pad_count_ref, nused_ref, dest_ref, h_ref, xs_hbm, stage, zrow, sem, zsem):
    i = pl.program_id(0)
    n_steps = pl.num_programs(0)
    tm = h_ref.shape[0]
    bm = zrow.shape[0]
    nb = xs_hbm.shape[0] // bm
    slot = i % 2

    def tile_wait(s):
        for _ in range(TOP_K_INNER):
            pltpu.make_async_copy(stage.at[s], xs_hbm.at[pl.ds(0, tm)], sem.at[s]).wait()

    @pl.when(i >= 2)
    def _():
        tile_wait(slot)

    _store_token_tiles(stage.at[slot], h_ref[...])

    for r in range(tm):
        for k in range(TOP_K_INNER):
            pltpu.make_async_copy(stage.at[slot, r], xs_hbm.at[dest_ref[k, r]], sem.at[slot]).start()

    @pl.when(i == n_steps - 1)
    def _():
        tile_wait(slot)

        @pl.when(n_steps >= 2)
        def _():
            tile_wait(1 - slot)

        zrow[...] = jnp.zeros(zrow.shape, zrow.dtype)

        def pad_copy(e, r):
            return pltpu.make_async_copy(zrow.at[0], xs_hbm.at[pad_start_ref[e] + r], zsem)

        def block_copy(b):
            return pltpu.make_async_copy(zrow, xs_hbm.at[pl.ds(b * bm, bm)], zsem)
        for e in range(N_EXPERTS):
            lax.fori_loop(0, pad_count_ref[e], lambda r, c, e=e: (pad_copy(e, r).start(), c)[1], 0)
        lax.fori_loop(nused_ref[0], nb, lambda b, c: (block_copy(b).start(), c)[1], 0)
        for e in range(N_EXPERTS):
            lax.fori_loop(0, pad_count_ref[e], lambda r, c, e=e: (pad_copy(e, r).wait(), c)[1], 0)
        lax.fori_loop(nused_ref[0], nb, lambda b, c: (block_copy(b).wait(), c)[1], 0)


def _dispatch(h2, dest, pad_start, pad_count, n_used, n_rows, tm, bm):
    t, width = h2.shape
    tile = (width // LANES, LANES)
    nt = t // tm
    dest_t = dest.reshape(nt, tm, TOP_K_INNER).transpose(0, 2, 1)
    grid_spec = pltpu.PrefetchScalarGridSpec(
        num_scalar_prefetch=3,
        grid=(nt,),
        in_specs=[pl.BlockSpec((None, TOP_K_INNER, tm), lambda i, *_: (i, 0, 0), memory_space=pltpu.SMEM),
                  pl.BlockSpec((tm, width), lambda i, *_: (i, 0))],
        out_specs=pl.BlockSpec(memory_space=pl.ANY),
        scratch_shapes=[pltpu.VMEM((2, tm) + tile, h2.dtype),
                        pltpu.VMEM((bm,) + tile, h2.dtype),
                        pltpu.SemaphoreType.DMA((2,)),
                        pltpu.SemaphoreType.DMA(())],
    )
    return pl.pallas_call(
        _dispatch_kernel,
        grid_spec=grid_spec,
        out_shape=jax.ShapeDtypeStruct((n_rows,) + tile, h2.dtype),
        compiler_params=_cparams(("arbitrary",)),
        name="moe_dispatch",
    )(pad_start, pad_count, n_used, dest_t, h2)


def _experts_kernel(bexp_ref, nused_ref, first_ref, next_ref, slot_ref, xs_ref, wg_hbm, wu_hbm, wd_hbm,
                    o_ref, wg_f, wu_f, wd_f, wg_s, wu_s, wd_s, sem):
    b = pl.program_id(0)
    n_used = nused_ref[0]

    def weight_copies(e, s):
        return (pltpu.make_async_copy(wg_hbm.at[e], wg_f.at[s], sem.at[s, 0]),
                pltpu.make_async_copy(wu_hbm.at[e], wu_f.at[s], sem.at[s, 1]),
                pltpu.make_async_copy(wd_hbm.at[e], wd_f.at[s], sem.at[s, 2]))

    @pl.when((b == 0) & (n_used > 0))
    def _():
        for c in weight_copies(bexp_ref[0], 0):
            c.start()

    @pl.when((b < n_used) & (first_ref[b] == 1))
    def _():
        s = slot_ref[b]
        for c in weight_copies(bexp_ref[b], s):
            c.wait()

        @pl.when(next_ref[b] >= 0)
        def _():
            for c in weight_copies(next_ref[b], 1 - s):
                c.start()

        wg_s[...] = wg_f[s].astype(BF16)
        wu_s[...] = wu_f[s].astype(BF16)
        wd_s[...] = wd_f[s].astype(BF16)

    @pl.when(b < n_used)
    def _():
        xb = _unpack_bf16_pairs(_load_token_tiles(xs_ref)).astype(BF16)
        hg = jnp.dot(xb, wg_s[...], preferred_element_type=F32)
        hu = jnp.dot(xb, wu_s[...], preferred_element_type=F32)
        act = (_silu(hg) * hu).astype(BF16)
        _store_token_tiles(o_ref, _pack_bf16_pairs(jnp.dot(act, wd_s[...], preferred_element_type=F32)))

    @pl.when(b >= n_used)
    def _():
        o_ref[...] = jnp.zeros(o_ref.shape, o_ref.dtype)


def _experts(xs, block_expert, n_used, w_g, w_u, w_d, bm):
    n_rows = xs.shape[0]
    tile = xs.shape[1:]
    d, ff = w_g.shape[1], w_g.shape[2]
    assert d == 2 * tile[0] * tile[1]
    nb = n_rows // bm
    blk = jnp.arange(nb, dtype=jnp.int32)
    valid = blk < n_used[0]
    first = (valid & ((blk == 0) | (block_expert != jnp.roll(block_expert, 1)))).astype(jnp.int32)
    later = valid[None, :] & (block_expert[None, :] > block_expert[:, None])
    nxt = jnp.min(jnp.where(later, block_expert[None, :], N_EXPERTS), axis=1)
    nxt = jnp.where(nxt == N_EXPERTS, -1, nxt).astype(jnp.int32)
    slot = ((jnp.cumsum(first) - 1) % 2).astype(jnp.int32)

    def used(b, nu):
        return jnp.minimum(b, jnp.maximum(nu[0] - 1, 0))

    grid_spec = pltpu.PrefetchScalarGridSpec(
        num_scalar_prefetch=5,
        grid=(nb,),
        in_specs=[pl.BlockSpec((bm,) + tile, lambda b, be, nu, *_: (used(b, nu), 0, 0)),
                  pl.BlockSpec(memory_space=pl.ANY),
                  pl.BlockSpec(memory_space=pl.ANY),
                  pl.BlockSpec(memory_space=pl.ANY)],
        out_specs=pl.BlockSpec((bm,) + tile, lambda b, *_: (b, 0, 0)),
        scratch_shapes=[pltpu.VMEM((2, d, ff), w_g.dtype),
                        pltpu.VMEM((2, d, ff), w_u.dtype),
                        pltpu.VMEM((2, ff, d), w_d.dtype),
                        pltpu.VMEM((d, ff), BF16),
                        pltpu.VMEM((d, ff), BF16),
                        pltpu.VMEM((ff, d), BF16),
                        pltpu.SemaphoreType.DMA((2, 3))],
    )
    return pl.pallas_call(
        _experts_kernel,
        grid_spec=grid_spec,
        out_shape=jax.ShapeDtypeStruct((n_rows,) + tile, xs.dtype),
        compiler_params=_cparams(("arbitrary",)),
        name="moe_experts",
    )(block_expert, n_used, first, nxt, slot, xs, w_g, w_u, w_d)


def _combine_kernel(pos_ref, x1_ref, rt_ref, nw_ref, y_hbm, o_ref, ybuf, sem, *, normalize):
    i = pl.program_id(0)
    n = pl.num_programs(0)
    tm = x1_ref.shape[0]

    def start_tile(slot, which):
        for r in range(tm):
            for k in range(TOP_K_INNER):
                pltpu.make_async_copy(y_hbm.at[pos_ref[which, k, r]], ybuf.at[slot, k, r],
                                      sem.at[slot]).start()

    def wait_tile(slot):
        for k in range(TOP_K_INNER):
            pltpu.make_async_copy(y_hbm.at[pl.ds(0, tm)], ybuf.at[slot, k], sem.at[slot]).wait()

    slot = i % 2

    @pl.when(i == 0)
    def _():
        start_tile(0, 0)

    wait_tile(slot)
    start_tile(1 - slot, 1)
    acc = x1_ref[...]
    for k in range(TOP_K_INNER):
        y_k = _unpack_bf16_pairs(_load_token_tiles(ybuf.at[slot, k]))
        acc = acc + rt_ref[:, ROUTE_W_LANE + k:ROUTE_W_LANE + k + 1] * y_k
    o_ref[...] = _rms(acc, nw_ref[...]) if normalize else acc

    @pl.when(i == n - 1)
    def _():
        wait_tile(1 - slot)


def _combine(x1, y_rows, pos, route, norm_w, tm, normalize):
    t, d = x1.shape
    nt = t // tm
    pos_t = pos.reshape(nt, tm, TOP_K_INNER).transpose(0, 2, 1)
    pos_next = jnp.concatenate([pos_t[1:], pos_t[-1:]], axis=0)
    pos2 = jnp.stack([pos_t, pos_next], axis=1)
    return pl.pallas_call(
        functools.partial(_combine_kernel, normalize=normalize),
        grid=(nt,),
        in_specs=[pl.BlockSpec((None, 2, TOP_K_INNER, tm), lambda i: (i, 0, 0, 0), memory_space=pltpu.SMEM),
                  pl.BlockSpec((tm, d), lambda i: (i, 0)),
                  pl.BlockSpec((tm, LANES), lambda i: (i, 0)),
                  pl.BlockSpec((1, d), lambda i: (0, 0)),
                  pl.BlockSpec(memory_space=pl.ANY)],
        out_specs=pl.BlockSpec((tm, d), lambda i: (i, 0)),
        out_shape=jax.ShapeDtypeStruct((t, d), F32),
        scratch_shapes=[pltpu.VMEM((2, TOP_K_INNER, tm) + y_rows.shape[1:], y_rows.dtype),
                        pltpu.SemaphoreType.DMA((2,))],
        compiler_params=_cparams(("arbitrary",)),
        name="moe_combine",
    )(pos2, x1, route, norm_w.reshape(1, d), y_rows)


def _layer(x, p):
    t, d = x.shape
    inner = p["w_a_up"].shape[0]
    n_heads = p["a_log"].shape[0]
    s5_width = p["w_glu"].shape[0]
    xbc_dim = inner + 2 * SSD_GROUPS * SSD_STATE
    sizes = (inner, xbc_dim, n_heads, s5_width, 2 * d)
    offs = [0]
    for s in sizes:
        offs.append(offs[-1] + s)
    w_in = p["w_in"].astype(F32)
    assert offs[2] % LANES == 0 and n_heads <= LANES
    w_head = lax.optimization_barrier(w_in[:, :offs[2] + LANES])
    w_tail = lax.optimization_barrier(w_in[:, offs[3]:])
    dt_b = jnp.zeros((1, LANES), F32).at[0, :n_heads].set(p["dt_bias"].astype(F32))

    tm = min(ROW_TILE, t)
    tmm = min(MM_ROW_TILE, t)

    h = _rmsnorm(x, p["norm_mix_w"], tm)
    zs = _proj(h, w_head, offs[0], inner, None, "silu", tmm, MM_COL_TILE)
    xbc = _conv_proj(h, w_head, offs[1], xbc_dim, p["conv_w"], p["conv_b"], tmm, MM_COL_TILE)
    dt, dtt = _dt_proj(h, w_head, offs[2], dt_b, tm)
    u_steps = _chunk_proj(h, w_tail, 0, s5_width)
    gates = _proj(h, w_tail, s5_width, 2 * d, p["gate_b"], "sigmoid_bias", tmm, MM_COL_TILE)

    y = _ssd(xbc, dt, dtt[:n_heads], p["a_log"].astype(F32), p["d_ssd"].astype(F32), n_heads,
             min(SSD_CHUNK, t))
    ya = _gated_up(y, zs, p["norm_ssd_w"], p["w_a_up"].astype(BF16), tm)

    uc, wc, bbt, a_chunk = _s5_operators(p["s5_lambda_re"], p["s5_lambda_im"], p["s5_log_dt"],
                                         p["s5_b_re"], p["s5_b_im"], p["s5_c_re"], p["s5_c_im"])
    y5 = _s5(u_steps, uc, wc, bbt, a_chunk, p["s5_d"].astype(F32), min(S5_ROWS, t // S5_CHUNK))
    yb = _glu_up(y5, p["w_glu"].astype(BF16), p["w_b_up"].astype(BF16))

    w_router = jnp.zeros((d, LANES), F32)
    w_router = w_router.at[:, :N_EXPERT_GROUPS].set(p["w_route_group"].astype(F32))
    w_router = w_router.at[:, ROUTE_EXPERT_LANE0:ROUTE_EXPERT_LANE0 + N_EXPERTS].set(
        p["w_route_expert"].astype(F32))
    r_hi = w_router.astype(BF16)
    r_lo = (w_router - r_hi.astype(F32)).astype(BF16)
    r_b = jnp.zeros((1, LANES), F32)
    r_b = r_b.at[0, :N_EXPERT_GROUPS].set(p["b_route_group"].astype(F32))
    r_b = r_b.at[0, ROUTE_EXPERT_LANE0:ROUTE_EXPERT_LANE0 + N_EXPERTS].set(p["b_route_expert"].astype(F32))
    x1, h2, route = _merge_out(gates, ya, yb, x, p["w_out"].astype(BF16), p["norm_ffn_w"],
                               r_hi, r_lo, r_b, tm)

    expert_ids = route[:, ROUTE_ID_LANE:ROUTE_ID_LANE + TOP_K_INNER].astype(jnp.int32)
    bm = MOE_BLOCK
    tg = min(GATHER_TILE, t)
    pos, block_expert, n_used, pad_start, pad_count, n_rows = _dispatch_plan(expert_ids, bm)
    xs = _dispatch(h2, pos, pad_start, pad_count, n_used, n_rows, tg, bm)
    y_rows = _experts(xs, block_expert, n_used, p["w_exp_gate"], p["w_exp_up"], p["w_exp_down"], bm)
    return x1, y_rows, pos, route


def kernel(x, norm_mix_w, w_in, conv_w, conv_b, dt_bias, a_log, d_ssd, norm_ssd_w, w_a_up,
           s5_lambda_re, s5_lambda_im, s5_log_dt, s5_b_re, s5_b_im, s5_c_re, s5_c_im, s5_d,
           w_glu, w_b_up, gate_b, w_out, norm_ffn_w, w_route_group, b_route_group,
           w_route_expert, b_route_expert, w_exp_gate, w_exp_up, w_exp_down, norm_final_w):
    b, seq, d = x.shape
    assert b == 1, "the scans carry state along the flattened token axis"
    depth = w_in.shape[0]
    per_layer = dict(norm_mix_w=norm_mix_w, w_in=w_in, conv_w=conv_w, conv_b=conv_b, dt_bias=dt_bias,
                     a_log=a_log, d_ssd=d_ssd, norm_ssd_w=norm_ssd_w, w_a_up=w_a_up,
                     s5_lambda_re=s5_lambda_re, s5_lambda_im=s5_lambda_im, s5_log_dt=s5_log_dt,
                     s5_b_re=s5_b_re, s5_b_im=s5_b_im, s5_c_re=s5_c_re, s5_c_im=s5_c_im, s5_d=s5_d,
                     w_glu=w_glu, w_b_up=w_b_up, gate_b=gate_b, w_out=w_out, norm_ffn_w=norm_ffn_w,
                     w_route_group=w_route_group, b_route_group=b_route_group,
                     w_route_expert=w_route_expert, b_route_expert=b_route_expert,
                     w_exp_gate=w_exp_gate, w_exp_up=w_exp_up, w_exp_down=w_exp_down)
    xt = x.reshape(b * seq, d)
    tg = min(GATHER_TILE, b * seq)
    for i in range(depth):
        p = {k: v[i] for k, v in per_layer.items()}
        x1, y_rows, pos, route = _layer(xt, p)
        xt = _combine(x1, y_rows, pos, route, norm_final_w, tg, normalize=(i == depth - 1))
    return xt.reshape(b, seq, d)
```

```python
import functools
import math

import jax
import jax.numpy as jnp
from jax import lax
from jax.experimental import pallas as pl
from jax.experimental.pallas import tpu as pltpu

F32 = jnp.float32
BF16 = jnp.bfloat16

SSD_HEAD_DIM = 64
SSD_GROUPS = 8
SSD_STATE = 128
CONV_WIDTH = 4
S5_GROUP_CH = 16
S5_STATE = 64
N_EXPERT_GROUPS = 4
EXPERTS_PER_GROUP = 8
N_EXPERTS = N_EXPERT_GROUPS * EXPERTS_PER_GROUP
TOP_K_INNER = 2
RMS_EPS = 1e-6

LANES = 128
SUBLANES = 8
VMEM_LIMIT_BYTES = 52 * 1024 * 1024

ROW_TILE = 512
MM_ROW_TILE = 1024
MM_COL_TILE = 1024
SSD_CHUNK = 128
S5_CHUNK = 16
S5_SUPER = S5_GROUP_CH * 8
S5_ROWS = 256
MOE_BLOCK = 256
GATHER_TILE = 256


def _cparams(sem, vmem=VMEM_LIMIT_BYTES):
    return pltpu.CompilerParams(dimension_semantics=sem, vmem_limit_bytes=vmem)


def _sigmoid(v):
    return 1.0 / (1.0 + jnp.exp(-v))


def _silu(v):
    return v * _sigmoid(v)


def _softplus(v):
    return jnp.maximum(v, 0.0) + jnp.log(1.0 + jnp.exp(-jnp.abs(v)))


def _gelu_tanh(v):
    c = math.sqrt(2.0 / math.pi)
    return 0.5 * v * (1.0 + jnp.tanh(c * (v + 0.044715 * (v * v * v))))


def _rms(v, w):
    ms = jnp.mean(v * v, axis=-1, keepdims=True)
    return v * lax.rsqrt(ms + RMS_EPS) * w


def _pack_bf16_pairs(v):
    n = v.shape[1] // 2
    lo = pltpu.bitcast(v[:, :n].astype(BF16).astype(F32), jnp.uint32)
    hi = pltpu.bitcast(v[:, n:].astype(BF16).astype(F32), jnp.uint32)
    return hi | (lo >> 16)


def _unpack_bf16_pairs(w):
    lo = pltpu.bitcast(w << 16, F32)
    hi = pltpu.bitcast(w & jnp.uint32(0xFFFF0000), F32)
    return jnp.concatenate([lo, hi], axis=1)


def _store_token_tiles(ref, rows):
    for a in range(ref.shape[-2]):
        ref[:, a, :] = rows[:, a * LANES:(a + 1) * LANES]


def _load_token_tiles(ref):
    return jnp.concatenate([ref[:, a, :] for a in range(ref.shape[-2])], axis=1)


def _rmsnorm_kernel(x_ref, w_ref, o_ref):
    o_ref[...] = _rms(x_ref[...], w_ref[...]).astype(o_ref.dtype)


def _rmsnorm(x, w, tm):
    t, d = x.shape
    return pl.pallas_call(
        _rmsnorm_kernel,
        grid=(t // tm,),
        in_specs=[pl.BlockSpec((tm, d), lambda i: (i, 0)),
                  pl.BlockSpec((1, d), lambda i: (0, 0))],
        out_specs=pl.BlockSpec((tm, d), lambda i: (i, 0)),
        out_shape=jax.ShapeDtypeStruct((t, d), BF16),
        compiler_params=_cparams(("parallel",)),
        name="rmsnorm",
    )(x, w.reshape(1, d))


def _resident_weight(w_ref, wbf_ref, row_axis=1):
    @pl.when(pl.program_id(row_axis) == 0)
    def _():
        wbf_ref[...] = w_ref[...].astype(BF16)
    return wbf_ref


def _proj_kernel(a_ref, w_ref, b_ref, o_ref, *scratch, act):
    w = _resident_weight(w_ref, scratch[0]) if scratch else w_ref
    p = jnp.dot(a_ref[...], w[...], preferred_element_type=F32)
    if act == "silu":
        p = _silu(p)
    elif act == "sigmoid_bias":
        p = _sigmoid(p + b_ref[...])
    o_ref[...] = p.astype(o_ref.dtype)


def _proj(h, w, layer, col0, n, b, act, tm, tn):
    t, k = h.shape
    tn = min(tn, n)
    assert col0 % tn == 0 and n % tn == 0
    jb = col0 // tn
    if b is None:
        b = jnp.zeros((1, n), F32)
    scratch = [pltpu.VMEM((k, tn), BF16)] if w.dtype != BF16 else []
    return pl.pallas_call(
        functools.partial(_proj_kernel, act=act),
        grid=(n // tn, t // tm),
        in_specs=[pl.BlockSpec((tm, k), lambda j, i: (i, 0)),
                  pl.BlockSpec((None, k, tn), lambda j, i: (layer, 0, jb + j)),
                  pl.BlockSpec((1, tn), lambda j, i: (0, j))],
        out_specs=pl.BlockSpec((tm, tn), lambda j, i: (i, j)),
        out_shape=jax.ShapeDtypeStruct((t, n), BF16),
        scratch_shapes=scratch,
        compiler_params=_cparams(("parallel", "arbitrary")),
        name="proj_" + act,
    )(h, w, b.reshape(1, n))


def _conv_proj_kernel(a_ref, w_ref, cw_ref, cb_ref, o_ref, ext_ref, wbf_ref):
    tm = a_ref.shape[0]
    halo = SUBLANES

    @pl.when(pl.program_id(1) == 0)
    def _():
        ext_ref[pl.ds(0, halo), :] = jnp.zeros((halo, ext_ref.shape[1]), F32)

    w = _resident_weight(w_ref, wbf_ref)
    p = jnp.dot(a_ref[...], w[...], preferred_element_type=F32)
    ext_ref[pl.ds(halo, tm), :] = p
    acc = cb_ref[...] + cw_ref[CONV_WIDTH - 1:CONV_WIDTH, :] * p
    for k in range(CONV_WIDTH - 1):
        back = CONV_WIDTH - 1 - k
        acc = acc + cw_ref[k:k + 1, :] * ext_ref[pl.ds(halo - back, tm), :]
    o_ref[...] = _silu(acc).astype(o_ref.dtype)
    ext_ref[pl.ds(0, halo), :] = p[tm - halo:, :]


def _conv_proj(h, w, layer, col0, n, conv_w, conv_b, tm, tn):
    t, k = h.shape
    assert col0 % tn == 0 and n % tn == 0
    jb = col0 // tn
    return pl.pallas_call(
        _conv_proj_kernel,
        grid=(n // tn, t // tm),
        in_specs=[pl.BlockSpec((tm, k), lambda j, i: (i, 0)),
                  pl.BlockSpec((None, k, tn), lambda j, i: (layer, 0, jb + j)),
                  pl.BlockSpec((CONV_WIDTH, tn), lambda j, i: (0, j)),
                  pl.BlockSpec((1, tn), lambda j, i: (0, j))],
        out_specs=pl.BlockSpec((tm, tn), lambda j, i: (i, j)),
        out_shape=jax.ShapeDtypeStruct((t, n), BF16),
        scratch_shapes=[pltpu.VMEM((tm + SUBLANES, tn), F32),
                        pltpu.VMEM((k, tn), BF16)],
        compiler_params=_cparams(("parallel", "arbitrary")),
        name="proj_conv",
    )(h, w, conv_w, conv_b.reshape(1, n))


def _dt_proj_kernel(a_ref, w_ref, b_ref, dt_ref, dtt_ref):
    p = jnp.dot(a_ref[...], w_ref[...].astype(BF16), preferred_element_type=F32) + b_ref[...]
    dt = _softplus(p)
    dt_ref[...] = dt
    dtt_ref[...] = dt.T


def _dt_proj(h, w, layer, col0, b_pad, tm):
    t, k = h.shape
    jb = col0 // LANES
    return pl.pallas_call(
        _dt_proj_kernel,
        grid=(t // tm,),
        in_specs=[pl.BlockSpec((tm, k), lambda i: (i, 0)),
                  pl.BlockSpec((None, k, LANES), lambda i: (layer, 0, jb)),
                  pl.BlockSpec((1, LANES), lambda i: (0, 0))],
        out_specs=[pl.BlockSpec((tm, LANES), lambda i: (i, 0)),
                   pl.BlockSpec((LANES, tm), lambda i: (0, i))],
        out_shape=[jax.ShapeDtypeStruct((t, LANES), F32),
                   jax.ShapeDtypeStruct((LANES, t), F32)],
        compiler_params=_cparams(("parallel",)),
        name="proj_dt",
    )(h, w, b_pad)


def _step_major_perm(n_rows):
    assert n_rows == S5_CHUNK * S5_CHUNK
    shift = S5_CHUNK.bit_length() - 1
    row = lax.broadcasted_iota(jnp.int32, (n_rows, n_rows), 0)
    col = lax.broadcasted_iota(jnp.int32, (n_rows, n_rows), 1)
    swapped = ((row & (S5_CHUNK - 1)) << shift) | (row >> shift)
    return jnp.where(col == swapped, 1.0, 0.0).astype(BF16)


def _chunk_proj_kernel(a_ref, w_ref, o_ref, wbf_ref):
    w = _resident_weight(w_ref, wbf_ref, row_axis=0)
    u = jnp.dot(a_ref[...], w[...], preferred_element_type=F32).astype(BF16)
    u_steps = jnp.dot(_step_major_perm(u.shape[0]), u, preferred_element_type=F32).astype(o_ref.dtype)
    nj = o_ref.shape[1]
    for s in range(o_ref.shape[0]):
        o_ref[s] = u_steps[s * nj:(s + 1) * nj, :]


def _chunk_proj(h, w, col0, n):
    t, k = h.shape
    nj = t // S5_CHUNK
    tm = S5_CHUNK * S5_CHUNK
    assert col0 % n == 0
    jb = col0 // n
    return pl.pallas_call(
        _chunk_proj_kernel,
        grid=(t // tm,),
        in_specs=[pl.BlockSpec((tm, k), lambda i: (i, 0)),
                  pl.BlockSpec((k, n), lambda i: (0, jb))],
        out_specs=pl.BlockSpec((S5_CHUNK, S5_CHUNK, n), lambda i: (0, i, 0)),
        out_shape=jax.ShapeDtypeStruct((S5_CHUNK, nj, n), BF16),
        scratch_shapes=[pltpu.VMEM((k, n), BF16)],
        compiler_params=_cparams(("arbitrary",)),
        name="proj_u",
    )(h, w)


def _cumsum_rows(v):
    n = v.shape[0]
    idx = lax.broadcasted_iota(jnp.int32, v.shape, 0)
    k = 1
    while k < n:
        v = v + jnp.where(idx >= k, pltpu.roll(v, k, 0), 0.0)
        k *= 2
    return v


def _cumsum_lanes(v):
    n = v.shape[1]
    idx = lax.broadcasted_iota(jnp.int32, v.shape, 1)
    k = 1
    while k < n:
        v = v + jnp.where(idx >= k, pltpu.roll(v, k, 1), 0.0)
        k *= 2
    return v


def _ssd_kernel(xbc_ref, dt_ref, dtt_ref, alog_r_ref, alog_c_ref, dskip_ref, o_ref, state_ref, *,
                n_heads):
    q = dt_ref.shape[0]
    n = SSD_STATE
    p_dim = SSD_HEAD_DIM
    r_heads = n_heads // SSD_GROUPS
    gw = r_heads * p_dim
    inner = n_heads * p_dim

    @pl.when(pl.program_id(0) == 0)
    def _():
        state_ref[...] = jnp.zeros(state_ref.shape, F32)

    dtt = dtt_ref[...]
    cs_col = _cumsum_rows(dt_ref[...] * -jnp.exp(alog_r_ref[...]))
    cs_row = _cumsum_lanes(dtt * -jnp.exp(alog_c_ref[...]))
    causal = lax.broadcasted_iota(jnp.int32, (q, q), 0) >= lax.broadcasted_iota(jnp.int32, (q, q), 1)
    head_of_lane = lax.broadcasted_iota(jnp.int32, (1, gw), 1) // p_dim
    eye = jnp.where(lax.broadcasted_iota(jnp.int32, (n, n), 0) == lax.broadcasted_iota(jnp.int32, (n, n), 1),
                    1.0, 0.0).astype(BF16)
    nt = (((1,), (1,)), ((), ()))

    for g in range(SSD_GROUPS):
        x_g = xbc_ref[:, g * gw:(g + 1) * gw]
        b_g = xbc_ref[:, inner + g * n:inner + (g + 1) * n]
        c_g = xbc_ref[:, inner + (SSD_GROUPS + g) * n:inner + (SSD_GROUPS + g + 1) * n]
        cb = lax.dot_general(c_g, b_g, nt, preferred_element_type=F32)
        b_t = lax.dot_general(eye, b_g, nt, preferred_element_type=F32)
        c_f = c_g.astype(F32)
        s_g = state_ref[g]
        lhs_parts, bt_parts, cd = [], [], jnp.zeros((1, gw), F32)
        for r in range(r_heads):
            h = g * r_heads + r
            csb = jnp.broadcast_to(cs_col[:, h:h + 1], (q, n))
            csr = cs_row[h:h + 1, :]
            dtr = dtt[h:h + 1, :]
            cs_last = csr[:, q - 1:q]
            decay = jnp.exp(jnp.where(causal, csb - csr, -1e30))
            lhs_parts.append(jnp.concatenate([cb * decay * dtr, c_f * jnp.exp(csb)], axis=1).astype(BF16))
            bt_parts.append((b_t * (dtr * jnp.exp(cs_last - csr))).astype(BF16))
            cd = jnp.where(head_of_lane == r, jnp.exp(cs_last), cd)
        rhs = jnp.concatenate([x_g, s_g.astype(BF16)], axis=0)
        per_half = LANES // p_dim
        y_halves, s_halves = [], []
        for half in range(gw // LANES):
            heads = range(half * per_half, (half + 1) * per_half)
            cols = slice(half * LANES, (half + 1) * LANES)
            y_all = jnp.dot(jnp.concatenate([lhs_parts[r] for r in heads], axis=0), rhs[:, cols],
                            preferred_element_type=F32)
            s_all = jnp.dot(jnp.concatenate([bt_parts[r] for r in heads], axis=0), x_g[:, cols],
                            preferred_element_type=F32)
            lane_head = head_of_lane[:, cols]
            y_h, s_h = y_all[:q, :], s_all[:n, :]
            for j, r in enumerate(heads):
                if j:
                    y_h = jnp.where(lane_head == r, y_all[j * q:(j + 1) * q, :], y_h)
                    s_h = jnp.where(lane_head == r, s_all[j * n:(j + 1) * n, :], s_h)
            y_halves.append(y_h)
            s_halves.append(s_h)
        state_ref[g] = s_g * cd + jnp.concatenate(s_halves, axis=1)
        y_g = jnp.concatenate(y_halves, axis=1) + x_g.astype(F32) * dskip_ref[:, g * gw:(g + 1) * gw]
        o_ref[:, g * gw:(g + 1) * gw] = y_g.astype(o_ref.dtype)


def _ssd(xbc, dt, dtt, a_log, d_skip, n_heads, q):
    t, width = xbc.shape
    inner = n_heads * SSD_HEAD_DIM
    gw = inner // SSD_GROUPS
    assert n_heads % SUBLANES == 0 and n_heads <= LANES
    alog_r = jnp.zeros((1, LANES), F32).at[0, :n_heads].set(a_log)
    alog_c = a_log.reshape(n_heads, 1)
    dskip = jnp.repeat(d_skip, SSD_HEAD_DIM).reshape(1, inner)
    return pl.pallas_call(
        functools.partial(_ssd_kernel, n_heads=n_heads),
        grid=(t // q,),
        in_specs=[pl.BlockSpec((q, width), lambda c: (c, 0)),
                  pl.BlockSpec((q, LANES), lambda c: (c, 0)),
                  pl.BlockSpec((n_heads, q), lambda c: (0, c)),
                  pl.BlockSpec((1, LANES), lambda c: (0, 0)),
                  pl.BlockSpec((n_heads, 1), lambda c: (0, 0)),
                  pl.BlockSpec((1, inner), lambda c: (0, 0))],
        out_specs=pl.BlockSpec((q, inner), lambda c: (c, 0)),
        out_shape=jax.ShapeDtypeStruct((t, inner), BF16),
        scratch_shapes=[pltpu.VMEM((SSD_GROUPS, SSD_STATE, gw), F32)],
        compiler_params=_cparams(("arbitrary",)),
        name="ssd_scan",
    )(xbc, dt, dtt, alog_r, alog_c, dskip)


def _gated_up_kernel(y_ref, z_ref, nw_ref, w_ref, o_ref):
    v = y_ref[...].astype(F32) * z_ref[...].astype(F32)
    na = _rms(v, nw_ref[...]).astype(BF16)
    o_ref[...] = jnp.dot(na, w_ref[...], preferred_element_type=F32).astype(o_ref.dtype)


def _gated_up(y, zs, norm_w, w, tm):
    t, d = y.shape
    n = w.shape[1]
    return pl.pallas_call(
        _gated_up_kernel,
        grid=(t // tm,),
        in_specs=[pl.BlockSpec((tm, d), lambda i: (i, 0)),
                  pl.BlockSpec((tm, d), lambda i: (i, 0)),
                  pl.BlockSpec((1, d), lambda i: (0, 0)),
                  pl.BlockSpec((d, n), lambda i: (0, 0))],
        out_specs=pl.BlockSpec((tm, n), lambda i: (i, 0)),
        out_shape=jax.ShapeDtypeStruct((t, n), BF16),
        compiler_params=_cparams(("parallel",)),
        name="ssd_gated_up",
    )(y, zs, norm_w.reshape(1, d), w)


def _s5_operators(lam_re, lam_im, log_dt, b_re, b_im, c_re, c_im):
    ng, ns = lam_re.shape
    nc = S5_GROUP_CH
    L = S5_CHUNK
    per = S5_SUPER // nc
    nsg = ng // per
    lr, li = lam_re.astype(F32), lam_im.astype(F32)
    dt = jnp.exp(log_dt.astype(F32))[:, None]
    mag = jnp.exp(lr * dt)
    ang = li * dt
    abar_r, abar_i = mag * jnp.cos(ang), mag * jnp.sin(ang)
    den = lr * lr + li * li
    nr, ni = abar_r - 1.0, abar_i
    coef_r = (nr * lr + ni * li) / den
    coef_i = (ni * lr - nr * li) / den
    bre, bim = b_re.astype(F32), b_im.astype(F32)
    bb_r = coef_r[..., None] * bre - coef_i[..., None] * bim
    bb_i = coef_r[..., None] * bim + coef_i[..., None] * bre
    cre, cim = c_re.astype(F32), c_im.astype(F32)
    ks = jnp.arange(L + 1, dtype=F32)[:, None, None]
    pmag = jnp.exp(ks * (lr * dt)[None])
    pang = ks * ang[None]
    pw_r, pw_i = pmag * jnp.cos(pang), pmag * jnp.sin(pang)
    ca_r = cre[None] * pw_r[:, :, None, :] - cim[None] * pw_i[:, :, None, :]
    ca_i = cre[None] * pw_i[:, :, None, :] + cim[None] * pw_r[:, :, None, :]
    def _rows_n(v, steps):
        return v.reshape(steps, nsg, per, nc, ns).transpose(1, 4, 0, 2, 3).reshape(nsg, ns, steps * per * nc)

    uc = jnp.concatenate([_rows_n(ca_r, L + 1), -_rows_n(ca_i, L + 1)], axis=1)
    ks_rev = (L - 1) - jnp.arange(L, dtype=F32)[:, None, None]
    rmag = jnp.exp(ks_rev * (lr * dt)[None])
    rang = ks_rev * ang[None]
    rev_r, rev_i = rmag * jnp.cos(rang), rmag * jnp.sin(rang)
    ab_r = rev_r[..., None] * bb_r[None] - rev_i[..., None] * bb_i[None]
    ab_i = rev_r[..., None] * bb_i[None] + rev_i[..., None] * bb_r[None]
    ab_rt, ab_it = ab_r.transpose(0, 1, 3, 2), ab_i.transpose(0, 1, 3, 2)
    wc = jnp.concatenate([_rows_n(ab_rt, L), _rows_n(ab_it, L)], axis=1)

    def _rows_gc(v):
        return v.reshape(nsg, per, ns, nc).transpose(0, 1, 3, 2).reshape(nsg, per * nc, ns)

    bbt = jnp.concatenate([_rows_gc(bb_r), _rows_gc(bb_i)], axis=2)
    a_chunk = jnp.concatenate([pw_r[L].reshape(nsg, 1, per * ns),
                               pw_i[L].reshape(nsg, 1, per * ns)], axis=2)
    return uc, wc, bbt, a_chunk


def _split_bf16(v):
    hi = v.astype(BF16)
    return hi, (v - hi.astype(F32)).astype(BF16)


def _s5_kernel(u_ref, uc_ref, wc_ref, bbt_ref, ach_ref, dsk_ref, o_ref,
               toep_ref, wt_ref, v_ref, x_ref, sp_ref, carry_ref):
    jb = pl.program_id(1)
    nl, nj, cw = u_ref.shape
    half = carry_ref.shape[1] // 2
    per = cw // S5_GROUP_CH
    ns = half // per
    nt = (((1,), (1,)), ((), ()))

    @pl.when(jb == 0)
    def _():
        carry_ref[...] = jnp.zeros(carry_ref.shape, F32)
        uc = uc_ref[...]
        wc = wc_ref[...]
        b_hi, b_lo = _split_bf16(bbt_ref[...])
        u_hi, u_lo = _split_bf16(uc[:, :nl * cw])
        kall = (jnp.dot(b_hi, u_hi, preferred_element_type=F32)
                + jnp.dot(b_lo, u_hi, preferred_element_type=F32)
                + jnp.dot(b_hi, u_lo, preferred_element_type=F32))
        row_g = lax.broadcasted_iota(jnp.int32, (cw, 1), 0) // S5_GROUP_CH
        col_g = (lax.broadcasted_iota(jnp.int32, (1, nl * cw), 1) // S5_GROUP_CH) % per
        kall = jnp.where(row_g == col_g, kall, 0.0).astype(BF16)
        toep_ref[...] = jnp.zeros(toep_ref.shape, toep_ref.dtype)
        for s_in in range(nl):
            for s_out in range(s_in, nl):
                k = s_out - s_in
                toep_ref[pl.ds(s_in * cw, cw), pl.ds(s_out * cw, cw)] = kall[:, k * cw:(k + 1) * cw]
        for gp in range(per):
            mine = col_g == gp
            for part in range(2):
                rows = pl.ds(part * half + gp * ns, ns)
                src = slice(part * ns, (part + 1) * ns)
                v_ref[rows, :] = jnp.where(mine, uc[src, cw:], 0.0).astype(BF16)
                wt_ref[rows, :] = jnp.where(mine, wc[src, :], 0.0).astype(BF16)

    lhs = jnp.concatenate([u_ref[s] for s in range(nl)], axis=1)
    x_ref[...] = lax.dot_general(lhs, wt_ref[...], nt, preferred_element_type=F32)

    a_re = ach_ref[:, :half]
    a_im = ach_ref[:, half:]

    def step(j, carry):
        s_re, s_im = carry
        sp_ref[pl.ds(j, 1), :half] = s_re
        sp_ref[pl.ds(j, 1), half:] = s_im
        xr = x_ref[pl.ds(j, 1), :half]
        xi = x_ref[pl.ds(j, 1), half:]
        return (a_re * s_re - a_im * s_im + xr, a_re * s_im + a_im * s_re + xi)

    s_re, s_im = lax.fori_loop(0, nj, step, (carry_ref[:, :half], carry_ref[:, half:]), unroll=8)
    carry_ref[:, :half] = s_re
    carry_ref[:, half:] = s_im

    y_state = jnp.dot(sp_ref[...].astype(BF16), v_ref[...], preferred_element_type=F32)
    dsk = dsk_ref[...]
    pair = 2 * cw
    for tp in range(nl // 2):
        kdim = pair * (tp + 1)
        y = jnp.dot(lhs[:, :kdim], toep_ref[pl.ds(0, kdim), pl.ds(tp * pair, pair)],
                    preferred_element_type=F32)
        y = y + y_state[:, tp * pair:(tp + 1) * pair]
        for h in range(2):
            s = 2 * tp + h
            o_ref[s] = (y[:, h * cw:(h + 1) * cw] + dsk * u_ref[s].astype(F32)).astype(o_ref.dtype)


def _s5(u_steps, uc, wc, bbt, a_chunk, d_skip, tj):
    nl, nj, width = u_steps.shape
    nsg = width // S5_SUPER
    nstate = a_chunk.shape[2]
    rows = uc.shape[1]
    return pl.pallas_call(
        _s5_kernel,
        grid=(nsg, nj // tj),
        in_specs=[pl.BlockSpec((nl, tj, S5_SUPER), lambda g, j: (0, j, g)),
                  pl.BlockSpec((None, rows, (nl + 1) * S5_SUPER), lambda g, j: (g, 0, 0)),
                  pl.BlockSpec((None, rows, nl * S5_SUPER), lambda g, j: (g, 0, 0)),
                  pl.BlockSpec((None, S5_SUPER, rows), lambda g, j: (g, 0, 0)),
                  pl.BlockSpec((None, 1, nstate), lambda g, j: (g, 0, 0)),
                  pl.BlockSpec((None, 1, S5_SUPER), lambda g, j: (g, 0, 0))],
        out_specs=pl.BlockSpec((nl, tj, S5_SUPER), lambda g, j: (0, j, g)),
        out_shape=jax.ShapeDtypeStruct((nl, nj, width), BF16),
        scratch_shapes=[pltpu.VMEM((nl * S5_SUPER, nl * S5_SUPER), BF16),
                        pltpu.VMEM((nstate, nl * S5_SUPER), BF16),
                        pltpu.VMEM((nstate, nl * S5_SUPER), BF16),
                        pltpu.VMEM((tj, nstate), F32),
                        pltpu.VMEM((tj, nstate), F32),
                        pltpu.VMEM((1, nstate), F32)],
        compiler_params=_cparams(("parallel", "arbitrary")),
        name="s5_scan",
    )(u_steps, uc, wc, bbt, a_chunk, d_skip.reshape(nsg, 1, S5_SUPER))


def _glu_up_kernel(y_ref, wg_ref, wu_ref, o_ref):
    y_steps = jnp.concatenate([y_ref[s] for s in range(y_ref.shape[0])], axis=0)
    y = jnp.dot(_step_major_perm(y_steps.shape[0]), y_steps, preferred_element_type=F32)
    v = _gelu_tanh(y)
    gate = _sigmoid(jnp.dot(v.astype(BF16), wg_ref[...], preferred_element_type=F32))
    o_ref[...] = jnp.dot((v * gate).astype(BF16), wu_ref[...],
                         preferred_element_type=F32).astype(o_ref.dtype)


def _glu_up(y_steps, w_glu, w_up):
    nl, nj, width = y_steps.shape
    n = w_up.shape[1]
    tm = S5_CHUNK * S5_CHUNK
    return pl.pallas_call(
        _glu_up_kernel,
        grid=(nj // S5_CHUNK,),
        in_specs=[pl.BlockSpec((nl, S5_CHUNK, width), lambda i: (0, i, 0)),
                  pl.BlockSpec((width, width), lambda i: (0, 0)),
                  pl.BlockSpec((width, n), lambda i: (0, 0))],
        out_specs=pl.BlockSpec((tm, n), lambda i: (i, 0)),
        out_shape=jax.ShapeDtypeStruct((nj * nl, n), BF16),
        compiler_params=_cparams(("parallel",)),
        name="s5_glu_up",
    )(y_steps, w_glu, w_up)


def _merge_out_kernel(ga_ref, gb_ref, ya_ref, yb_ref, x_ref, w_ref, nw_ref, rw_ref, rb_ref,
                      x1_ref, h2_ref, rt_ref):
    merged = (ga_ref[...].astype(F32) * ya_ref[...].astype(F32)
              + gb_ref[...].astype(F32) * yb_ref[...].astype(F32))
    x1 = x_ref[...] + jnp.dot(merged.astype(BF16), w_ref[...], preferred_element_type=F32)
    x1_ref[...] = x1
    h2 = _rms(x1, nw_ref[...])
    h2_ref[...] = _pack_bf16_pairs(h2)
    lg = jnp.dot(h2.astype(BF16), rw_ref[...], preferred_element_type=F32)
    rt_ref[...] = _route_math(lg + rb_ref[...])


def _merge_out(gates, ya, yb, x, w_out, norm_w, r_w, r_b, tm):
    t, d = x.shape
    row = lambda i: (i, 0)
    full = lambda i: (0, 0)
    return pl.pallas_call(
        _merge_out_kernel,
        grid=(t // tm,),
        in_specs=[pl.BlockSpec((tm, d), row),
                  pl.BlockSpec((tm, d), lambda i: (i, 1)),
                  pl.BlockSpec((tm, d), row),
                  pl.BlockSpec((tm, d), row),
                  pl.BlockSpec((tm, d), row),
                  pl.BlockSpec((d, d), full),
                  pl.BlockSpec((1, d), full),
                  pl.BlockSpec((d, LANES), full),
                  pl.BlockSpec((1, LANES), full)],
        out_specs=[pl.BlockSpec((tm, d), row),
                   pl.BlockSpec((tm, d // 2), row),
                   pl.BlockSpec((tm, LANES), row)],
        out_shape=[jax.ShapeDtypeStruct((t, d), F32),
                   jax.ShapeDtypeStruct((t, d // 2), jnp.uint32),
                   jax.ShapeDtypeStruct((t, LANES), F32)],
        compiler_params=_cparams(("parallel",)),
        name="merge_out",
    )(gates, gates, ya, yb, x, w_out, norm_w.reshape(1, d), r_w, r_b)


ROUTE_ID_LANE = 0
ROUTE_W_LANE = TOP_K_INNER
ROUTE_EXPERT_LANE0 = SUBLANES


def _route_math(lg):
    lane = lax.broadcasted_iota(jnp.int32, lg.shape, 1)
    neg = -jnp.inf
    big = LANES
    is_g = lane < N_EXPERT_GROUPS
    gl = jnp.where(is_g, lg, neg)
    gmax = jnp.max(gl, axis=1, keepdims=True)
    grp = jnp.min(jnp.where(gl == gmax, lane, big), axis=1, keepdims=True)
    pg_sel = 1.0 / jnp.sum(jnp.where(is_g, jnp.exp(lg - gmax), 0.0), axis=1, keepdims=True)
    e_lo = ROUTE_EXPERT_LANE0 + grp * EXPERTS_PER_GROUP
    in_grp = (lane >= e_lo) & (lane < e_lo + EXPERTS_PER_GROUP)
    el = jnp.where(in_grp, lg, neg)
    v0 = jnp.max(el, axis=1, keepdims=True)
    i0 = jnp.min(jnp.where(el == v0, lane, big), axis=1, keepdims=True)
    el1 = jnp.where(lane == i0, neg, el)
    v1 = jnp.max(el1, axis=1, keepdims=True)
    i1 = jnp.min(jnp.where(el1 == v1, lane, big), axis=1, keepdims=True)
    e1w = jnp.exp(v1 - v0)
    w0 = pg_sel / (1.0 + e1w)
    w1 = pg_sel * e1w / (1.0 + e1w)
    return jnp.where(lane == ROUTE_ID_LANE, (i0 - ROUTE_EXPERT_LANE0).astype(F32),
           jnp.where(lane == ROUTE_ID_LANE + 1, (i1 - ROUTE_EXPERT_LANE0).astype(F32),
           jnp.where(lane == ROUTE_W_LANE, w0, jnp.where(lane == ROUTE_W_LANE + 1, w1, 0.0))))


def _dispatch_plan(expert_ids, bm):
    n_tokens = expert_ids.shape[0]
    n_assign = n_tokens * TOP_K_INNER
    eid = expert_ids.reshape(n_assign)
    experts = jnp.arange(N_EXPERTS, dtype=jnp.int32)
    onehot = (eid[:, None] == experts[None, :]).astype(jnp.int32)
    csum = jnp.cumsum(onehot, axis=0)
    counts = csum[-1]
    rank = jnp.sum(onehot * csum, axis=1) - 1
    padded = ((counts + bm - 1) // bm) * bm
    ends = jnp.cumsum(padded)
    starts = ends - padded
    dest = jnp.sum(onehot * starts[None, :], axis=1) + rank
    n_rows = n_assign + N_EXPERTS * bm
    nb = n_rows // bm
    block_start = jnp.arange(nb, dtype=jnp.int32) * bm
    block_expert = jnp.minimum(jnp.sum((ends[None, :] <= block_start[:, None]).astype(jnp.int32), axis=1),
                               N_EXPERTS - 1)
    n_used = (ends[-1] // bm).astype(jnp.int32).reshape(1)
    return (dest.reshape(n_tokens, TOP_K_INNER), block_expert, n_used,
            (starts + counts).astype(jnp.int32), (padded - counts).astype(jnp.int32), n_rows)


def _dispatch_kernel(pad_start_ref, pad_count_ref, nused_ref, dest_ref, h_ref, xs_hbm, stage, zrow, sem, zsem):
    i = pl.program_id(0)
    n_steps = pl.num_programs(0)
    tm = h_ref.shape[0]
    bm = zrow.shape[0]
    nb = xs_hbm.shape[0] // bm
    slot = i % 2

    def tile_wait(s):
        for _ in range(TOP_K_INNER):
            pltpu.make_async_copy(stage.at[s], xs_hbm.at[pl.ds(0, tm)], sem.at[s]).wait()

    @pl.when(i >= 2)
    def _():
        tile_wait(slot)

    _store_token_tiles(stage.at[slot], h_ref[...])

    for r in range(tm):
        for k in range(TOP_K_INNER):
            pltpu.make_async_copy(stage.at[slot, r], xs_hbm.at[dest_ref[k, r]], sem.at[slot]).start()

    @pl.when(i == n_steps - 1)
    def _():
        tile_wait(slot)

        @pl.when(n_steps >= 2)
        def _():
            tile_wait(1 - slot)

        zrow[...] = jnp.zeros(zrow.shape, zrow.dtype)

        def pad_copy(e, r):
            return pltpu.make_async_copy(zrow.at[0], xs_hbm.at[pad_start_ref[e] + r], zsem)

        def block_copy(b):
            return pltpu.make_async_copy(zrow, xs_hbm.at[pl.ds(b * bm, bm)], zsem)
        for e in range(N_EXPERTS):
            lax.fori_loop(0, pad_count_ref[e], lambda r, c, e=e: (pad_copy(e, r).start(), c)[1], 0)
        lax.fori_loop(nused_ref[0], nb, lambda b, c: (block_copy(b).start(), c)[1], 0)
        for e in range(N_EXPERTS):
            lax.fori_loop(0, pad_count_ref[e], lambda r, c, e=e: (pad_copy(e, r).wait(), c)[1], 0)
        lax.fori_loop(nused_ref[0], nb, lambda b, c: (block_copy(b).wait(), c)[1], 0)


def _dispatch(h2, dest, pad_start, pad_count, n_used, n_rows, tm, bm):
    t, width = h2.shape
    tile = (width // LANES, LANES)
    nt = t // tm
    dest_t = dest.reshape(nt, tm, TOP_K_INNER).transpose(0, 2, 1)
    grid_spec = pltpu.PrefetchScalarGridSpec(
        num_scalar_prefetch=3,
        grid=(nt,),
        in_specs=[pl.BlockSpec((None, TOP_K_INNER, tm), lambda i, *_: (i, 0, 0), memory_space=pltpu.SMEM),
                  pl.BlockSpec((tm, width), lambda i, *_: (i, 0))],
        out_specs=pl.BlockSpec(memory_space=pl.ANY),
        scratch_shapes=[pltpu.VMEM((2, tm) + tile, h2.dtype),
                        pltpu.VMEM((bm,) + tile, h2.dtype),
                        pltpu.SemaphoreType.DMA((2,)),
                        pltpu.SemaphoreType.DMA(())],
    )
    return pl.pallas_call(
        _dispatch_kernel,
        grid_spec=grid_spec,
        out_shape=jax.ShapeDtypeStruct((n_rows,) + tile, h2.dtype),
        compiler_params=_cparams(("arbitrary",)),
        name="moe_dispatch",
    )(pad_start, pad_count, n_used, dest_t, h2)


def _experts_kernel(bexp_ref, nused_ref, first_ref, next_ref, slot_ref, xs_ref, wg_hbm, wu_hbm, wd_hbm,
                    o_ref, wg_f, wu_f, wd_f, wg_s, wu_s, wd_s, sem):
    b = pl.program_id(0)
    n_used = nused_ref[0]

    def weight_copies(e, s):
        return (pltpu.make_async_copy(wg_hbm.at[e], wg_f.at[s], sem.at[s, 0]),
                pltpu.make_async_copy(wu_hbm.at[e], wu_f.at[s], sem.at[s, 1]),
                pltpu.make_async_copy(wd_hbm.at[e], wd_f.at[s], sem.at[s, 2]))

    @pl.when((b == 0) & (n_used > 0))
    def _():
        for c in weight_copies(bexp_ref[0], 0):
            c.start()

    @pl.when((b < n_used) & (first_ref[b] == 1))
    def _():
        s = slot_ref[b]
        for c in weight_copies(bexp_ref[b], s):
            c.wait()

        @pl.when(next_ref[b] >= 0)
        def _():
            for c in weight_copies(next_ref[b], 1 - s):
                c.start()

        wg_s[...] = wg_f[s].astype(BF16)
        wu_s[...] = wu_f[s].astype(BF16)
        wd_s[...] = wd_f[s].astype(BF16)

    @pl.when(b < n_used)
    def _():
        xb = _unpack_bf16_pairs(_load_token_tiles(xs_ref)).astype(BF16)
        hg = jnp.dot(xb, wg_s[...], preferred_element_type=F32)
        hu = jnp.dot(xb, wu_s[...], preferred_element_type=F32)
        act = (_silu(hg) * hu).astype(BF16)
        _store_token_tiles(o_ref, _pack_bf16_pairs(jnp.dot(act, wd_s[...], preferred_element_type=F32)))

    @pl.when(b >= n_used)
    def _():
        o_ref[...] = jnp.zeros(o_ref.shape, o_ref.dtype)


def _experts(xs, block_expert, n_used, w_g, w_u, w_d, bm):
    n_rows = xs.shape[0]
    tile = xs.shape[1:]
    d, ff = w_g.shape[1], w_g.shape[2]
    assert d == 2 * tile[0] * tile[1]
    nb = n_rows // bm
    blk = jnp.arange(nb, dtype=jnp.int32)
    valid = blk < n_used[0]
    first = (valid & ((blk == 0) | (block_expert != jnp.roll(block_expert, 1)))).astype(jnp.int32)
    later = valid[None, :] & (block_expert[None, :] > block_expert[:, None])
    nxt = jnp.min(jnp.where(later, block_expert[None, :], N_EXPERTS), axis=1)
    nxt = jnp.where(nxt == N_EXPERTS, -1, nxt).astype(jnp.int32)
    slot = ((jnp.cumsum(first) - 1) % 2).astype(jnp.int32)

    def used(b, nu):
        return jnp.minimum(b, jnp.maximum(nu[0] - 1, 0))

    grid_spec = pltpu.PrefetchScalarGridSpec(
        num_scalar_prefetch=5,
        grid=(nb,),
        in_specs=[pl.BlockSpec((bm,) + tile, lambda b, be, nu, *_: (used(b, nu), 0, 0)),
                  pl.BlockSpec(memory_space=pl.ANY),
                  pl.BlockSpec(memory_space=pl.ANY),
                  pl.BlockSpec(memory_space=pl.ANY)],
        out_specs=pl.BlockSpec((bm,) + tile, lambda b, *_: (b, 0, 0)),
        scratch_shapes=[pltpu.VMEM((2, d, ff), w_g.dtype),
                        pltpu.VMEM((2, d, ff), w_u.dtype),
                        pltpu.VMEM((2, ff, d), w_d.dtype),
                        pltpu.VMEM((d, ff), BF16),
                        pltpu.VMEM((d, ff), BF16),
                        pltpu.VMEM((ff, d), BF16),
                        pltpu.SemaphoreType.DMA((2, 3))],
    )
    return pl.pallas_call(
        _experts_kernel,
        grid_spec=grid_spec,
        out_shape=jax.ShapeDtypeStruct((n_rows,) + tile, xs.dtype),
        compiler_params=_cparams(("arbitrary",)),
        name="moe_experts",
    )(block_expert, n_used, first, nxt, slot, xs, w_g, w_u, w_d)


def _combine_kernel(pos_ref, x1_ref, rt_ref, nw_ref, y_hbm, o_ref, ybuf, sem, *, normalize):
    i = pl.program_id(0)
    n = pl.num_programs(0)
    tm = x1_ref.shape[0]

    def start_tile(slot, which):
        for r in range(tm):
            for k in range(TOP_K_INNER):
                pltpu.make_async_copy(y_hbm.at[pos_ref[which, k, r]], ybuf.at[slot, k, r],
                                      sem.at[slot]).start()

    def wait_tile(slot):
        for k in range(TOP_K_INNER):
            pltpu.make_async_copy(y_hbm.at[pl.ds(0, tm)], ybuf.at[slot, k], sem.at[slot]).wait()

    slot = i % 2

    @pl.when(i == 0)
    def _():
        start_tile(0, 0)

    wait_tile(slot)
    start_tile(1 - slot, 1)
    acc = x1_ref[...]
    for k in range(TOP_K_INNER):
        y_k = _unpack_bf16_pairs(_load_token_tiles(ybuf.at[slot, k]))
        acc = acc + rt_ref[:, ROUTE_W_LANE + k:ROUTE_W_LANE + k + 1] * y_k
    o_ref[...] = _rms(acc, nw_ref[...]) if normalize else acc

    @pl.when(i == n - 1)
    def _():
        wait_tile(1 - slot)


def _combine(x1, y_rows, pos, route, norm_w, tm, normalize):
    t, d = x1.shape
    nt = t // tm
    pos_t = pos.reshape(nt, tm, TOP_K_INNER).transpose(0, 2, 1)
    pos_next = jnp.concatenate([pos_t[1:], pos_t[-1:]], axis=0)
    pos2 = jnp.stack([pos_t, pos_next], axis=1)
    return pl.pallas_call(
        functools.partial(_combine_kernel, normalize=normalize),
        grid=(nt,),
        in_specs=[pl.BlockSpec((None, 2, TOP_K_INNER, tm), lambda i: (i, 0, 0, 0), memory_space=pltpu.SMEM),
                  pl.BlockSpec((tm, d), lambda i: (i, 0)),
                  pl.BlockSpec((tm, LANES), lambda i: (i, 0)),
                  pl.BlockSpec((1, d), lambda i: (0, 0)),
                  pl.BlockSpec(memory_space=pl.ANY)],
        out_specs=pl.BlockSpec((tm, d), lambda i: (i, 0)),
        out_shape=jax.ShapeDtypeStruct((t, d), F32),
        scratch_shapes=[pltpu.VMEM((2, TOP_K_INNER, tm) + y_rows.shape[1:], y_rows.dtype),
                        pltpu.SemaphoreType.DMA((2,))],
        compiler_params=_cparams(("arbitrary",)),
        name="moe_combine",
    )(pos2, x1, route, norm_w.reshape(1, d), y_rows)


def _layer(x, p, w_in_all, layer):
    t, d = x.shape
    inner = p["w_a_up"].shape[0]
    n_heads = p["a_log"].shape[0]
    s5_width = p["w_glu"].shape[0]
    xbc_dim = inner + 2 * SSD_GROUPS * SSD_STATE
    sizes = (inner, xbc_dim, n_heads, s5_width, 2 * d)
    offs = [0]
    for s in sizes:
        offs.append(offs[-1] + s)
    assert offs[2] % LANES == 0 and n_heads <= LANES
    w_tail = lax.optimization_barrier(w_in_all[layer:layer + 1, :, offs[3]:])
    dt_b = jnp.zeros((1, LANES), F32).at[0, :n_heads].set(p["dt_bias"].astype(F32))

    tm = min(ROW_TILE, t)
    tmm = min(MM_ROW_TILE, t)

    h = _rmsnorm(x, p["norm_mix_w"], tm)
    zs = _proj(h, w_in_all, layer, offs[0], inner, None, "silu", tmm, MM_COL_TILE)
    xbc = _conv_proj(h, w_in_all, layer, offs[1], xbc_dim, p["conv_w"], p["conv_b"], tmm, MM_COL_TILE)
    dt, dtt = _dt_proj(h, w_in_all, layer, offs[2], dt_b, tm)
    u_steps = _chunk_proj(h, w_tail[0], 0, s5_width)
    gates = _proj(h, w_tail, 0, s5_width, 2 * d, p["gate_b"], "sigmoid_bias", tmm, MM_COL_TILE)

    y = _ssd(xbc, dt, dtt[:n_heads], p["a_log"].astype(F32), p["d_ssd"].astype(F32), n_heads,
             min(SSD_CHUNK, t))
    ya = _gated_up(y, zs, p["norm_ssd_w"], p["w_a_up"].astype(BF16), tm)

    uc, wc, bbt, a_chunk = _s5_operators(p["s5_lambda_re"], p["s5_lambda_im"], p["s5_log_dt"],
                                         p["s5_b_re"], p["s5_b_im"], p["s5_c_re"], p["s5_c_im"])
    y5 = _s5(u_steps, uc, wc, bbt, a_chunk, p["s5_d"].astype(F32), min(S5_ROWS, t // S5_CHUNK))
    yb = _glu_up(y5, p["w_glu"].astype(BF16), p["w_b_up"].astype(BF16))

    w_router = jnp.zeros((d, LANES), F32)
    w_router = w_router.at[:, :N_EXPERT_GROUPS].set(p["w_route_group"].astype(F32))
    w_router = w_router.at[:, ROUTE_EXPERT_LANE0:ROUTE_EXPERT_LANE0 + N_EXPERTS].set(
        p["w_route_expert"].astype(F32))
    r_b = jnp.zeros((1, LANES), F32)
    r_b = r_b.at[0, :N_EXPERT_GROUPS].set(p["b_route_group"].astype(F32))
    r_b = r_b.at[0, ROUTE_EXPERT_LANE0:ROUTE_EXPERT_LANE0 + N_EXPERTS].set(p["b_route_expert"].astype(F32))
    x1, h2, route = _merge_out(gates, ya, yb, x, p["w_out"].astype(BF16), p["norm_ffn_w"],
                               w_router.astype(BF16), r_b, tm)

    expert_ids = route[:, ROUTE_ID_LANE:ROUTE_ID_LANE + TOP_K_INNER].astype(jnp.int32)
    bm = MOE_BLOCK
    tg = min(GATHER_TILE, t)
    pos, block_expert, n_used, pad_start, pad_count, n_rows = _dispatch_plan(expert_ids, bm)
    xs = _dispatch(h2, pos, pad_start, pad_count, n_used, n_rows, tg, bm)
    y_rows = _experts(xs, block_expert, n_used, p["w_exp_gate"], p["w_exp_up"], p["w_exp_down"], bm)
    return x1, y_rows, pos, route


def kernel(x, norm_mix_w, w_in, conv_w, conv_b, dt_bias, a_log, d_ssd, norm_ssd_w, w_a_up,
           s5_lambda_re, s5_lambda_im, s5_log_dt, s5_b_re, s5_b_im, s5_c_re, s5_c_im, s5_d,
           w_glu, w_b_up, gate_b, w_out, norm_ffn_w, w_route_group, b_route_group,
           w_route_expert, b_route_expert, w_exp_gate, w_exp_up, w_exp_down, norm_final_w):
    b, seq, d = x.shape
    assert b == 1, "the scans carry state along the flattened token axis"
    depth = w_in.shape[0]
    per_layer = dict(norm_mix_w=norm_mix_w, conv_w=conv_w, conv_b=conv_b, dt_bias=dt_bias,
                     a_log=a_log, d_ssd=d_ssd, norm_ssd_w=norm_ssd_w, w_a_up=w_a_up,
                     s5_lambda_re=s5_lambda_re, s5_lambda_im=s5_lambda_im, s5_log_dt=s5_log_dt,
                     s5_b_re=s5_b_re, s5_b_im=s5_b_im, s5_c_re=s5_c_re, s5_c_im=s5_c_im, s5_d=s5_d,
                     w_glu=w_glu, w_b_up=w_b_up, gate_b=gate_b, w_out=w_out, norm_ffn_w=norm_ffn_w,
                     w_route_group=w_route_group, b_route_group=b_route_group,
                     w_route_expert=w_route_expert, b_route_expert=b_route_expert,
                     w_exp_gate=w_exp_gate, w_exp_up=w_exp_up, w_exp_down=w_exp_down)
    xt = x.reshape(b * seq, d)
    tg = min(GATHER_TILE, b * seq)
    for i in range(depth):
        p = {k: v[i] for k, v in per_layer.items()}
        x1, y_rows, pos, route = _layer(xt, p, w_in.astype(F32), i)
        xt = _combine(x1, y_rows, pos, route, norm_final_w, tg, normalize=(i == depth - 1))
    return xt.reshape(b, seq, d)
```

```python
import functools
import math

import jax
import jax.numpy as jnp
from jax import lax
from jax.experimental import pallas as pl
from jax.experimental.pallas import tpu as pltpu

F32 = jnp.float32
BF16 = jnp.bfloat16

SSD_HEAD_DIM = 64
SSD_GROUPS = 8
SSD_STATE = 128
CONV_WIDTH = 4
S5_GROUP_CH = 16
S5_STATE = 64
N_EXPERT_GROUPS = 4
EXPERTS_PER_GROUP = 8
N_EXPERTS = N_EXPERT_GROUPS * EXPERTS_PER_GROUP
TOP_K_INNER = 2
RMS_EPS = 1e-6

LANES = 128
SUBLANES = 8
VMEM_LIMIT_BYTES = 52 * 1024 * 1024

ROW_TILE = 512
MM_ROW_TILE = 1024
MM_COL_TILE = 1024
SSD_CHUNK = 128
S5_CHUNK = 16
S5_SUPER = S5_GROUP_CH * 8
S5_ROWS = 256
MOE_BLOCK = 256
GATHER_TILE = 256


def _cparams(sem, vmem=VMEM_LIMIT_BYTES):
    return pltpu.CompilerParams(dimension_semantics=sem, vmem_limit_bytes=vmem)


def _sigmoid(v):
    return 1.0 / (1.0 + jnp.exp(-v))


def _silu(v):
    return v * _sigmoid(v)


def _softplus(v):
    return jnp.maximum(v, 0.0) + jnp.log(1.0 + jnp.exp(-jnp.abs(v)))


def _gelu_tanh(v):
    c = math.sqrt(2.0 / math.pi)
    return 0.5 * v * (1.0 + jnp.tanh(c * (v + 0.044715 * (v * v * v))))


def _rms(v, w):
    ms = jnp.mean(v * v, axis=-1, keepdims=True)
    return v * lax.rsqrt(ms + RMS_EPS) * w


def _pack_bf16_pairs(v):
    n = v.shape[1] // 2
    lo = pltpu.bitcast(v[:, :n].astype(BF16).astype(F32), jnp.uint32)
    hi = pltpu.bitcast(v[:, n:].astype(BF16).astype(F32), jnp.uint32)
    return hi | (lo >> 16)


def _unpack_bf16_pairs(w):
    lo = pltpu.bitcast(w << 16, F32)
    hi = pltpu.bitcast(w & jnp.uint32(0xFFFF0000), F32)
    return jnp.concatenate([lo, hi], axis=1)


def _store_token_tiles(ref, rows):
    for a in range(ref.shape[-2]):
        ref[:, a, :] = rows[:, a * LANES:(a + 1) * LANES]


def _load_token_tiles(ref):
    return jnp.concatenate([ref[:, a, :] for a in range(ref.shape[-2])], axis=1)


def _rmsnorm_kernel(x_ref, w_ref, o_ref):
    o_ref[...] = _rms(x_ref[...], w_ref[...]).astype(o_ref.dtype)


def _rmsnorm(x, w, tm):
    t, d = x.shape
    return pl.pallas_call(
        _rmsnorm_kernel,
        grid=(t // tm,),
        in_specs=[pl.BlockSpec((tm, d), lambda i: (i, 0)),
                  pl.BlockSpec((1, d), lambda i: (0, 0))],
        out_specs=pl.BlockSpec((tm, d), lambda i: (i, 0)),
        out_shape=jax.ShapeDtypeStruct((t, d), BF16),
        compiler_params=_cparams(("parallel",)),
        name="rmsnorm",
    )(x, w.reshape(1, d))


def _resident_weight(wt_ref, wbf_ref, row_axis=1):
    @pl.when(pl.program_id(row_axis) == 0)
    def _():
        wbf_ref[...] = wt_ref[...].T.astype(BF16)
    return wbf_ref


def _weight_rows_spec(layer, row0, n_rows, k, index_of_step):
    assert row0 % SUBLANES == 0 and n_rows % SUBLANES == 0
    return pl.BlockSpec((None, pl.Element(n_rows), pl.Element(k)),
                        lambda *idx: (layer, pl.multiple_of(row0 + n_rows * index_of_step(*idx), SUBLANES), 0))


def _proj_kernel(a_ref, wt_ref, b_ref, o_ref, wbf_ref, *, act):
    w = _resident_weight(wt_ref, wbf_ref)
    p = jnp.dot(a_ref[...], w[...], preferred_element_type=F32)
    if act == "silu":
        p = _silu(p)
    elif act == "sigmoid_bias":
        p = _sigmoid(p + b_ref[...])
    o_ref[...] = p.astype(o_ref.dtype)


def _proj(h, wt, layer, row0, n, b, act, tm, tn):
    t, k = h.shape
    tn = min(tn, n)
    assert n % tn == 0 and row0 % SUBLANES == 0
    if b is None:
        b = jnp.zeros((1, n), F32)
    return pl.pallas_call(
        functools.partial(_proj_kernel, act=act),
        grid=(n // tn, t // tm),
        in_specs=[pl.BlockSpec((tm, k), lambda j, i: (i, 0)),
                  _weight_rows_spec(layer, row0, tn, k, lambda j, i: j),
                  pl.BlockSpec((1, tn), lambda j, i: (0, j))],
        out_specs=pl.BlockSpec((tm, tn), lambda j, i: (i, j)),
        out_shape=jax.ShapeDtypeStruct((t, n), BF16),
        scratch_shapes=[pltpu.VMEM((k, tn), BF16)],
        compiler_params=_cparams(("parallel", "arbitrary")),
        name="proj_" + act,
    )(h, wt, b.reshape(1, n))


def _conv_proj_kernel(a_ref, w_ref, cw_ref, cb_ref, o_ref, ext_ref, wbf_ref):
    tm = a_ref.shape[0]
    halo = SUBLANES

    @pl.when(pl.program_id(1) == 0)
    def _():
        ext_ref[pl.ds(0, halo), :] = jnp.zeros((halo, ext_ref.shape[1]), F32)

    w = _resident_weight(w_ref, wbf_ref)
    p = jnp.dot(a_ref[...], w[...], preferred_element_type=F32)
    ext_ref[pl.ds(halo, tm), :] = p
    acc = cb_ref[...] + cw_ref[CONV_WIDTH - 1:CONV_WIDTH, :] * p
    for k in range(CONV_WIDTH - 1):
        back = CONV_WIDTH - 1 - k
        acc = acc + cw_ref[k:k + 1, :] * ext_ref[pl.ds(halo - back, tm), :]
    o_ref[...] = _silu(acc).astype(o_ref.dtype)
    ext_ref[pl.ds(0, halo), :] = p[tm - halo:, :]


def _conv_proj(h, wt, layer, row0, n, conv_w, conv_b, tm, tn):
    t, k = h.shape
    assert n % tn == 0 and row0 % SUBLANES == 0
    return pl.pallas_call(
        _conv_proj_kernel,
        grid=(n // tn, t // tm),
        in_specs=[pl.BlockSpec((tm, k), lambda j, i: (i, 0)),
                  _weight_rows_spec(layer, row0, tn, k, lambda j, i: j),
                  pl.BlockSpec((CONV_WIDTH, tn), lambda j, i: (0, j)),
                  pl.BlockSpec((1, tn), lambda j, i: (0, j))],
        out_specs=pl.BlockSpec((tm, tn), lambda j, i: (i, j)),
        out_shape=jax.ShapeDtypeStruct((t, n), BF16),
        scratch_shapes=[pltpu.VMEM((tm + SUBLANES, tn), F32),
                        pltpu.VMEM((k, tn), BF16)],
        compiler_params=_cparams(("parallel", "arbitrary")),
        name="proj_conv",
    )(h, wt, conv_w, conv_b.reshape(1, n))


def _dt_proj_kernel(a_ref, wt_ref, b_ref, dt_ref, dtt_ref, wbf_ref):
    w = _resident_weight(wt_ref, wbf_ref, row_axis=0)
    p = jnp.dot(a_ref[...], w[...], preferred_element_type=F32) + b_ref[...]
    dt = _softplus(p)
    dt_ref[...] = dt
    dtt_ref[...] = dt.T


def _dt_proj(h, wt, layer, row0, b_pad, tm):
    t, k = h.shape
    assert row0 % SUBLANES == 0
    return pl.pallas_call(
        _dt_proj_kernel,
        grid=(t // tm,),
        in_specs=[pl.BlockSpec((tm, k), lambda i: (i, 0)),
                  _weight_rows_spec(layer, row0, LANES, k, lambda i: 0),
                  pl.BlockSpec((1, LANES), lambda i: (0, 0))],
        out_specs=[pl.BlockSpec((tm, LANES), lambda i: (i, 0)),
                   pl.BlockSpec((LANES, tm), lambda i: (0, i))],
        out_shape=[jax.ShapeDtypeStruct((t, LANES), F32),
                   jax.ShapeDtypeStruct((LANES, t), F32)],
        scratch_shapes=[pltpu.VMEM((k, LANES), BF16)],
        compiler_params=_cparams(("arbitrary",)),
        name="proj_dt",
    )(h, wt, b_pad)


def _step_major_perm(n_rows):
    assert n_rows == S5_CHUNK * S5_CHUNK
    shift = S5_CHUNK.bit_length() - 1
    row = lax.broadcasted_iota(jnp.int32, (n_rows, n_rows), 0)
    col = lax.broadcasted_iota(jnp.int32, (n_rows, n_rows), 1)
    swapped = ((row & (S5_CHUNK - 1)) << shift) | (row >> shift)
    return jnp.where(col == swapped, 1.0, 0.0).astype(BF16)


def _chunk_proj_kernel(a_ref, w_ref, o_ref, wbf_ref):
    w = _resident_weight(w_ref, wbf_ref, row_axis=0)
    u = jnp.dot(a_ref[...], w[...], preferred_element_type=F32).astype(BF16)
    u_steps = jnp.dot(_step_major_perm(u.shape[0]), u, preferred_element_type=F32).astype(o_ref.dtype)
    nj = o_ref.shape[1]
    for s in range(o_ref.shape[0]):
        o_ref[s] = u_steps[s * nj:(s + 1) * nj, :]


def _chunk_proj(h, wt, layer, row0, n):
    t, k = h.shape
    nj = t // S5_CHUNK
    tm = S5_CHUNK * S5_CHUNK
    assert row0 % SUBLANES == 0
    return pl.pallas_call(
        _chunk_proj_kernel,
        grid=(t // tm,),
        in_specs=[pl.BlockSpec((tm, k), lambda i: (i, 0)),
                  _weight_rows_spec(layer, row0, n, k, lambda i: 0)],
        out_specs=pl.BlockSpec((S5_CHUNK, S5_CHUNK, n), lambda i: (0, i, 0)),
        out_shape=jax.ShapeDtypeStruct((S5_CHUNK, nj, n), BF16),
        scratch_shapes=[pltpu.VMEM((k, n), BF16)],
        compiler_params=_cparams(("arbitrary",)),
        name="proj_u",
    )(h, wt)


def _cumsum_rows(v):
    n = v.shape[0]
    idx = lax.broadcasted_iota(jnp.int32, v.shape, 0)
    k = 1
    while k < n:
        v = v + jnp.where(idx >= k, pltpu.roll(v, k, 0), 0.0)
        k *= 2
    return v


def _cumsum_lanes(v):
    n = v.shape[1]
    idx = lax.broadcasted_iota(jnp.int32, v.shape, 1)
    k = 1
    while k < n:
        v = v + jnp.where(idx >= k, pltpu.roll(v, k, 1), 0.0)
        k *= 2
    return v


def _ssd_kernel(xbc_ref, dt_ref, dtt_ref, alog_r_ref, alog_c_ref, dskip_ref, o_ref, state_ref, *,
                n_heads):
    q = dt_ref.shape[0]
    n = SSD_STATE
    p_dim = SSD_HEAD_DIM
    r_heads = n_heads // SSD_GROUPS
    gw = r_heads * p_dim
    inner = n_heads * p_dim

    @pl.when(pl.program_id(0) == 0)
    def _():
        state_ref[...] = jnp.zeros(state_ref.shape, F32)

    dtt = dtt_ref[...]
    cs_col = _cumsum_rows(dt_ref[...] * -jnp.exp(alog_r_ref[...]))
    cs_row = _cumsum_lanes(dtt * -jnp.exp(alog_c_ref[...]))
    causal = lax.broadcasted_iota(jnp.int32, (q, q), 0) >= lax.broadcasted_iota(jnp.int32, (q, q), 1)
    head_of_lane = lax.broadcasted_iota(jnp.int32, (1, gw), 1) // p_dim
    eye = jnp.where(lax.broadcasted_iota(jnp.int32, (n, n), 0) == lax.broadcasted_iota(jnp.int32, (n, n), 1),
                    1.0, 0.0).astype(BF16)
    nt = (((1,), (1,)), ((), ()))

    for g in range(SSD_GROUPS):
        x_g = xbc_ref[:, g * gw:(g + 1) * gw]
        b_g = xbc_ref[:, inner + g * n:inner + (g + 1) * n]
        c_g = xbc_ref[:, inner + (SSD_GROUPS + g) * n:inner + (SSD_GROUPS + g + 1) * n]
        cb = lax.dot_general(c_g, b_g, nt, preferred_element_type=F32)
        b_t = lax.dot_general(eye, b_g, nt, preferred_element_type=F32)
        c_f = c_g.astype(F32)
        s_g = state_ref[g]
        lhs_parts, bt_parts, cd = [], [], jnp.zeros((1, gw), F32)
        for r in range(r_heads):
            h = g * r_heads + r
            csb = jnp.broadcast_to(cs_col[:, h:h + 1], (q, n))
            csr = cs_row[h:h + 1, :]
            dtr = dtt[h:h + 1, :]
            cs_last = csr[:, q - 1:q]
            decay = jnp.exp(jnp.where(causal, csb - csr, -1e30))
            lhs_parts.append(jnp.concatenate([cb * decay * dtr, c_f * jnp.exp(csb)], axis=1).astype(BF16))
            bt_parts.append((b_t * (dtr * jnp.exp(cs_last - csr))).astype(BF16))
            cd = jnp.where(head_of_lane == r, jnp.exp(cs_last), cd)
        rhs = jnp.concatenate([x_g, s_g.astype(BF16)], axis=0)
        per_half = LANES // p_dim
        y_halves, s_halves = [], []
        for half in range(gw // LANES):
            heads = range(half * per_half, (half + 1) * per_half)
            cols = slice(half * LANES, (half + 1) * LANES)
            y_all = jnp.dot(jnp.concatenate([lhs_parts[r] for r in heads], axis=0), rhs[:, cols],
                            preferred_element_type=F32)
            s_all = jnp.dot(jnp.concatenate([bt_parts[r] for r in heads], axis=0), x_g[:, cols],
                            preferred_element_type=F32)
            lane_head = head_of_lane[:, cols]
            y_h, s_h = y_all[:q, :], s_all[:n, :]
            for j, r in enumerate(heads):
                if j:
                    y_h = jnp.where(lane_head == r, y_all[j * q:(j + 1) * q, :], y_h)
                    s_h = jnp.where(lane_head == r, s_all[j * n:(j + 1) * n, :], s_h)
            y_halves.append(y_h)
            s_halves.append(s_h)
        state_ref[g] = s_g * cd + jnp.concatenate(s_halves, axis=1)
        y_g = jnp.concatenate(y_halves, axis=1) + x_g.astype(F32) * dskip_ref[:, g * gw:(g + 1) * gw]
        o_ref[:, g * gw:(g + 1) * gw] = y_g.astype(o_ref.dtype)


def _ssd(xbc, dt, dtt, a_log, d_skip, n_heads, q):
    t, width = xbc.shape
    inner = n_heads * SSD_HEAD_DIM
    gw = inner // SSD_GROUPS
    assert n_heads % SUBLANES == 0 and n_heads <= LANES
    alog_r = jnp.zeros((1, LANES), F32).at[0, :n_heads].set(a_log)
    alog_c = a_log.reshape(n_heads, 1)
    dskip = jnp.repeat(d_skip, SSD_HEAD_DIM).reshape(1, inner)
    return pl.pallas_call(
        functools.partial(_ssd_kernel, n_heads=n_heads),
        grid=(t // q,),
        in_specs=[pl.BlockSpec((q, width), lambda c: (c, 0)),
                  pl.BlockSpec((q, LANES), lambda c: (c, 0)),
                  pl.BlockSpec((n_heads, q), lambda c: (0, c)),
                  pl.BlockSpec((1, LANES), lambda c: (0, 0)),
                  pl.BlockSpec((n_heads, 1), lambda c: (0, 0)),
                  pl.BlockSpec((1, inner), lambda c: (0, 0))],
        out_specs=pl.BlockSpec((q, inner), lambda c: (c, 0)),
        out_shape=jax.ShapeDtypeStruct((t, inner), BF16),
        scratch_shapes=[pltpu.VMEM((SSD_GROUPS, SSD_STATE, gw), F32)],
        compiler_params=_cparams(("arbitrary",)),
        name="ssd_scan",
    )(xbc, dt, dtt, alog_r, alog_c, dskip)


def _gated_up_kernel(y_ref, z_ref, nw_ref, w_ref, o_ref):
    v = y_ref[...].astype(F32) * z_ref[...].astype(F32)
    na = _rms(v, nw_ref[...]).astype(BF16)
    o_ref[...] = jnp.dot(na, w_ref[...], preferred_element_type=F32).astype(o_ref.dtype)


def _gated_up(y, zs, norm_w, w, tm):
    t, d = y.shape
    n = w.shape[1]
    return pl.pallas_call(
        _gated_up_kernel,
        grid=(t // tm,),
        in_specs=[pl.BlockSpec((tm, d), lambda i: (i, 0)),
                  pl.BlockSpec((tm, d), lambda i: (i, 0)),
                  pl.BlockSpec((1, d), lambda i: (0, 0)),
                  pl.BlockSpec((d, n), lambda i: (0, 0))],
        out_specs=pl.BlockSpec((tm, n), lambda i: (i, 0)),
        out_shape=jax.ShapeDtypeStruct((t, n), BF16),
        compiler_params=_cparams(("parallel",)),
        name="ssd_gated_up",
    )(y, zs, norm_w.reshape(1, d), w)


def _s5_operators(lam_re, lam_im, log_dt, b_re, b_im, c_re, c_im):
    ng, ns = lam_re.shape
    nc = S5_GROUP_CH
    L = S5_CHUNK
    per = S5_SUPER // nc
    nsg = ng // per
    lr, li = lam_re.astype(F32), lam_im.astype(F32)
    dt = jnp.exp(log_dt.astype(F32))[:, None]
    mag = jnp.exp(lr * dt)
    ang = li * dt
    abar_r, abar_i = mag * jnp.cos(ang), mag * jnp.sin(ang)
    den = lr * lr + li * li
    nr, ni = abar_r - 1.0, abar_i
    coef_r = (nr * lr + ni * li) / den
    coef_i = (ni * lr - nr * li) / den
    bre, bim = b_re.astype(F32), b_im.astype(F32)
    bb_r = coef_r[..., None] * bre - coef_i[..., None] * bim
    bb_i = coef_r[..., None] * bim + coef_i[..., None] * bre
    cre, cim = c_re.astype(F32), c_im.astype(F32)
    ks = jnp.arange(L + 1, dtype=F32)[:, None, None]
    pmag = jnp.exp(ks * (lr * dt)[None])
    pang = ks * ang[None]
    pw_r, pw_i = pmag * jnp.cos(pang), pmag * jnp.sin(pang)
    ca_r = cre[None] * pw_r[:, :, None, :] - cim[None] * pw_i[:, :, None, :]
    ca_i = cre[None] * pw_i[:, :, None, :] + cim[None] * pw_r[:, :, None, :]
    def _rows_n(v, steps):
        return v.reshape(steps, nsg, per, nc, ns).transpose(1, 4, 0, 2, 3).reshape(nsg, ns, steps * per * nc)

    uc = jnp.concatenate([_rows_n(ca_r, L + 1), -_rows_n(ca_i, L + 1)], axis=1)
    ks_rev = (L - 1) - jnp.arange(L, dtype=F32)[:, None, None]
    rmag = jnp.exp(ks_rev * (lr * dt)[None])
    rang = ks_rev * ang[None]
    rev_r, rev_i = rmag * jnp.cos(rang), rmag * jnp.sin(rang)
    ab_r = rev_r[..., None] * bb_r[None] - rev_i[..., None] * bb_i[None]
    ab_i = rev_r[..., None] * bb_i[None] + rev_i[..., None] * bb_r[None]
    ab_rt, ab_it = ab_r.transpose(0, 1, 3, 2), ab_i.transpose(0, 1, 3, 2)
    wc = jnp.concatenate([_rows_n(ab_rt, L), _rows_n(ab_it, L)], axis=1)

    def _rows_gc(v):
        return v.reshape(nsg, per, ns, nc).transpose(0, 1, 3, 2).reshape(nsg, per * nc, ns)

    bbt = jnp.concatenate([_rows_gc(bb_r), _rows_gc(bb_i)], axis=2)
    a_chunk = jnp.concatenate([pw_r[L].reshape(nsg, 1, per * ns),
                               pw_i[L].reshape(nsg, 1, per * ns)], axis=2)
    return uc, wc, bbt, a_chunk


def _split_bf16(v):
    hi = v.astype(BF16)
    return hi, (v - hi.astype(F32)).astype(BF16)


def _s5_kernel(u_ref, uc_ref, wc_ref, bbt_ref, ach_ref, dsk_ref, o_ref,
               toep_ref, wt_ref, v_ref, x_ref, sp_ref, carry_ref):
    jb = pl.program_id(1)
    nl, nj, cw = u_ref.shape
    half = carry_ref.shape[1] // 2
    per = cw // S5_GROUP_CH
    ns = half // per
    nt = (((1,), (1,)), ((), ()))

    @pl.when(jb == 0)
    def _():
        carry_ref[...] = jnp.zeros(carry_ref.shape, F32)
        uc = uc_ref[...]
        wc = wc_ref[...]
        b_hi, b_lo = _split_bf16(bbt_ref[...])
        u_hi, u_lo = _split_bf16(uc[:, :nl * cw])
        kall = (jnp.dot(b_hi, u_hi, preferred_element_type=F32)
                + jnp.dot(b_lo, u_hi, preferred_element_type=F32)
                + jnp.dot(b_hi, u_lo, preferred_element_type=F32))
        row_g = lax.broadcasted_iota(jnp.int32, (cw, 1), 0) // S5_GROUP_CH
        col_g = (lax.broadcasted_iota(jnp.int32, (1, nl * cw), 1) // S5_GROUP_CH) % per
        kall = jnp.where(row_g == col_g, kall, 0.0).astype(BF16)
        toep_ref[...] = jnp.zeros(toep_ref.shape, toep_ref.dtype)
        for s_in in range(nl):
            for s_out in range(s_in, nl):
                k = s_out - s_in
                toep_ref[pl.ds(s_in * cw, cw), pl.ds(s_out * cw, cw)] = kall[:, k * cw:(k + 1) * cw]
        for gp in range(per):
            mine = col_g == gp
            for part in range(2):
                rows = pl.ds(part * half + gp * ns, ns)
                src = slice(part * ns, (part + 1) * ns)
                v_ref[rows, :] = jnp.where(mine, uc[src, cw:], 0.0).astype(BF16)
                wt_ref[rows, :] = jnp.where(mine, wc[src, :], 0.0).astype(BF16)

    lhs = jnp.concatenate([u_ref[s] for s in range(nl)], axis=1)
    x_ref[...] = lax.dot_general(lhs, wt_ref[...], nt, preferred_element_type=F32)

    a_re = ach_ref[:, :half]
    a_im = ach_ref[:, half:]

    def step(j, carry):
        s_re, s_im = carry
        sp_ref[pl.ds(j, 1), :half] = s_re
        sp_ref[pl.ds(j, 1), half:] = s_im
        xr = x_ref[pl.ds(j, 1), :half]
        xi = x_ref[pl.ds(j, 1), half:]
        return (a_re * s_re - a_im * s_im + xr, a_re * s_im + a_im * s_re + xi)

    s_re, s_im = lax.fori_loop(0, nj, step, (carry_ref[:, :half], carry_ref[:, half:]), unroll=8)
    carry_ref[:, :half] = s_re
    carry_ref[:, half:] = s_im

    y_state = jnp.dot(sp_ref[...].astype(BF16), v_ref[...], preferred_element_type=F32)
    dsk = dsk_ref[...]
    pair = 2 * cw
    for tp in range(nl // 2):
        kdim = pair * (tp + 1)
        y = jnp.dot(lhs[:, :kdim], toep_ref[pl.ds(0, kdim), pl.ds(tp * pair, pair)],
                    preferred_element_type=F32)
        y = y + y_state[:, tp * pair:(tp + 1) * pair]
        for h in range(2):
            s = 2 * tp + h
            o_ref[s] = (y[:, h * cw:(h + 1) * cw] + dsk * u_ref[s].astype(F32)).astype(o_ref.dtype)


def _s5(u_steps, uc, wc, bbt, a_chunk, d_skip, tj):
    nl, nj, width = u_steps.shape
    nsg = width // S5_SUPER
    nstate = a_chunk.shape[2]
    rows = uc.shape[1]
    return pl.pallas_call(
        _s5_kernel,
        grid=(nsg, nj // tj),
        in_specs=[pl.BlockSpec((nl, tj, S5_SUPER), lambda g, j: (0, j, g)),
                  pl.BlockSpec((None, rows, (nl + 1) * S5_SUPER), lambda g, j: (g, 0, 0)),
                  pl.BlockSpec((None, rows, nl * S5_SUPER), lambda g, j: (g, 0, 0)),
                  pl.BlockSpec((None, S5_SUPER, rows), lambda g, j: (g, 0, 0)),
                  pl.BlockSpec((None, 1, nstate), lambda g, j: (g, 0, 0)),
                  pl.BlockSpec((None, 1, S5_SUPER), lambda g, j: (g, 0, 0))],
        out_specs=pl.BlockSpec((nl, tj, S5_SUPER), lambda g, j: (0, j, g)),
        out_shape=jax.ShapeDtypeStruct((nl, nj, width), BF16),
        scratch_shapes=[pltpu.VMEM((nl * S5_SUPER, nl * S5_SUPER), BF16),
                        pltpu.VMEM((nstate, nl * S5_SUPER), BF16),
                        pltpu.VMEM((nstate, nl * S5_SUPER), BF16),
                        pltpu.VMEM((tj, nstate), F32),
                        pltpu.VMEM((tj, nstate), F32),
                        pltpu.VMEM((1, nstate), F32)],
        compiler_params=_cparams(("parallel", "arbitrary")),
        name="s5_scan",
    )(u_steps, uc, wc, bbt, a_chunk, d_skip.reshape(nsg, 1, S5_SUPER))


def _glu_up_kernel(y_ref, wg_ref, wu_ref, o_ref):
    y_steps = jnp.concatenate([y_ref[s] for s in range(y_ref.shape[0])], axis=0)
    y = jnp.dot(_step_major_perm(y_steps.shape[0]), y_steps, preferred_element_type=F32)
    v = _gelu_tanh(y)
    gate = _sigmoid(jnp.dot(v.astype(BF16), wg_ref[...], preferred_element_type=F32))
    o_ref[...] = jnp.dot((v * gate).astype(BF16), wu_ref[...],
                         preferred_element_type=F32).astype(o_ref.dtype)


def _glu_up(y_steps, w_glu, w_up):
    nl, nj, width = y_steps.shape
    n = w_up.shape[1]
    tm = S5_CHUNK * S5_CHUNK
    return pl.pallas_call(
        _glu_up_kernel,
        grid=(nj // S5_CHUNK,),
        in_specs=[pl.BlockSpec((nl, S5_CHUNK, width), lambda i: (0, i, 0)),
                  pl.BlockSpec((width, width), lambda i: (0, 0)),
                  pl.BlockSpec((width, n), lambda i: (0, 0))],
        out_specs=pl.BlockSpec((tm, n), lambda i: (i, 0)),
        out_shape=jax.ShapeDtypeStruct((nj * nl, n), BF16),
        compiler_params=_cparams(("parallel",)),
        name="s5_glu_up",
    )(y_steps, w_glu, w_up)


def _merge_out_kernel(ga_ref, gb_ref, ya_ref, yb_ref, x_ref, w_ref, nw_ref, rw_ref, rb_ref,
                      x1_ref, h2_ref, rt_ref):
    merged = (ga_ref[...].astype(F32) * ya_ref[...].astype(F32)
              + gb_ref[...].astype(F32) * yb_ref[...].astype(F32))
    x1 = x_ref[...] + jnp.dot(merged.astype(BF16), w_ref[...], preferred_element_type=F32)
    x1_ref[...] = x1
    h2 = _rms(x1, nw_ref[...])
    h2_ref[...] = _pack_bf16_pairs(h2)
    lg = jnp.dot(h2.astype(BF16), rw_ref[...], preferred_element_type=F32)
    rt_ref[...] = _route_math(lg + rb_ref[...])


def _merge_out(gates, ya, yb, x, w_out, norm_w, r_w, r_b, tm):
    t, d = x.shape
    row = lambda i: (i, 0)
    full = lambda i: (0, 0)
    return pl.pallas_call(
        _merge_out_kernel,
        grid=(t // tm,),
        in_specs=[pl.BlockSpec((tm, d), row),
                  pl.BlockSpec((tm, d), lambda i: (i, 1)),
                  pl.BlockSpec((tm, d), row),
                  pl.BlockSpec((tm, d), row),
                  pl.BlockSpec((tm, d), row),
                  pl.BlockSpec((d, d), full),
                  pl.BlockSpec((1, d), full),
                  pl.BlockSpec((d, LANES), full),
                  pl.BlockSpec((1, LANES), full)],
        out_specs=[pl.BlockSpec((tm, d), row),
                   pl.BlockSpec((tm, d // 2), row),
                   pl.BlockSpec((tm, LANES), row)],
        out_shape=[jax.ShapeDtypeStruct((t, d), F32),
                   jax.ShapeDtypeStruct((t, d // 2), jnp.uint32),
                   jax.ShapeDtypeStruct((t, LANES), F32)],
        compiler_params=_cparams(("parallel",)),
        name="merge_out",
    )(gates, gates, ya, yb, x, w_out, norm_w.reshape(1, d), r_w, r_b)


ROUTE_ID_LANE = 0
ROUTE_W_LANE = TOP_K_INNER
ROUTE_EXPERT_LANE0 = SUBLANES


def _route_math(lg):
    lane = lax.broadcasted_iota(jnp.int32, lg.shape, 1)
    neg = -jnp.inf
    big = LANES
    is_g = lane < N_EXPERT_GROUPS
    gl = jnp.where(is_g, lg, neg)
    gmax = jnp.max(gl, axis=1, keepdims=True)
    grp = jnp.min(jnp.where(gl == gmax, lane, big), axis=1, keepdims=True)
    pg_sel = 1.0 / jnp.sum(jnp.where(is_g, jnp.exp(lg - gmax), 0.0), axis=1, keepdims=True)
    e_lo = ROUTE_EXPERT_LANE0 + grp * EXPERTS_PER_GROUP
    in_grp = (lane >= e_lo) & (lane < e_lo + EXPERTS_PER_GROUP)
    el = jnp.where(in_grp, lg, neg)
    v0 = jnp.max(el, axis=1, keepdims=True)
    i0 = jnp.min(jnp.where(el == v0, lane, big), axis=1, keepdims=True)
    el1 = jnp.where(lane == i0, neg, el)
    v1 = jnp.max(el1, axis=1, keepdims=True)
    i1 = jnp.min(jnp.where(el1 == v1, lane, big), axis=1, keepdims=True)
    e1w = jnp.exp(v1 - v0)
    w0 = pg_sel / (1.0 + e1w)
    w1 = pg_sel * e1w / (1.0 + e1w)
    return jnp.where(lane == ROUTE_ID_LANE, (i0 - ROUTE_EXPERT_LANE0).astype(F32),
           jnp.where(lane == ROUTE_ID_LANE + 1, (i1 - ROUTE_EXPERT_LANE0).astype(F32),
           jnp.where(lane == ROUTE_W_LANE, w0, jnp.where(lane == ROUTE_W_LANE + 1, w1, 0.0))))


def _dispatch_plan(expert_ids, bm):
    n_tokens = expert_ids.shape[0]
    n_assign = n_tokens * TOP_K_INNER
    eid = expert_ids.reshape(n_assign)
    experts = jnp.arange(N_EXPERTS, dtype=jnp.int32)
    onehot = (eid[:, None] == experts[None, :]).astype(jnp.int32)
    csum = jnp.cumsum(onehot, axis=0)
    counts = csum[-1]
    rank = jnp.sum(onehot * csum, axis=1) - 1
    padded = ((counts + bm - 1) // bm) * bm
    ends = jnp.cumsum(padded)
    starts = ends - padded
    dest = jnp.sum(onehot * starts[None, :], axis=1) + rank
    n_rows = n_assign + N_EXPERTS * bm
    nb = n_rows // bm
    block_start = jnp.arange(nb, dtype=jnp.int32) * bm
    block_expert = jnp.minimum(jnp.sum((ends[None, :] <= block_start[:, None]).astype(jnp.int32), axis=1),
                               N_EXPERTS - 1)
    n_used = (ends[-1] // bm).astype(jnp.int32).reshape(1)
    return (dest.reshape(n_tokens, TOP_K_INNER), block_expert, n_used,
            (starts + counts).astype(jnp.int32), (padded - counts).astype(jnp.int32), n_rows)


def _dispatch_kernel(pad_start_ref, pad_count_ref, nused_ref, dest_ref, h_ref, xs_hbm, stage, zrow, sem, zsem):
    i = pl.program_id(0)
    n_steps = pl.num_programs(0)
    tm = h_ref.shape[0]
    bm = zrow.shape[0]
    nb = xs_hbm.shape[0] // bm
    slot = i % 2

    def tile_wait(s):
        for _ in range(TOP_K_INNER):
            pltpu.make_async_copy(stage.at[s], xs_hbm.at[pl.ds(0, tm)], sem.at[s]).wait()

    @pl.when(i >= 2)
    def _():
        tile_wait(slot)

    _store_token_tiles(stage.at[slot], h_ref[...])

    for r in range(tm):
        for k in range(TOP_K_INNER):
            pltpu.make_async_copy(stage.at[slot, r], xs_hbm.at[dest_ref[k, r]], sem.at[slot]).start()

    @pl.when(i == n_steps - 1)
    def _():
        tile_wait(slot)

        @pl.when(n_steps >= 2)
        def _():
            tile_wait(1 - slot)

        zrow[...] = jnp.zeros(zrow.shape, zrow.dtype)

        def pad_copy(e, r):
            return pltpu.make_async_copy(zrow.at[0], xs_hbm.at[pad_start_ref[e] + r], zsem)

        def block_copy(b):
            return pltpu.make_async_copy(zrow, xs_hbm.at[pl.ds(b * bm, bm)], zsem)
        for e in range(N_EXPERTS):
            lax.fori_loop(0, pad_count_ref[e], lambda r, c, e=e: (pad_copy(e, r).start(), c)[1], 0)
        lax.fori_loop(nused_ref[0], nb, lambda b, c: (block_copy(b).start(), c)[1], 0)
        for e in range(N_EXPERTS):
            lax.fori_loop(0, pad_count_ref[e], lambda r, c, e=e: (pad_copy(e, r).wait(), c)[1], 0)
        lax.fori_loop(nused_ref[0], nb, lambda b, c: (block_copy(b).wait(), c)[1], 0)


def _dispatch(h2, dest, pad_start, pad_count, n_used, n_rows, tm, bm):
    t, width = h2.shape
    tile = (width // LANES, LANES)
    nt = t // tm
    dest_t = dest.reshape(nt, tm, TOP_K_INNER).transpose(0, 2, 1)
    grid_spec = pltpu.PrefetchScalarGridSpec(
        num_scalar_prefetch=3,
        grid=(nt,),
        in_specs=[pl.BlockSpec((None, TOP_K_INNER, tm), lambda i, *_: (i, 0, 0), memory_space=pltpu.SMEM),
                  pl.BlockSpec((tm, width), lambda i, *_: (i, 0))],
        out_specs=pl.BlockSpec(memory_space=pl.ANY),
        scratch_shapes=[pltpu.VMEM((2, tm) + tile, h2.dtype),
                        pltpu.VMEM((bm,) + tile, h2.dtype),
                        pltpu.SemaphoreType.DMA((2,)),
                        pltpu.SemaphoreType.DMA(())],
    )
    return pl.pallas_call(
        _dispatch_kernel,
        grid_spec=grid_spec,
        out_shape=jax.ShapeDtypeStruct((n_rows,) + tile, h2.dtype),
        compiler_params=_cparams(("arbitrary",)),
        name="moe_dispatch",
    )(pad_start, pad_count, n_used, dest_t, h2)


def _experts_kernel(bexp_ref, nused_ref, first_ref, next_ref, slot_ref, xs_ref, wg_hbm, wu_hbm, wd_hbm,
                    o_ref, wg_f, wu_f, wd_f, wg_s, wu_s, wd_s, sem):
    b = pl.program_id(0)
    n_used = nused_ref[0]

    def weight_copies(e, s):
        return (pltpu.make_async_copy(wg_hbm.at[e], wg_f.at[s], sem.at[s, 0]),
                pltpu.make_async_copy(wu_hbm.at[e], wu_f.at[s], sem.at[s, 1]),
                pltpu.make_async_copy(wd_hbm.at[e], wd_f.at[s], sem.at[s, 2]))

    @pl.when((b == 0) & (n_used > 0))
    def _():
        for c in weight_copies(bexp_ref[0], 0):
            c.start()

    @pl.when((b < n_used) & (first_ref[b] == 1))
    def _():
        s = slot_ref[b]
        for c in weight_copies(bexp_ref[b], s):
            c.wait()

        @pl.when(next_ref[b] >= 0)
        def _():
            for c in weight_copies(next_ref[b], 1 - s):
                c.start()

        wg_s[...] = wg_f[s].astype(BF16)
        wu_s[...] = wu_f[s].astype(BF16)
        wd_s[...] = wd_f[s].astype(BF16)

    @pl.when(b < n_used)
    def _():
        xb = _unpack_bf16_pairs(_load_token_tiles(xs_ref)).astype(BF16)
        hg = jnp.dot(xb, wg_s[...], preferred_element_type=F32)
        hu = jnp.dot(xb, wu_s[...], preferred_element_type=F32)
        act = (_silu(hg) * hu).astype(BF16)
        _store_token_tiles(o_ref, _pack_bf16_pairs(jnp.dot(act, wd_s[...], preferred_element_type=F32)))

    @pl.when(b >= n_used)
    def _():
        o_ref[...] = jnp.zeros(o_ref.shape, o_ref.dtype)


def _experts(xs, block_expert, n_used, w_g, w_u, w_d, bm):
    n_rows = xs.shape[0]
    tile = xs.shape[1:]
    d, ff = w_g.shape[1], w_g.shape[2]
    assert d == 2 * tile[0] * tile[1]
    nb = n_rows // bm
    blk = jnp.arange(nb, dtype=jnp.int32)
    valid = blk < n_used[0]
    first = (valid & ((blk == 0) | (block_expert != jnp.roll(block_expert, 1)))).astype(jnp.int32)
    later = valid[None, :] & (block_expert[None, :] > block_expert[:, None])
    nxt = jnp.min(jnp.where(later, block_expert[None, :], N_EXPERTS), axis=1)
    nxt = jnp.where(nxt == N_EXPERTS, -1, nxt).astype(jnp.int32)
    slot = ((jnp.cumsum(first) - 1) % 2).astype(jnp.int32)

    def used(b, nu):
        return jnp.minimum(b, jnp.maximum(nu[0] - 1, 0))

    grid_spec = pltpu.PrefetchScalarGridSpec(
        num_scalar_prefetch=5,
        grid=(nb,),
        in_specs=[pl.BlockSpec((bm,) + tile, lambda b, be, nu, *_: (used(b, nu), 0, 0)),
                  pl.BlockSpec(memory_space=pl.ANY),
                  pl.BlockSpec(memory_space=pl.ANY),
                  pl.BlockSpec(memory_space=pl.ANY)],
        out_specs=pl.BlockSpec((bm,) + tile, lambda b, *_: (b, 0, 0)),
        scratch_shapes=[pltpu.VMEM((2, d, ff), w_g.dtype),
                        pltpu.VMEM((2, d, ff), w_u.dtype),
                        pltpu.VMEM((2, ff, d), w_d.dtype),
                        pltpu.VMEM((d, ff), BF16),
                        pltpu.VMEM((d, ff), BF16),
                        pltpu.VMEM((ff, d), BF16),
                        pltpu.SemaphoreType.DMA((2, 3))],
    )
    return pl.pallas_call(
        _experts_kernel,
        grid_spec=grid_spec,
        out_shape=jax.ShapeDtypeStruct((n_rows,) + tile, xs.dtype),
        compiler_params=_cparams(("arbitrary",)),
        name="moe_experts",
    )(block_expert, n_used, first, nxt, slot, xs, w_g, w_u, w_d)


def _combine_kernel(pos_ref, x1_ref, rt_ref, nw_ref, y_hbm, o_ref, ybuf, sem, *, normalize):
    i = pl.program_id(0)
    n = pl.num_programs(0)
    tm = x1_ref.shape[0]

    def start_tile(slot, which):
        for r in range(tm):
            for k in range(TOP_K_INNER):
                pltpu.make_async_copy(y_hbm.at[pos_ref[which, k, r]], ybuf.at[slot, k, r],
                                      sem.at[slot]).start()

    def wait_tile(slot):
        for k in range(TOP_K_INNER):
            pltpu.make_async_copy(y_hbm.at[pl.ds(0, tm)], ybuf.at[slot, k], sem.at[slot]).wait()

    slot = i % 2

    @pl.when(i == 0)
    def _():
        start_tile(0, 0)

    wait_tile(slot)
    start_tile(1 - slot, 1)
    acc = x1_ref[...]
    for k in range(TOP_K_INNER):
        y_k = _unpack_bf16_pairs(_load_token_tiles(ybuf.at[slot, k]))
        acc = acc + rt_ref[:, ROUTE_W_LANE + k:ROUTE_W_LANE + k + 1] * y_k
    o_ref[...] = _rms(acc, nw_ref[...]) if normalize else acc

    @pl.when(i == n - 1)
    def _():
        wait_tile(1 - slot)


def _combine(x1, y_rows, pos, route, norm_w, tm, normalize):
    t, d = x1.shape
    nt = t // tm
    pos_t = pos.reshape(nt, tm, TOP_K_INNER).transpose(0, 2, 1)
    pos_next = jnp.concatenate([pos_t[1:], pos_t[-1:]], axis=0)
    pos2 = jnp.stack([pos_t, pos_next], axis=1)
    return pl.pallas_call(
        functools.partial(_combine_kernel, normalize=normalize),
        grid=(nt,),
        in_specs=[pl.BlockSpec((None, 2, TOP_K_INNER, tm), lambda i: (i, 0, 0, 0), memory_space=pltpu.SMEM),
                  pl.BlockSpec((tm, d), lambda i: (i, 0)),
                  pl.BlockSpec((tm, LANES), lambda i: (i, 0)),
                  pl.BlockSpec((1, d), lambda i: (0, 0)),
                  pl.BlockSpec(memory_space=pl.ANY)],
        out_specs=pl.BlockSpec((tm, d), lambda i: (i, 0)),
        out_shape=jax.ShapeDtypeStruct((t, d), F32),
        scratch_shapes=[pltpu.VMEM((2, TOP_K_INNER, tm) + y_rows.shape[1:], y_rows.dtype),
                        pltpu.SemaphoreType.DMA((2,))],
        compiler_params=_cparams(("arbitrary",)),
        name="moe_combine",
    )(pos2, x1, route, norm_w.reshape(1, d), y_rows)


def _layer(x, p, w_in_t, layer):
    t, d = x.shape
    inner = p["w_a_up"].shape[0]
    n_heads = p["a_log"].shape[0]
    s5_width = p["w_glu"].shape[0]
    xbc_dim = inner + 2 * SSD_GROUPS * SSD_STATE
    sizes = (inner, xbc_dim, n_heads, s5_width, 2 * d)
    offs = [0]
    for s in sizes:
        offs.append(offs[-1] + s)
    assert n_heads <= LANES and offs[2] + LANES <= offs[5]
    dt_b = jnp.zeros((1, LANES), F32).at[0, :n_heads].set(p["dt_bias"].astype(F32))

    tm = min(ROW_TILE, t)
    tmm = min(MM_ROW_TILE, t)

    h = _rmsnorm(x, p["norm_mix_w"], tm)
    zs = _proj(h, w_in_t, layer, offs[0], inner, None, "silu", tmm, MM_COL_TILE)
    xbc = _conv_proj(h, w_in_t, layer, offs[1], xbc_dim, p["conv_w"], p["conv_b"], tmm, MM_COL_TILE)
    dt, dtt = _dt_proj(h, w_in_t, layer, offs[2], dt_b, tm)
    u_steps = _chunk_proj(h, w_in_t, layer, offs[3], s5_width)
    gates = _proj(h, w_in_t, layer, offs[4], 2 * d, p["gate_b"], "sigmoid_bias", tmm, MM_COL_TILE)

    y = _ssd(xbc, dt, dtt[:n_heads], p["a_log"].astype(F32), p["d_ssd"].astype(F32), n_heads,
             min(SSD_CHUNK, t))
    ya = _gated_up(y, zs, p["norm_ssd_w"], p["w_a_up"].astype(BF16), tm)

    uc, wc, bbt, a_chunk = _s5_operators(p["s5_lambda_re"], p["s5_lambda_im"], p["s5_log_dt"],
                                         p["s5_b_re"], p["s5_b_im"], p["s5_c_re"], p["s5_c_im"])
    y5 = _s5(u_steps, uc, wc, bbt, a_chunk, p["s5_d"].astype(F32), min(S5_ROWS, t // S5_CHUNK))
    yb = _glu_up(y5, p["w_glu"].astype(BF16), p["w_b_up"].astype(BF16))

    w_router = jnp.zeros((d, LANES), F32)
    w_router = w_router.at[:, :N_EXPERT_GROUPS].set(p["w_route_group"].astype(F32))
    w_router = w_router.at[:, ROUTE_EXPERT_LANE0:ROUTE_EXPERT_LANE0 + N_EXPERTS].set(
        p["w_route_expert"].astype(F32))
    r_b = jnp.zeros((1, LANES), F32)
    r_b = r_b.at[0, :N_EXPERT_GROUPS].set(p["b_route_group"].astype(F32))
    r_b = r_b.at[0, ROUTE_EXPERT_LANE0:ROUTE_EXPERT_LANE0 + N_EXPERTS].set(p["b_route_expert"].astype(F32))
    x1, h2, route = _merge_out(gates, ya, yb, x, p["w_out"].astype(BF16), p["norm_ffn_w"],
                               w_router.astype(BF16), r_b, tm)

    expert_ids = route[:, ROUTE_ID_LANE:ROUTE_ID_LANE + TOP_K_INNER].astype(jnp.int32)
    bm = MOE_BLOCK
    tg = min(GATHER_TILE, t)
    pos, block_expert, n_used, pad_start, pad_count, n_rows = _dispatch_plan(expert_ids, bm)
    xs = _dispatch(h2, pos, pad_start, pad_count, n_used, n_rows, tg, bm)
    y_rows = _experts(xs, block_expert, n_used, p["w_exp_gate"], p["w_exp_up"], p["w_exp_down"], bm)
    return x1, y_rows, pos, route


def kernel(x, norm_mix_w, w_in, conv_w, conv_b, dt_bias, a_log, d_ssd, norm_ssd_w, w_a_up,
           s5_lambda_re, s5_lambda_im, s5_log_dt, s5_b_re, s5_b_im, s5_c_re, s5_c_im, s5_d,
           w_glu, w_b_up, gate_b, w_out, norm_ffn_w, w_route_group, b_route_group,
           w_route_expert, b_route_expert, w_exp_gate, w_exp_up, w_exp_down, norm_final_w):
    b, seq, d = x.shape
    assert b == 1, "the scans carry state along the flattened token axis"
    depth = w_in.shape[0]
    per_layer = dict(norm_mix_w=norm_mix_w, conv_w=conv_w, conv_b=conv_b, dt_bias=dt_bias,
                     a_log=a_log, d_ssd=d_ssd, norm_ssd_w=norm_ssd_w, w_a_up=w_a_up,
                     s5_lambda_re=s5_lambda_re, s5_lambda_im=s5_lambda_im, s5_log_dt=s5_log_dt,
                     s5_b_re=s5_b_re, s5_b_im=s5_b_im, s5_c_re=s5_c_re, s5_c_im=s5_c_im, s5_d=s5_d,
                     w_glu=w_glu, w_b_up=w_b_up, gate_b=gate_b, w_out=w_out, norm_ffn_w=norm_ffn_w,
                     w_route_group=w_route_group, b_route_group=b_route_group,
                     w_route_expert=w_route_expert, b_route_expert=b_route_expert,
                     w_exp_gate=w_exp_gate, w_exp_up=w_exp_up, w_exp_down=w_exp_down)
    xt = x.reshape(b * seq, d)
    tg = min(GATHER_TILE, b * seq)
    w_in_t = jnp.swapaxes(w_in.astype(F32), 1, 2)
    for i in range(depth):
        p = {k: v[i] for k, v in per_layer.items()}
        x1, y_rows, pos, route = _layer(xt, p, w_in_t, i)
        xt = _combine(x1, y_rows, pos, route, norm_final_w, tg, normalize=(i == depth - 1))
    return xt.reshape(b, seq, d)
```

```python
import functools
import math

import jax
import jax.numpy as jnp
from jax import lax
from jax.experimental import pallas as pl
from jax.experimental.pallas import tpu as pltpu

F32 = jnp.float32
BF16 = jnp.bfloat16

SSD_HEAD_DIM = 64
SSD_GROUPS = 8
SSD_STATE = 128
CONV_WIDTH = 4
S5_GROUP_CH = 16
S5_STATE = 64
N_EXPERT_GROUPS = 4
EXPERTS_PER_GROUP = 8
N_EXPERTS = N_EXPERT_GROUPS * EXPERTS_PER_GROUP
TOP_K_INNER = 2
RMS_EPS = 1e-6

LANES = 128
SUBLANES = 8
VMEM_LIMIT_BYTES = 52 * 1024 * 1024

ROW_TILE = 512
MM_ROW_TILE = 1024
MM_COL_TILE = 1024
SSD_CHUNK = 128
S5_CHUNK = 16
S5_SUPER = S5_GROUP_CH * 8
S5_ROWS = 256
MOE_BLOCK = 256
GATHER_TILE = 256


def _cparams(sem, vmem=VMEM_LIMIT_BYTES):
    return pltpu.CompilerParams(dimension_semantics=sem, vmem_limit_bytes=vmem)


def _sigmoid(v):
    return 1.0 / (1.0 + jnp.exp(-v))


def _silu(v):
    return v * _sigmoid(v)


def _softplus(v):
    return jnp.maximum(v, 0.0) + jnp.log(1.0 + jnp.exp(-jnp.abs(v)))


def _gelu_tanh(v):
    c = math.sqrt(2.0 / math.pi)
    return 0.5 * v * (1.0 + jnp.tanh(c * (v + 0.044715 * (v * v * v))))


def _rms(v, w):
    ms = jnp.mean(v * v, axis=-1, keepdims=True)
    return v * lax.rsqrt(ms + RMS_EPS) * w


def _pack_bf16_pairs(v):
    n = v.shape[1] // 2
    lo = pltpu.bitcast(v[:, :n].astype(BF16).astype(F32), jnp.uint32)
    hi = pltpu.bitcast(v[:, n:].astype(BF16).astype(F32), jnp.uint32)
    return hi | (lo >> 16)


def _unpack_bf16_pairs(w):
    lo = pltpu.bitcast(w << 16, F32)
    hi = pltpu.bitcast(w & jnp.uint32(0xFFFF0000), F32)
    return jnp.concatenate([lo, hi], axis=1)


def _store_token_tiles(ref, rows):
    for a in range(ref.shape[-2]):
        ref[:, a, :] = rows[:, a * LANES:(a + 1) * LANES]


def _load_token_tiles(ref):
    return jnp.concatenate([ref[:, a, :] for a in range(ref.shape[-2])], axis=1)


def _resident_weight(wt_ref, wbf_ref, row_axis=1):
    @pl.when(pl.program_id(row_axis) == 0)
    def _():
        wbf_ref[...] = wt_ref[...].T.astype(BF16)
    return wbf_ref


def _weight_rows_spec(layer, row0, n_rows, k, index_of_step):
    assert row0 % SUBLANES == 0 and n_rows % SUBLANES == 0
    return pl.BlockSpec((None, pl.Element(n_rows), pl.Element(k)),
                        lambda *idx: (layer, pl.multiple_of(row0 + n_rows * index_of_step(*idx), SUBLANES), 0))


def _proj_kernel(a_ref, wt_ref, b_ref, o_ref, wbf_ref, *, act):
    w = _resident_weight(wt_ref, wbf_ref)
    p = jnp.dot(a_ref[...], w[...], preferred_element_type=F32)
    if act == "silu":
        p = _silu(p)
    elif act == "sigmoid_bias":
        p = _sigmoid(p + b_ref[...])
    o_ref[...] = p.astype(o_ref.dtype)


def _proj(h, wt, layer, row0, n, b, act, tm, tn):
    t, k = h.shape
    tn = min(tn, n)
    assert n % tn == 0 and row0 % SUBLANES == 0
    if b is None:
        b = jnp.zeros((1, n), F32)
    return pl.pallas_call(
        functools.partial(_proj_kernel, act=act),
        grid=(n // tn, t // tm),
        in_specs=[pl.BlockSpec((tm, k), lambda j, i: (i, 0)),
                  _weight_rows_spec(layer, row0, tn, k, lambda j, i: j),
                  pl.BlockSpec((1, tn), lambda j, i: (0, j))],
        out_specs=pl.BlockSpec((tm, tn), lambda j, i: (i, j)),
        out_shape=jax.ShapeDtypeStruct((t, n), BF16),
        scratch_shapes=[pltpu.VMEM((k, tn), BF16)],
        compiler_params=_cparams(("parallel", "arbitrary")),
        name="proj_" + act,
    )(h, wt, b.reshape(1, n))


def _conv_proj_kernel(a_ref, w_ref, cw_ref, cb_ref, o_ref, ext_ref, wbf_ref):
    tm = a_ref.shape[0]
    halo = SUBLANES

    @pl.when(pl.program_id(1) == 0)
    def _():
        ext_ref[pl.ds(0, halo), :] = jnp.zeros((halo, ext_ref.shape[1]), F32)

    w = _resident_weight(w_ref, wbf_ref)
    p = jnp.dot(a_ref[...], w[...], preferred_element_type=F32)
    ext_ref[pl.ds(halo, tm), :] = p
    acc = cb_ref[...] + cw_ref[CONV_WIDTH - 1:CONV_WIDTH, :] * p
    for k in range(CONV_WIDTH - 1):
        back = CONV_WIDTH - 1 - k
        acc = acc + cw_ref[k:k + 1, :] * ext_ref[pl.ds(halo - back, tm), :]
    o_ref[...] = _silu(acc).astype(o_ref.dtype)
    ext_ref[pl.ds(0, halo), :] = p[tm - halo:, :]


def _conv_proj(h, wt, layer, row0, n, conv_w, conv_b, tm, tn):
    t, k = h.shape
    assert n % tn == 0 and row0 % SUBLANES == 0
    return pl.pallas_call(
        _conv_proj_kernel,
        grid=(n // tn, t // tm),
        in_specs=[pl.BlockSpec((tm, k), lambda j, i: (i, 0)),
                  _weight_rows_spec(layer, row0, tn, k, lambda j, i: j),
                  pl.BlockSpec((CONV_WIDTH, tn), lambda j, i: (0, j)),
                  pl.BlockSpec((1, tn), lambda j, i: (0, j))],
        out_specs=pl.BlockSpec((tm, tn), lambda j, i: (i, j)),
        out_shape=jax.ShapeDtypeStruct((t, n), BF16),
        scratch_shapes=[pltpu.VMEM((tm + SUBLANES, tn), F32),
                        pltpu.VMEM((k, tn), BF16)],
        compiler_params=_cparams(("parallel", "arbitrary")),
        name="proj_conv",
    )(h, wt, conv_w, conv_b.reshape(1, n))


def _norm_dt_kernel(x_ref, nw_ref, wt_ref, b_ref, h_ref, dt_ref, dtt_ref, wbf_ref):
    w = _resident_weight(wt_ref, wbf_ref, row_axis=0)
    h = _rms(x_ref[...], nw_ref[...]).astype(BF16)
    h_ref[...] = h
    p = jnp.dot(h, w[...], preferred_element_type=F32) + b_ref[...]
    dt = _softplus(p)
    dt_ref[...] = dt
    dtt_ref[...] = dt.T


def _norm_dt(x, norm_w, wt, layer, row0, b_pad, tm):
    t, k = x.shape
    return pl.pallas_call(
        _norm_dt_kernel,
        grid=(t // tm,),
        in_specs=[pl.BlockSpec((tm, k), lambda i: (i, 0)),
                  pl.BlockSpec((1, k), lambda i: (0, 0)),
                  _weight_rows_spec(layer, row0, LANES, k, lambda i: 0),
                  pl.BlockSpec((1, LANES), lambda i: (0, 0))],
        out_specs=[pl.BlockSpec((tm, k), lambda i: (i, 0)),
                   pl.BlockSpec((tm, LANES), lambda i: (i, 0)),
                   pl.BlockSpec((LANES, tm), lambda i: (0, i))],
        out_shape=[jax.ShapeDtypeStruct((t, k), BF16),
                   jax.ShapeDtypeStruct((t, LANES), F32),
                   jax.ShapeDtypeStruct((LANES, t), F32)],
        scratch_shapes=[pltpu.VMEM((k, LANES), BF16)],
        compiler_params=_cparams(("arbitrary",)),
        name="norm_dt",
    )(x, norm_w.reshape(1, k), wt, b_pad)


def _step_major_perm(n_rows):
    assert n_rows == S5_CHUNK * S5_CHUNK
    shift = S5_CHUNK.bit_length() - 1
    row = lax.broadcasted_iota(jnp.int32, (n_rows, n_rows), 0)
    col = lax.broadcasted_iota(jnp.int32, (n_rows, n_rows), 1)
    swapped = ((row & (S5_CHUNK - 1)) << shift) | (row >> shift)
    return jnp.where(col == swapped, 1.0, 0.0).astype(BF16)


def _chunk_proj_kernel(a_ref, w_ref, o_ref, wbf_ref):
    w = _resident_weight(w_ref, wbf_ref, row_axis=0)
    u = jnp.dot(a_ref[...], w[...], preferred_element_type=F32).astype(BF16)
    u_steps = jnp.dot(_step_major_perm(u.shape[0]), u, preferred_element_type=F32).astype(o_ref.dtype)
    nj = o_ref.shape[1]
    for s in range(o_ref.shape[0]):
        o_ref[s] = u_steps[s * nj:(s + 1) * nj, :]


def _chunk_proj(h, wt, layer, row0, n):
    t, k = h.shape
    nj = t // S5_CHUNK
    tm = S5_CHUNK * S5_CHUNK
    assert row0 % SUBLANES == 0
    return pl.pallas_call(
        _chunk_proj_kernel,
        grid=(t // tm,),
        in_specs=[pl.BlockSpec((tm, k), lambda i: (i, 0)),
                  _weight_rows_spec(layer, row0, n, k, lambda i: 0)],
        out_specs=pl.BlockSpec((S5_CHUNK, S5_CHUNK, n), lambda i: (0, i, 0)),
        out_shape=jax.ShapeDtypeStruct((S5_CHUNK, nj, n), BF16),
        scratch_shapes=[pltpu.VMEM((k, n), BF16)],
        compiler_params=_cparams(("arbitrary",)),
        name="proj_u",
    )(h, wt)


def _cumsum_rows(v):
    n = v.shape[0]
    idx = lax.broadcasted_iota(jnp.int32, v.shape, 0)
    k = 1
    while k < n:
        v = v + jnp.where(idx >= k, pltpu.roll(v, k, 0), 0.0)
        k *= 2
    return v


def _cumsum_lanes(v):
    n = v.shape[1]
    idx = lax.broadcasted_iota(jnp.int32, v.shape, 1)
    k = 1
    while k < n:
        v = v + jnp.where(idx >= k, pltpu.roll(v, k, 1), 0.0)
        k *= 2
    return v


def _ssd_kernel(xbc_ref, dt_ref, dtt_ref, alog_r_ref, alog_c_ref, dskip_ref, o_ref, state_ref, *,
                n_heads):
    q = dt_ref.shape[0]
    n = SSD_STATE
    p_dim = SSD_HEAD_DIM
    r_heads = n_heads // SSD_GROUPS
    gw = r_heads * p_dim
    inner = n_heads * p_dim

    @pl.when(pl.program_id(0) == 0)
    def _():
        state_ref[...] = jnp.zeros(state_ref.shape, F32)

    dtt = dtt_ref[...]
    cs_col = _cumsum_rows(dt_ref[...] * -jnp.exp(alog_r_ref[...]))
    cs_row = _cumsum_lanes(dtt * -jnp.exp(alog_c_ref[...]))
    causal = lax.broadcasted_iota(jnp.int32, (q, q), 0) >= lax.broadcasted_iota(jnp.int32, (q, q), 1)
    head_of_lane = lax.broadcasted_iota(jnp.int32, (1, gw), 1) // p_dim
    eye = jnp.where(lax.broadcasted_iota(jnp.int32, (n, n), 0) == lax.broadcasted_iota(jnp.int32, (n, n), 1),
                    1.0, 0.0).astype(BF16)
    nt = (((1,), (1,)), ((), ()))

    cbs, b_ts = [], []
    for g in range(SSD_GROUPS):
        b_g = xbc_ref[:, inner + g * n:inner + (g + 1) * n]
        c_g = xbc_ref[:, inner + (SSD_GROUPS + g) * n:inner + (SSD_GROUPS + g + 1) * n]
        cbs.append(lax.dot_general(c_g, b_g, nt, preferred_element_type=F32))
        b_ts.append(lax.dot_general(eye, b_g, nt, preferred_element_type=F32))

    def operands(g):
        c_g = xbc_ref[:, inner + (SSD_GROUPS + g) * n:inner + (SSD_GROUPS + g + 1) * n]
        cb, b_t = cbs[g], b_ts[g]
        c_f = c_g.astype(F32)
        lhs_parts, bt_parts, cd = [], [], jnp.zeros((1, gw), F32)
        for r in range(r_heads):
            h = g * r_heads + r
            csb = jnp.broadcast_to(cs_col[:, h:h + 1], (q, n))
            csr = cs_row[h:h + 1, :]
            dtr = dtt[h:h + 1, :]
            cs_last = csr[:, q - 1:q]
            decay = jnp.exp(jnp.where(causal, csb - csr, -1e30))
            lhs_parts.append(jnp.concatenate([cb * decay * dtr, c_f * jnp.exp(csb)], axis=1).astype(BF16))
            bt_parts.append((b_t * (dtr * jnp.exp(cs_last - csr))).astype(BF16))
            cd = jnp.where(head_of_lane == r, jnp.exp(cs_last), cd)
        return lhs_parts, bt_parts, cd

    for g in range(SSD_GROUPS):
        lhs_parts, bt_parts, cd = operands(g)
        x_g = xbc_ref[:, g * gw:(g + 1) * gw]
        s_g = state_ref[g]
        rhs = jnp.concatenate([x_g, s_g.astype(BF16)], axis=0)
        per_half = LANES // p_dim
        y_halves, s_halves = [], []
        for half in range(gw // LANES):
            heads = range(half * per_half, (half + 1) * per_half)
            cols = slice(half * LANES, (half + 1) * LANES)
            y_all = jnp.dot(jnp.concatenate([lhs_parts[r] for r in heads], axis=0), rhs[:, cols],
                            preferred_element_type=F32)
            s_all = jnp.dot(jnp.concatenate([bt_parts[r] for r in heads], axis=0), x_g[:, cols],
                            preferred_element_type=F32)
            lane_head = head_of_lane[:, cols]
            y_h, s_h = y_all[:q, :], s_all[:n, :]
            for j, r in enumerate(heads):
                if j:
                    y_h = jnp.where(lane_head == r, y_all[j * q:(j + 1) * q, :], y_h)
                    s_h = jnp.where(lane_head == r, s_all[j * n:(j + 1) * n, :], s_h)
            y_halves.append(y_h)
            s_halves.append(s_h)
        state_ref[g] = s_g * cd + jnp.concatenate(s_halves, axis=1)
        y_g = jnp.concatenate(y_halves, axis=1) + x_g.astype(F32) * dskip_ref[:, g * gw:(g + 1) * gw]
        o_ref[:, g * gw:(g + 1) * gw] = y_g.astype(o_ref.dtype)


def _ssd(xbc, dt, dtt, a_log, d_skip, n_heads, q):
    t, width = xbc.shape
    inner = n_heads * SSD_HEAD_DIM
    gw = inner // SSD_GROUPS
    assert n_heads % SUBLANES == 0 and n_heads <= LANES
    alog_r = jnp.zeros((1, LANES), F32).at[0, :n_heads].set(a_log)
    alog_c = a_log.reshape(n_heads, 1)
    dskip = jnp.repeat(d_skip, SSD_HEAD_DIM).reshape(1, inner)
    return pl.pallas_call(
        functools.partial(_ssd_kernel, n_heads=n_heads),
        grid=(t // q,),
        in_specs=[pl.BlockSpec((q, width), lambda c: (c, 0)),
                  pl.BlockSpec((q, LANES), lambda c: (c, 0)),
                  pl.BlockSpec((n_heads, q), lambda c: (0, c)),
                  pl.BlockSpec((1, LANES), lambda c: (0, 0)),
                  pl.BlockSpec((n_heads, 1), lambda c: (0, 0)),
                  pl.BlockSpec((1, inner), lambda c: (0, 0))],
        out_specs=pl.BlockSpec((q, inner), lambda c: (c, 0)),
        out_shape=jax.ShapeDtypeStruct((t, inner), BF16),
        scratch_shapes=[pltpu.VMEM((SSD_GROUPS, SSD_STATE, gw), F32)],
        compiler_params=_cparams(("arbitrary",)),
        name="ssd_scan",
    )(xbc, dt, dtt, alog_r, alog_c, dskip)


def _gated_up_kernel(y_ref, z_ref, nw_ref, w_ref, o_ref):
    v = y_ref[...].astype(F32) * z_ref[...].astype(F32)
    na = _rms(v, nw_ref[...]).astype(BF16)
    o_ref[...] = jnp.dot(na, w_ref[...], preferred_element_type=F32).astype(o_ref.dtype)


def _gated_up(y, zs, norm_w, w, tm):
    t, d = y.shape
    n = w.shape[1]
    return pl.pallas_call(
        _gated_up_kernel,
        grid=(t // tm,),
        in_specs=[pl.BlockSpec((tm, d), lambda i: (i, 0)),
                  pl.BlockSpec((tm, d), lambda i: (i, 0)),
                  pl.BlockSpec((1, d), lambda i: (0, 0)),
                  pl.BlockSpec((d, n), lambda i: (0, 0))],
        out_specs=pl.BlockSpec((tm, n), lambda i: (i, 0)),
        out_shape=jax.ShapeDtypeStruct((t, n), BF16),
        compiler_params=_cparams(("parallel",)),
        name="ssd_gated_up",
    )(y, zs, norm_w.reshape(1, d), w)


def _s5_operators(lam_re, lam_im, log_dt, b_re, b_im, c_re, c_im):
    ng, ns = lam_re.shape
    nc = S5_GROUP_CH
    L = S5_CHUNK
    per = S5_SUPER // nc
    nsg = ng // per
    lr, li = lam_re.astype(F32), lam_im.astype(F32)
    dt = jnp.exp(log_dt.astype(F32))[:, None]
    mag = jnp.exp(lr * dt)
    ang = li * dt
    abar_r, abar_i = mag * jnp.cos(ang), mag * jnp.sin(ang)
    den = lr * lr + li * li
    nr, ni = abar_r - 1.0, abar_i
    coef_r = (nr * lr + ni * li) / den
    coef_i = (ni * lr - nr * li) / den
    bre, bim = b_re.astype(F32), b_im.astype(F32)
    bb_r = coef_r[..., None] * bre - coef_i[..., None] * bim
    bb_i = coef_r[..., None] * bim + coef_i[..., None] * bre
    cre, cim = c_re.astype(F32), c_im.astype(F32)
    ks = jnp.arange(L + 1, dtype=F32)[:, None, None]
    pmag = jnp.exp(ks * (lr * dt)[None])
    pang = ks * ang[None]
    pw_r, pw_i = pmag * jnp.cos(pang), pmag * jnp.sin(pang)
    ca_r = cre[None] * pw_r[:, :, None, :] - cim[None] * pw_i[:, :, None, :]
    ca_i = cre[None] * pw_i[:, :, None, :] + cim[None] * pw_r[:, :, None, :]
    def _rows_n(v, steps):
        return v.reshape(steps, nsg, per, nc, ns).transpose(1, 4, 0, 2, 3).reshape(nsg, ns, steps * per * nc)

    uc = jnp.concatenate([_rows_n(ca_r, L + 1), -_rows_n(ca_i, L + 1)], axis=1)
    ks_rev = (L - 1) - jnp.arange(L, dtype=F32)[:, None, None]
    rmag = jnp.exp(ks_rev * (lr * dt)[None])
    rang = ks_rev * ang[None]
    rev_r, rev_i = rmag * jnp.cos(rang), rmag * jnp.sin(rang)
    ab_r = rev_r[..., None] * bb_r[None] - rev_i[..., None] * bb_i[None]
    ab_i = rev_r[..., None] * bb_i[None] + rev_i[..., None] * bb_r[None]
    ab_rt, ab_it = ab_r.transpose(0, 1, 3, 2), ab_i.transpose(0, 1, 3, 2)
    wc = jnp.concatenate([_rows_n(ab_rt, L), _rows_n(ab_it, L)], axis=1)

    def _rows_gc(v):
        return v.reshape(nsg, per, ns, nc).transpose(0, 1, 3, 2).reshape(nsg, per * nc, ns)

    bbt = jnp.concatenate([_rows_gc(bb_r), _rows_gc(bb_i)], axis=2)
    a_chunk = jnp.concatenate([pw_r[L].reshape(nsg, 1, per * ns),
                               pw_i[L].reshape(nsg, 1, per * ns)], axis=2)
    return uc, wc, bbt, a_chunk


def _split_bf16(v):
    hi = v.astype(BF16)
    return hi, (v - hi.astype(F32)).astype(BF16)


def _s5_kernel(u_ref, uc_ref, wc_ref, bbt_ref, ach_ref, dsk_ref, o_ref,
               toep_ref, wt_ref, v_ref, x_ref, sp_ref, carry_ref):
    jb = pl.program_id(1)
    nl, nj, cw = u_ref.shape
    half = carry_ref.shape[1] // 2
    per = cw // S5_GROUP_CH
    ns = half // per
    nt = (((1,), (1,)), ((), ()))

    @pl.when(jb == 0)
    def _():
        carry_ref[...] = jnp.zeros(carry_ref.shape, F32)
        uc = uc_ref[...]
        wc = wc_ref[...]
        b_hi, b_lo = _split_bf16(bbt_ref[...])
        u_hi, u_lo = _split_bf16(uc[:, :nl * cw])
        kall = (jnp.dot(b_hi, u_hi, preferred_element_type=F32)
                + jnp.dot(b_lo, u_hi, preferred_element_type=F32)
                + jnp.dot(b_hi, u_lo, preferred_element_type=F32))
        row_g = lax.broadcasted_iota(jnp.int32, (cw, 1), 0) // S5_GROUP_CH
        col_g = (lax.broadcasted_iota(jnp.int32, (1, nl * cw), 1) // S5_GROUP_CH) % per
        kall = jnp.where(row_g == col_g, kall, 0.0).astype(BF16)
        toep_ref[...] = jnp.zeros(toep_ref.shape, toep_ref.dtype)
        for s_in in range(nl):
            for s_out in range(s_in, nl):
                k = s_out - s_in
                toep_ref[pl.ds(s_in * cw, cw), pl.ds(s_out * cw, cw)] = kall[:, k * cw:(k + 1) * cw]
        for gp in range(per):
            mine = col_g == gp
            for part in range(2):
                rows = pl.ds(part * half + gp * ns, ns)
                src = slice(part * ns, (part + 1) * ns)
                v_ref[rows, :] = jnp.where(mine, uc[src, cw:], 0.0).astype(BF16)
                wt_ref[rows, :] = jnp.where(mine, wc[src, :], 0.0).astype(BF16)

    lhs = jnp.concatenate([u_ref[s] for s in range(nl)], axis=1)
    x_ref[...] = lax.dot_general(lhs, wt_ref[...], nt, preferred_element_type=F32)

    pair = 2 * cw
    y_intra = []
    for tp in range(nl // 2):
        kdim = pair * (tp + 1)
        y_intra.append(jnp.dot(lhs[:, :kdim], toep_ref[pl.ds(0, kdim), pl.ds(tp * pair, pair)],
                               preferred_element_type=F32))

    a_re = ach_ref[:, :half]
    a_im = ach_ref[:, half:]
    s_re, s_im = carry_ref[:, :half], carry_ref[:, half:]
    for j in range(nj):
        sp_ref[j:j + 1, :half] = s_re
        sp_ref[j:j + 1, half:] = s_im
        xr = x_ref[j:j + 1, :half]
        xi = x_ref[j:j + 1, half:]
        s_re, s_im = a_re * s_re - a_im * s_im + xr, a_re * s_im + a_im * s_re + xi
    carry_ref[:, :half] = s_re
    carry_ref[:, half:] = s_im

    y_state = jnp.dot(sp_ref[...].astype(BF16), v_ref[...], preferred_element_type=F32)
    dsk = dsk_ref[...]
    for tp in range(nl // 2):
        y = y_intra[tp] + y_state[:, tp * pair:(tp + 1) * pair]
        for h in range(2):
            s = 2 * tp + h
            o_ref[s] = (y[:, h * cw:(h + 1) * cw] + dsk * u_ref[s].astype(F32)).astype(o_ref.dtype)


def _s5(u_steps, uc, wc, bbt, a_chunk, d_skip, tj):
    nl, nj, width = u_steps.shape
    nsg = width // S5_SUPER
    nstate = a_chunk.shape[2]
    rows = uc.shape[1]
    return pl.pallas_call(
        _s5_kernel,
        grid=(nsg, nj // tj),
        in_specs=[pl.BlockSpec((nl, tj, S5_SUPER), lambda g, j: (0, j, g)),
                  pl.BlockSpec((None, rows, (nl + 1) * S5_SUPER), lambda g, j: (g, 0, 0)),
                  pl.BlockSpec((None, rows, nl * S5_SUPER), lambda g, j: (g, 0, 0)),
                  pl.BlockSpec((None, S5_SUPER, rows), lambda g, j: (g, 0, 0)),
                  pl.BlockSpec((None, 1, nstate), lambda g, j: (g, 0, 0)),
                  pl.BlockSpec((None, 1, S5_SUPER), lambda g, j: (g, 0, 0))],
        out_specs=pl.BlockSpec((nl, tj, S5_SUPER), lambda g, j: (0, j, g)),
        out_shape=jax.ShapeDtypeStruct((nl, nj, width), BF16),
        scratch_shapes=[pltpu.VMEM((nl * S5_SUPER, nl * S5_SUPER), BF16),
                        pltpu.VMEM((nstate, nl * S5_SUPER), BF16),
                        pltpu.VMEM((nstate, nl * S5_SUPER), BF16),
                        pltpu.VMEM((tj, nstate), F32),
                        pltpu.VMEM((tj, nstate), F32),
                        pltpu.VMEM((1, nstate), F32)],
        compiler_params=_cparams(("parallel", "arbitrary")),
        name="s5_scan",
    )(u_steps, uc, wc, bbt, a_chunk, d_skip.reshape(nsg, 1, S5_SUPER))


def _glu_up_kernel(y_ref, wg_ref, wu_ref, o_ref):
    y_steps = jnp.concatenate([y_ref[s] for s in range(y_ref.shape[0])], axis=0)
    y = jnp.dot(_step_major_perm(y_steps.shape[0]), y_steps, preferred_element_type=F32)
    v = _gelu_tanh(y)
    gate = _sigmoid(jnp.dot(v.astype(BF16), wg_ref[...], preferred_element_type=F32))
    o_ref[...] = jnp.dot((v * gate).astype(BF16), wu_ref[...],
                         preferred_element_type=F32).astype(o_ref.dtype)


def _glu_up(y_steps, w_glu, w_up):
    nl, nj, width = y_steps.shape
    n = w_up.shape[1]
    tm = S5_CHUNK * S5_CHUNK
    return pl.pallas_call(
        _glu_up_kernel,
        grid=(nj // S5_CHUNK,),
        in_specs=[pl.BlockSpec((nl, S5_CHUNK, width), lambda i: (0, i, 0)),
                  pl.BlockSpec((width, width), lambda i: (0, 0)),
                  pl.BlockSpec((width, n), lambda i: (0, 0))],
        out_specs=pl.BlockSpec((tm, n), lambda i: (i, 0)),
        out_shape=jax.ShapeDtypeStruct((nj * nl, n), BF16),
        compiler_params=_cparams(("parallel",)),
        name="s5_glu_up",
    )(y_steps, w_glu, w_up)


def _merge_out_kernel(ga_ref, gb_ref, ya_ref, yb_ref, x_ref, w_ref, nw_ref, rw_ref, rb_ref,
                      x1_ref, h2_ref, rt_ref):
    merged = (ga_ref[...].astype(F32) * ya_ref[...].astype(F32)
              + gb_ref[...].astype(F32) * yb_ref[...].astype(F32))
    x1 = x_ref[...] + jnp.dot(merged.astype(BF16), w_ref[...], preferred_element_type=F32)
    x1_ref[...] = x1
    h2 = _rms(x1, nw_ref[...])
    h2_ref[...] = _pack_bf16_pairs(h2)
    lg = jnp.dot(h2.astype(BF16), rw_ref[...], preferred_element_type=F32)
    rt_ref[...] = _route_math(lg + rb_ref[...])


def _merge_out(gates, ya, yb, x, w_out, norm_w, r_w, r_b, tm):
    t, d = x.shape
    row = lambda i: (i, 0)
    full = lambda i: (0, 0)
    return pl.pallas_call(
        _merge_out_kernel,
        grid=(t // tm,),
        in_specs=[pl.BlockSpec((tm, d), row),
                  pl.BlockSpec((tm, d), lambda i: (i, 1)),
                  pl.BlockSpec((tm, d), row),
                  pl.BlockSpec((tm, d), row),
                  pl.BlockSpec((tm, d), row),
                  pl.BlockSpec((d, d), full),
                  pl.BlockSpec((1, d), full),
                  pl.BlockSpec((d, LANES), full),
                  pl.BlockSpec((1, LANES), full)],
        out_specs=[pl.BlockSpec((tm, d), row),
                   pl.BlockSpec((tm, d // 2), row),
                   pl.BlockSpec((tm, LANES), row)],
        out_shape=[jax.ShapeDtypeStruct((t, d), F32),
                   jax.ShapeDtypeStruct((t, d // 2), jnp.uint32),
                   jax.ShapeDtypeStruct((t, LANES), F32)],
        compiler_params=_cparams(("parallel",)),
        name="merge_out",
    )(gates, gates, ya, yb, x, w_out, norm_w.reshape(1, d), r_w, r_b)


ROUTE_ID_LANE = 0
ROUTE_W_LANE = TOP_K_INNER
ROUTE_EXPERT_LANE0 = SUBLANES


def _route_math(lg):
    lane = lax.broadcasted_iota(jnp.int32, lg.shape, 1)
    neg = -jnp.inf
    big = LANES
    is_g = lane < N_EXPERT_GROUPS
    gl = jnp.where(is_g, lg, neg)
    gmax = jnp.max(gl, axis=1, keepdims=True)
    grp = jnp.min(jnp.where(gl == gmax, lane, big), axis=1, keepdims=True)
    pg_sel = 1.0 / jnp.sum(jnp.where(is_g, jnp.exp(lg - gmax), 0.0), axis=1, keepdims=True)
    e_lo = ROUTE_EXPERT_LANE0 + grp * EXPERTS_PER_GROUP
    in_grp = (lane >= e_lo) & (lane < e_lo + EXPERTS_PER_GROUP)
    el = jnp.where(in_grp, lg, neg)
    v0 = jnp.max(el, axis=1, keepdims=True)
    i0 = jnp.min(jnp.where(el == v0, lane, big), axis=1, keepdims=True)
    el1 = jnp.where(lane == i0, neg, el)
    v1 = jnp.max(el1, axis=1, keepdims=True)
    i1 = jnp.min(jnp.where(el1 == v1, lane, big), axis=1, keepdims=True)
    e1w = jnp.exp(v1 - v0)
    w0 = pg_sel / (1.0 + e1w)
    w1 = pg_sel * e1w / (1.0 + e1w)
    return jnp.where(lane == ROUTE_ID_LANE, (i0 - ROUTE_EXPERT_LANE0).astype(F32),
           jnp.where(lane == ROUTE_ID_LANE + 1, (i1 - ROUTE_EXPERT_LANE0).astype(F32),
           jnp.where(lane == ROUTE_W_LANE, w0, jnp.where(lane == ROUTE_W_LANE + 1, w1, 0.0))))


def _dispatch_plan(expert_ids, bm):
    n_tokens = expert_ids.shape[0]
    n_assign = n_tokens * TOP_K_INNER
    eid = expert_ids.reshape(n_assign)
    experts = jnp.arange(N_EXPERTS, dtype=jnp.int32)
    onehot = (eid[:, None] == experts[None, :]).astype(jnp.int32)
    csum = jnp.cumsum(onehot, axis=0)
    counts = csum[-1]
    rank = jnp.sum(onehot * csum, axis=1) - 1
    padded = ((counts + bm - 1) // bm) * bm
    ends = jnp.cumsum(padded)
    starts = ends - padded
    dest = jnp.sum(onehot * starts[None, :], axis=1) + rank
    n_rows = n_assign + N_EXPERTS * bm
    nb = n_rows // bm
    block_start = jnp.arange(nb, dtype=jnp.int32) * bm
    block_expert = jnp.minimum(jnp.sum((ends[None, :] <= block_start[:, None]).astype(jnp.int32), axis=1),
                               N_EXPERTS - 1)
    n_used = (ends[-1] // bm).astype(jnp.int32).reshape(1)
    return (dest.reshape(n_tokens, TOP_K_INNER), block_expert, n_used,
            (starts + counts).astype(jnp.int32), (padded - counts).astype(jnp.int32), n_rows)


def _dispatch_kernel(pad_start_ref, pad_count_ref, nused_ref, dest_ref, h_ref, xs_hbm, stage, zrow, sem, zsem):
    i = pl.program_id(0)
    n_steps = pl.num_programs(0)
    tm = h_ref.shape[0]
    bm = zrow.shape[0]
    nb = xs_hbm.shape[0] // bm
    slot = i % 2

    def tile_wait(s):
        for _ in range(TOP_K_INNER):
            pltpu.make_async_copy(stage.at[s], xs_hbm.at[pl.ds(0, tm)], sem.at[s]).wait()

    @pl.when(i >= 2)
    def _():
        tile_wait(slot)

    _store_token_tiles(stage.at[slot], h_ref[...])

    for r in range(tm):
        for k in range(TOP_K_INNER):
            pltpu.make_async_copy(stage.at[slot, r], xs_hbm.at[dest_ref[k, r]], sem.at[slot]).start()

    @pl.when(i == n_steps - 1)
    def _():
        tile_wait(slot)

        @pl.when(n_steps >= 2)
        def _():
            tile_wait(1 - slot)

        zrow[...] = jnp.zeros(zrow.shape, zrow.dtype)

        def pad_copy(e, r):
            return pltpu.make_async_copy(zrow.at[0], xs_hbm.at[pad_start_ref[e] + r], zsem)

        def block_copy(b):
            return pltpu.make_async_copy(zrow, xs_hbm.at[pl.ds(b * bm, bm)], zsem)
        for e in range(N_EXPERTS):
            lax.fori_loop(0, pad_count_ref[e], lambda r, c, e=e: (pad_copy(e, r).start(), c)[1], 0)
        lax.fori_loop(nused_ref[0], nb, lambda b, c: (block_copy(b).start(), c)[1], 0)
        for e in range(N_EXPERTS):
            lax.fori_loop(0, pad_count_ref[e], lambda r, c, e=e: (pad_copy(e, r).wait(), c)[1], 0)
        lax.fori_loop(nused_ref[0], nb, lambda b, c: (block_copy(b).wait(), c)[1], 0)


def _dispatch(h2, dest, pad_start, pad_count, n_used, n_rows, tm, bm):
    t, width = h2.shape
    tile = (width // LANES, LANES)
    nt = t // tm
    dest_t = dest.reshape(nt, tm, TOP_K_INNER).transpose(0, 2, 1)
    grid_spec = pltpu.PrefetchScalarGridSpec(
        num_scalar_prefetch=3,
        grid=(nt,),
        in_specs=[pl.BlockSpec((None, TOP_K_INNER, tm), lambda i, *_: (i, 0, 0), memory_space=pltpu.SMEM),
                  pl.BlockSpec((tm, width), lambda i, *_: (i, 0))],
        out_specs=pl.BlockSpec(memory_space=pl.ANY),
        scratch_shapes=[pltpu.VMEM((2, tm) + tile, h2.dtype),
                        pltpu.VMEM((bm,) + tile, h2.dtype),
                        pltpu.SemaphoreType.DMA((2,)),
                        pltpu.SemaphoreType.DMA(())],
    )
    return pl.pallas_call(
        _dispatch_kernel,
        grid_spec=grid_spec,
        out_shape=jax.ShapeDtypeStruct((n_rows,) + tile, h2.dtype),
        compiler_params=_cparams(("arbitrary",)),
        name="moe_dispatch",
    )(pad_start, pad_count, n_used, dest_t, h2)


def _experts_kernel(bexp_ref, nused_ref, first_ref, next_ref, slot_ref, xs_ref, wg_hbm, wu_hbm, wd_hbm,
                    o_ref, wg_f, wu_f, wd_f, wg_s, wu_s, wd_s, sem):
    b = pl.program_id(0)
    n_used = nused_ref[0]

    def weight_copies(e, s):
        return (pltpu.make_async_copy(wg_hbm.at[e], wg_f.at[s], sem.at[s, 0]),
                pltpu.make_async_copy(wu_hbm.at[e], wu_f.at[s], sem.at[s, 1]),
                pltpu.make_async_copy(wd_hbm.at[e], wd_f.at[s], sem.at[s, 2]))

    @pl.when((b == 0) & (n_used > 0))
    def _():
        for c in weight_copies(bexp_ref[0], 0):
            c.start()

    @pl.when((b < n_used) & (first_ref[b] == 1))
    def _():
        s = slot_ref[b]
        for c in weight_copies(bexp_ref[b], s):
            c.wait()

        @pl.when(next_ref[b] >= 0)
        def _():
            for c in weight_copies(next_ref[b], 1 - s):
                c.start()

        wg_s[...] = wg_f[s].astype(BF16)
        wu_s[...] = wu_f[s].astype(BF16)
        wd_s[...] = wd_f[s].astype(BF16)

    @pl.when(b < n_used)
    def _():
        xb = _unpack_bf16_pairs(_load_token_tiles(xs_ref)).astype(BF16)
        hg = jnp.dot(xb, wg_s[...], preferred_element_type=F32)
        hu = jnp.dot(xb, wu_s[...], preferred_element_type=F32)
        act = (_silu(hg) * hu).astype(BF16)
        _store_token_tiles(o_ref, _pack_bf16_pairs(jnp.dot(act, wd_s[...], preferred_element_type=F32)))

    @pl.when(b >= n_used)
    def _():
        o_ref[...] = jnp.zeros(o_ref.shape, o_ref.dtype)


def _experts(xs, block_expert, n_used, w_g, w_u, w_d, bm):
    n_rows = xs.shape[0]
    tile = xs.shape[1:]
    d, ff = w_g.shape[1], w_g.shape[2]
    assert d == 2 * tile[0] * tile[1]
    nb = n_rows // bm
    blk = jnp.arange(nb, dtype=jnp.int32)
    valid = blk < n_used[0]
    first = (valid & ((blk == 0) | (block_expert != jnp.roll(block_expert, 1)))).astype(jnp.int32)
    later = valid[None, :] & (block_expert[None, :] > block_expert[:, None])
    nxt = jnp.min(jnp.where(later, block_expert[None, :], N_EXPERTS), axis=1)
    nxt = jnp.where(nxt == N_EXPERTS, -1, nxt).astype(jnp.int32)
    slot = ((jnp.cumsum(first) - 1) % 2).astype(jnp.int32)

    def used(b, nu):
        return jnp.minimum(b, jnp.maximum(nu[0] - 1, 0))

    grid_spec = pltpu.PrefetchScalarGridSpec(
        num_scalar_prefetch=5,
        grid=(nb,),
        in_specs=[pl.BlockSpec((bm,) + tile, lambda b, be, nu, *_: (used(b, nu), 0, 0)),
                  pl.BlockSpec(memory_space=pl.ANY),
                  pl.BlockSpec(memory_space=pl.ANY),
                  pl.BlockSpec(memory_space=pl.ANY)],
        out_specs=pl.BlockSpec((bm,) + tile, lambda b, *_: (b, 0, 0)),
        scratch_shapes=[pltpu.VMEM((2, d, ff), w_g.dtype),
                        pltpu.VMEM((2, d, ff), w_u.dtype),
                        pltpu.VMEM((2, ff, d), w_d.dtype),
                        pltpu.VMEM((d, ff), BF16),
                        pltpu.VMEM((d, ff), BF16),
                        pltpu.VMEM((ff, d), BF16),
                        pltpu.SemaphoreType.DMA((2, 3))],
    )
    return pl.pallas_call(
        _experts_kernel,
        grid_spec=grid_spec,
        out_shape=jax.ShapeDtypeStruct((n_rows,) + tile, xs.dtype),
        compiler_params=_cparams(("arbitrary",)),
        name="moe_experts",
    )(block_expert, n_used, first, nxt, slot, xs, w_g, w_u, w_d)


def _combine_kernel(pos_ref, x1_ref, rt_ref, nw_ref, y_hbm, o_ref, ybuf, sem, *, normalize):
    i = pl.program_id(0)
    n = pl.num_programs(0)
    tm = x1_ref.shape[0]

    def start_tile(slot, which):
        for r in range(tm):
            for k in range(TOP_K_INNER):
                pltpu.make_async_copy(y_hbm.at[pos_ref[which, k, r]], ybuf.at[slot, k, r],
                                      sem.at[slot]).start()

    def wait_tile(slot):
        for k in range(TOP_K_INNER):
            pltpu.make_async_copy(y_hbm.at[pl.ds(0, tm)], ybuf.at[slot, k], sem.at[slot]).wait()

    slot = i % 2

    @pl.when(i == 0)
    def _():
        start_tile(0, 0)

    wait_tile(slot)
    start_tile(1 - slot, 1)
    acc = x1_ref[...]
    for k in range(TOP_K_INNER):
        y_k = _unpack_bf16_pairs(_load_token_tiles(ybuf.at[slot, k]))
        acc = acc + rt_ref[:, ROUTE_W_LANE + k:ROUTE_W_LANE + k + 1] * y_k
    o_ref[...] = _rms(acc, nw_ref[...]) if normalize else acc

    @pl.when(i == n - 1)
    def _():
        wait_tile(1 - slot)


def _combine(x1, y_rows, pos, route, norm_w, tm, normalize):
    t, d = x1.shape
    nt = t // tm
    pos_t = pos.reshape(nt, tm, TOP_K_INNER).transpose(0, 2, 1)
    pos_next = jnp.concatenate([pos_t[1:], pos_t[-1:]], axis=0)
    pos2 = jnp.stack([pos_t, pos_next], axis=1)
    return pl.pallas_call(
        functools.partial(_combine_kernel, normalize=normalize),
        grid=(nt,),
        in_specs=[pl.BlockSpec((None, 2, TOP_K_INNER, tm), lambda i: (i, 0, 0, 0), memory_space=pltpu.SMEM),
                  pl.BlockSpec((tm, d), lambda i: (i, 0)),
                  pl.BlockSpec((tm, LANES), lambda i: (i, 0)),
                  pl.BlockSpec((1, d), lambda i: (0, 0)),
                  pl.BlockSpec(memory_space=pl.ANY)],
        out_specs=pl.BlockSpec((tm, d), lambda i: (i, 0)),
        out_shape=jax.ShapeDtypeStruct((t, d), F32),
        scratch_shapes=[pltpu.VMEM((2, TOP_K_INNER, tm) + y_rows.shape[1:], y_rows.dtype),
                        pltpu.SemaphoreType.DMA((2,))],
        compiler_params=_cparams(("arbitrary",)),
        name="moe_combine",
    )(pos2, x1, route, norm_w.reshape(1, d), y_rows)


def _layer(x, p, w_in_t, layer):
    t, d = x.shape
    inner = p["w_a_up"].shape[0]
    n_heads = p["a_log"].shape[0]
    s5_width = p["w_glu"].shape[0]
    xbc_dim = inner + 2 * SSD_GROUPS * SSD_STATE
    sizes = (inner, xbc_dim, n_heads, s5_width, 2 * d)
    offs = [0]
    for s in sizes:
        offs.append(offs[-1] + s)
    assert n_heads <= LANES and offs[2] + LANES <= offs[5]
    dt_b = jnp.zeros((1, LANES), F32).at[0, :n_heads].set(p["dt_bias"].astype(F32))

    tm = min(ROW_TILE, t)
    tmm = min(MM_ROW_TILE, t)

    h, dt, dtt = _norm_dt(x, p["norm_mix_w"], w_in_t, layer, offs[2], dt_b, tm)
    zs = _proj(h, w_in_t, layer, offs[0], inner, None, "silu", tmm, MM_COL_TILE)
    xbc = _conv_proj(h, w_in_t, layer, offs[1], xbc_dim, p["conv_w"], p["conv_b"], tmm, MM_COL_TILE)
    u_steps = _chunk_proj(h, w_in_t, layer, offs[3], s5_width)
    gates = _proj(h, w_in_t, layer, offs[4], 2 * d, p["gate_b"], "sigmoid_bias", tmm, MM_COL_TILE)

    y = _ssd(xbc, dt, dtt[:n_heads], p["a_log"].astype(F32), p["d_ssd"].astype(F32), n_heads,
             min(SSD_CHUNK, t))
    ya = _gated_up(y, zs, p["norm_ssd_w"], p["w_a_up"].astype(BF16), tm)

    uc, wc, bbt, a_chunk = _s5_operators(p["s5_lambda_re"], p["s5_lambda_im"], p["s5_log_dt"],
                                         p["s5_b_re"], p["s5_b_im"], p["s5_c_re"], p["s5_c_im"])
    y5 = _s5(u_steps, uc, wc, bbt, a_chunk, p["s5_d"].astype(F32), min(S5_ROWS, t // S5_CHUNK))
    yb = _glu_up(y5, p["w_glu"].astype(BF16), p["w_b_up"].astype(BF16))

    w_router = jnp.zeros((d, LANES), F32)
    w_router = w_router.at[:, :N_EXPERT_GROUPS].set(p["w_route_group"].astype(F32))
    w_router = w_router.at[:, ROUTE_EXPERT_LANE0:ROUTE_EXPERT_LANE0 + N_EXPERTS].set(
        p["w_route_expert"].astype(F32))
    r_b = jnp.zeros((1, LANES), F32)
    r_b = r_b.at[0, :N_EXPERT_GROUPS].set(p["b_route_group"].astype(F32))
    r_b = r_b.at[0, ROUTE_EXPERT_LANE0:ROUTE_EXPERT_LANE0 + N_EXPERTS].set(p["b_route_expert"].astype(F32))
    x1, h2, route = _merge_out(gates, ya, yb, x, p["w_out"].astype(BF16), p["norm_ffn_w"],
                               w_router.astype(BF16), r_b, tm)

    expert_ids = route[:, ROUTE_ID_LANE:ROUTE_ID_LANE + TOP_K_INNER].astype(jnp.int32)
    bm = MOE_BLOCK
    tg = min(GATHER_TILE, t)
    pos, block_expert, n_used, pad_start, pad_count, n_rows = _dispatch_plan(expert_ids, bm)
    xs = _dispatch(h2, pos, pad_start, pad_count, n_used, n_rows, tg, bm)
    y_rows = _experts(xs, block_expert, n_used, p["w_exp_gate"], p["w_exp_up"], p["w_exp_down"], bm)
    return x1, y_rows, pos, route


def kernel(x, norm_mix_w, w_in, conv_w, conv_b, dt_bias, a_log, d_ssd, norm_ssd_w, w_a_up,
           s5_lambda_re, s5_lambda_im, s5_log_dt, s5_b_re, s5_b_im, s5_c_re, s5_c_im, s5_d,
           w_glu, w_b_up, gate_b, w_out, norm_ffn_w, w_route_group, b_route_group,
           w_route_expert, b_route_expert, w_exp_gate, w_exp_up, w_exp_down, norm_final_w):
    b, seq, d = x.shape
    assert b == 1, "the scans carry state along the flattened token axis"
    depth = w_in.shape[0]
    per_layer = dict(norm_mix_w=norm_mix_w, conv_w=conv_w, conv_b=conv_b, dt_bias=dt_bias,
                     a_log=a_log, d_ssd=d_ssd, norm_ssd_w=norm_ssd_w, w_a_up=w_a_up,
                     s5_lambda_re=s5_lambda_re, s5_lambda_im=s5_lambda_im, s5_log_dt=s5_log_dt,
                     s5_b_re=s5_b_re, s5_b_im=s5_b_im, s5_c_re=s5_c_re, s5_c_im=s5_c_im, s5_d=s5_d,
                     w_glu=w_glu, w_b_up=w_b_up, gate_b=gate_b, w_out=w_out, norm_ffn_w=norm_ffn_w,
                     w_route_group=w_route_group, b_route_group=b_route_group,
                     w_route_expert=w_route_expert, b_route_expert=b_route_expert,
                     w_exp_gate=w_exp_gate, w_exp_up=w_exp_up, w_exp_down=w_exp_down)
    xt = x.reshape(b * seq, d)
    tg = min(GATHER_TILE, b * seq)
    w_in_t = jnp.swapaxes(w_in.astype(F32), 1, 2)
    for i in range(depth):
        p = {k: v[i] for k, v in per_layer.items()}
        x1, y_rows, pos, route = _layer(xt, p, w_in_t, i)
        xt = _combine(x1, y_rows, pos, route, norm_final_w, tg, normalize=(i == depth - 1))
    return xt.reshape(b, seq, d)
```

```python
import functools
import math

import jax
import jax.numpy as jnp
from jax import lax
from jax.experimental import pallas as pl
from jax.experimental.pallas import tpu as pltpu

F32 = jnp.float32
BF16 = jnp.bfloat16

SSD_HEAD_DIM = 64
SSD_GROUPS = 8
SSD_STATE = 128
CONV_WIDTH = 4
S5_GROUP_CH = 16
S5_STATE = 64
N_EXPERT_GROUPS = 4
EXPERTS_PER_GROUP = 8
N_EXPERTS = N_EXPERT_GROUPS * EXPERTS_PER_GROUP
TOP_K_INNER = 2
RMS_EPS = 1e-6

LANES = 128
SUBLANES = 8
VMEM_LIMIT_BYTES = 52 * 1024 * 1024

ROW_TILE = 512
MM_ROW_TILE = 1024
MM_COL_TILE = 1024
SSD_CHUNK = 128
S5_CHUNK = 16
S5_SUPER = S5_GROUP_CH * 8
S5_ROWS = 256
MOE_BLOCK = 256
GATHER_TILE = 256


def _cparams(sem, vmem=VMEM_LIMIT_BYTES):
    return pltpu.CompilerParams(dimension_semantics=sem, vmem_limit_bytes=vmem)


def _sigmoid(v):
    return 1.0 / (1.0 + jnp.exp(-v))


def _silu(v):
    return v * _sigmoid(v)


def _softplus(v):
    return jnp.maximum(v, 0.0) + jnp.log(1.0 + jnp.exp(-jnp.abs(v)))


def _gelu_tanh(v):
    c = math.sqrt(2.0 / math.pi)
    return 0.5 * v * (1.0 + jnp.tanh(c * (v + 0.044715 * (v * v * v))))


def _rms(v, w):
    ms = jnp.mean(v * v, axis=-1, keepdims=True)
    return v * lax.rsqrt(ms + RMS_EPS) * w


def _pack_bf16_pairs(v):
    n = v.shape[1] // 2
    lo = pltpu.bitcast(v[:, :n].astype(BF16).astype(F32), jnp.uint32)
    hi = pltpu.bitcast(v[:, n:].astype(BF16).astype(F32), jnp.uint32)
    return hi | (lo >> 16)


def _unpack_bf16_pairs(w):
    lo = pltpu.bitcast(w << 16, F32)
    hi = pltpu.bitcast(w & jnp.uint32(0xFFFF0000), F32)
    return jnp.concatenate([lo, hi], axis=1)


def _store_token_tiles(ref, rows):
    for a in range(ref.shape[-2]):
        ref[:, a, :] = rows[:, a * LANES:(a + 1) * LANES]


def _load_token_tiles(ref):
    return jnp.concatenate([ref[:, a, :] for a in range(ref.shape[-2])], axis=1)


def _resident_weight(wt_ref, wbf_ref, row_axis=1):
    @pl.when(pl.program_id(row_axis) == 0)
    def _():
        wbf_ref[...] = wt_ref[...].T.astype(BF16)
    return wbf_ref


def _weight_rows_spec(layer, row0, n_rows, k, index_of_step):
    assert row0 % SUBLANES == 0 and n_rows % SUBLANES == 0
    return pl.BlockSpec((None, pl.Element(n_rows), pl.Element(k)),
                        lambda *idx: (layer, pl.multiple_of(row0 + n_rows * index_of_step(*idx), SUBLANES), 0))


def _proj_kernel(a_ref, wt_ref, b_ref, o_ref, wbf_ref, *, act):
    w = _resident_weight(wt_ref, wbf_ref)
    p = jnp.dot(a_ref[...], w[...], preferred_element_type=F32)
    if act == "silu":
        p = _silu(p)
    elif act == "sigmoid_bias":
        p = _sigmoid(p + b_ref[...])
    o_ref[...] = p.astype(o_ref.dtype)


def _proj(h, wt, layer, row0, n, b, act, tm, tn):
    t, k = h.shape
    tn = min(tn, n)
    assert n % tn == 0 and row0 % SUBLANES == 0
    if b is None:
        b = jnp.zeros((1, n), F32)
    return pl.pallas_call(
        functools.partial(_proj_kernel, act=act),
        grid=(n // tn, t // tm),
        in_specs=[pl.BlockSpec((tm, k), lambda j, i: (i, 0)),
                  _weight_rows_spec(layer, row0, tn, k, lambda j, i: j),
                  pl.BlockSpec((1, tn), lambda j, i: (0, j))],
        out_specs=pl.BlockSpec((tm, tn), lambda j, i: (i, j)),
        out_shape=jax.ShapeDtypeStruct((t, n), BF16),
        scratch_shapes=[pltpu.VMEM((k, tn), BF16)],
        compiler_params=_cparams(("parallel", "arbitrary")),
        name="proj_" + act,
    )(h, wt, b.reshape(1, n))


def _conv_proj_kernel(a_ref, w_ref, cw_ref, cb_ref, o_ref, ext_ref, wbf_ref):
    tm = a_ref.shape[0]
    halo = SUBLANES

    @pl.when(pl.program_id(1) == 0)
    def _():
        ext_ref[pl.ds(0, halo), :] = jnp.zeros((halo, ext_ref.shape[1]), F32)

    w = _resident_weight(w_ref, wbf_ref)
    p = jnp.dot(a_ref[...], w[...], preferred_element_type=F32)
    ext_ref[pl.ds(halo, tm), :] = p
    acc = cb_ref[...] + cw_ref[CONV_WIDTH - 1:CONV_WIDTH, :] * p
    for k in range(CONV_WIDTH - 1):
        back = CONV_WIDTH - 1 - k
        acc = acc + cw_ref[k:k + 1, :] * ext_ref[pl.ds(halo - back, tm), :]
    o_ref[...] = _silu(acc).astype(o_ref.dtype)
    ext_ref[pl.ds(0, halo), :] = p[tm - halo:, :]


def _conv_proj(h, wt, layer, row0, n, conv_w, conv_b, tm, tn):
    t, k = h.shape
    assert n % tn == 0 and row0 % SUBLANES == 0
    return pl.pallas_call(
        _conv_proj_kernel,
        grid=(n // tn, t // tm),
        in_specs=[pl.BlockSpec((tm, k), lambda j, i: (i, 0)),
                  _weight_rows_spec(layer, row0, tn, k, lambda j, i: j),
                  pl.BlockSpec((CONV_WIDTH, tn), lambda j, i: (0, j)),
                  pl.BlockSpec((1, tn), lambda j, i: (0, j))],
        out_specs=pl.BlockSpec((tm, tn), lambda j, i: (i, j)),
        out_shape=jax.ShapeDtypeStruct((t, n), BF16),
        scratch_shapes=[pltpu.VMEM((tm + SUBLANES, tn), F32),
                        pltpu.VMEM((k, tn), BF16)],
        compiler_params=_cparams(("parallel", "arbitrary")),
        name="proj_conv",
    )(h, wt, conv_w, conv_b.reshape(1, n))


def _norm_dt_kernel(x_ref, nw_ref, wt_ref, b_ref, h_ref, dt_ref, dtt_ref, wbf_ref):
    w = _resident_weight(wt_ref, wbf_ref, row_axis=0)
    h = _rms(x_ref[...], nw_ref[...]).astype(BF16)
    h_ref[...] = h
    p = jnp.dot(h, w[...], preferred_element_type=F32) + b_ref[...]
    dt = _softplus(p)
    dt_ref[...] = dt
    dtt_ref[...] = dt.T


def _norm_dt(x, norm_w, wt, layer, row0, b_pad, tm):
    t, k = x.shape
    return pl.pallas_call(
        _norm_dt_kernel,
        grid=(t // tm,),
        in_specs=[pl.BlockSpec((tm, k), lambda i: (i, 0)),
                  pl.BlockSpec((1, k), lambda i: (0, 0)),
                  _weight_rows_spec(layer, row0, LANES, k, lambda i: 0),
                  pl.BlockSpec((1, LANES), lambda i: (0, 0))],
        out_specs=[pl.BlockSpec((tm, k), lambda i: (i, 0)),
                   pl.BlockSpec((tm, LANES), lambda i: (i, 0)),
                   pl.BlockSpec((LANES, tm), lambda i: (0, i))],
        out_shape=[jax.ShapeDtypeStruct((t, k), BF16),
                   jax.ShapeDtypeStruct((t, LANES), F32),
                   jax.ShapeDtypeStruct((LANES, t), F32)],
        scratch_shapes=[pltpu.VMEM((k, LANES), BF16)],
        compiler_params=_cparams(("arbitrary",)),
        name="norm_dt",
    )(x, norm_w.reshape(1, k), wt, b_pad)


def _step_major_perm(n_rows):
    assert n_rows == S5_CHUNK * S5_CHUNK
    shift = S5_CHUNK.bit_length() - 1
    row = lax.broadcasted_iota(jnp.int32, (n_rows, n_rows), 0)
    col = lax.broadcasted_iota(jnp.int32, (n_rows, n_rows), 1)
    swapped = ((row & (S5_CHUNK - 1)) << shift) | (row >> shift)
    return jnp.where(col == swapped, 1.0, 0.0).astype(BF16)


S5_PERM_ROWS = S5_CHUNK * S5_CHUNK
S5_TILE_SUBS = 2


def _chunk_proj_kernel(a_ref, w_ref, o_ref, wbf_ref):
    w = _resident_weight(w_ref, wbf_ref, row_axis=0)
    u = jnp.dot(a_ref[...], w[...], preferred_element_type=F32).astype(BF16)
    perm = _step_major_perm(S5_PERM_ROWS)
    for sub in range(u.shape[0] // S5_PERM_ROWS):
        u_steps = jnp.dot(perm, u[sub * S5_PERM_ROWS:(sub + 1) * S5_PERM_ROWS, :],
                          preferred_element_type=F32).astype(o_ref.dtype)
        for s in range(o_ref.shape[0]):
            o_ref[s, sub * S5_CHUNK:(sub + 1) * S5_CHUNK, :] = u_steps[s * S5_CHUNK:(s + 1) * S5_CHUNK, :]


def _chunk_proj(h, wt, layer, row0, n):
    t, k = h.shape
    nj = t // S5_CHUNK
    subs = S5_TILE_SUBS if t % (S5_TILE_SUBS * S5_PERM_ROWS) == 0 else 1
    tm = subs * S5_PERM_ROWS
    assert row0 % SUBLANES == 0
    return pl.pallas_call(
        _chunk_proj_kernel,
        grid=(t // tm,),
        in_specs=[pl.BlockSpec((tm, k), lambda i: (i, 0)),
                  _weight_rows_spec(layer, row0, n, k, lambda i: 0)],
        out_specs=pl.BlockSpec((S5_CHUNK, subs * S5_CHUNK, n), lambda i: (0, i, 0)),
        out_shape=jax.ShapeDtypeStruct((S5_CHUNK, nj, n), BF16),
        scratch_shapes=[pltpu.VMEM((k, n), BF16)],
        compiler_params=_cparams(("arbitrary",)),
        name="proj_u",
    )(h, wt)


def _cumsum_rows(v):
    n = v.shape[0]
    idx = lax.broadcasted_iota(jnp.int32, v.shape, 0)
    k = 1
    while k < n:
        v = v + jnp.where(idx >= k, pltpu.roll(v, k, 0), 0.0)
        k *= 2
    return v


def _cumsum_lanes(v):
    n = v.shape[1]
    idx = lax.broadcasted_iota(jnp.int32, v.shape, 1)
    k = 1
    while k < n:
        v = v + jnp.where(idx >= k, pltpu.roll(v, k, 1), 0.0)
        k *= 2
    return v


def _ssd_kernel(xbc_ref, dt_ref, dtt_ref, alog_r_ref, alog_c_ref, dskip_ref, o_ref, state_ref, *,
                n_heads):
    q = dt_ref.shape[0]
    n = SSD_STATE
    p_dim = SSD_HEAD_DIM
    r_heads = n_heads // SSD_GROUPS
    gw = r_heads * p_dim
    inner = n_heads * p_dim

    @pl.when(pl.program_id(0) == 0)
    def _():
        state_ref[...] = jnp.zeros(state_ref.shape, F32)

    dtt = dtt_ref[...]
    cs_col = _cumsum_rows(dt_ref[...] * -jnp.exp(alog_r_ref[...]))
    cs_row = _cumsum_lanes(dtt * -jnp.exp(alog_c_ref[...]))
    causal = lax.broadcasted_iota(jnp.int32, (q, q), 0) >= lax.broadcasted_iota(jnp.int32, (q, q), 1)
    head_of_lane = lax.broadcasted_iota(jnp.int32, (1, gw), 1) // p_dim
    eye = jnp.where(lax.broadcasted_iota(jnp.int32, (n, n), 0) == lax.broadcasted_iota(jnp.int32, (n, n), 1),
                    1.0, 0.0).astype(BF16)
    nt = (((1,), (1,)), ((), ()))

    cbs, b_ts = [], []
    for g in range(SSD_GROUPS):
        b_g = xbc_ref[:, inner + g * n:inner + (g + 1) * n]
        c_g = xbc_ref[:, inner + (SSD_GROUPS + g) * n:inner + (SSD_GROUPS + g + 1) * n]
        cbs.append(lax.dot_general(c_g, b_g, nt, preferred_element_type=F32))
        b_ts.append(lax.dot_general(eye, b_g, nt, preferred_element_type=F32))

    def operands(g):
        c_g = xbc_ref[:, inner + (SSD_GROUPS + g) * n:inner + (SSD_GROUPS + g + 1) * n]
        cb, b_t = cbs[g], b_ts[g]
        c_f = c_g.astype(F32)
        lhs_parts, bt_parts, cd = [], [], jnp.zeros((1, gw), F32)
        for r in range(r_heads):
            h = g * r_heads + r
            csb = jnp.broadcast_to(cs_col[:, h:h + 1], (q, n))
            csr = cs_row[h:h + 1, :]
            dtr = dtt[h:h + 1, :]
            cs_last = csr[:, q - 1:q]
            decay = jnp.exp(jnp.where(causal, csb - csr, -1e30))
            lhs_parts.append(jnp.concatenate([cb * decay * dtr, c_f * jnp.exp(csb)], axis=1).astype(BF16))
            bt_parts.append((b_t * (dtr * jnp.exp(cs_last - csr))).astype(BF16))
            cd = jnp.where(head_of_lane == r, jnp.exp(cs_last), cd)
        return lhs_parts, bt_parts, cd

    for g in range(SSD_GROUPS):
        lhs_parts, bt_parts, cd = operands(g)
        x_g = xbc_ref[:, g * gw:(g + 1) * gw]
        s_g = state_ref[g]
        rhs = jnp.concatenate([x_g, s_g.astype(BF16)], axis=0)
        per_half = LANES // p_dim
        y_halves, s_halves = [], []
        for half in range(gw // LANES):
            heads = range(half * per_half, (half + 1) * per_half)
            cols = slice(half * LANES, (half + 1) * LANES)
            y_all = jnp.dot(jnp.concatenate([lhs_parts[r] for r in heads], axis=0), rhs[:, cols],
                            preferred_element_type=F32)
            s_all = jnp.dot(jnp.concatenate([bt_parts[r] for r in heads], axis=0), x_g[:, cols],
                            preferred_element_type=F32)
            lane_head = head_of_lane[:, cols]
            y_h, s_h = y_all[:q, :], s_all[:n, :]
            for j, r in enumerate(heads):
                if j:
                    y_h = jnp.where(lane_head == r, y_all[j * q:(j + 1) * q, :], y_h)
                    s_h = jnp.where(lane_head == r, s_all[j * n:(j + 1) * n, :], s_h)
            y_halves.append(y_h)
            s_halves.append(s_h)
        state_ref[g] = s_g * cd + jnp.concatenate(s_halves, axis=1)
        y_g = jnp.concatenate(y_halves, axis=1) + x_g.astype(F32) * dskip_ref[:, g * gw:(g + 1) * gw]
        o_ref[:, g * gw:(g + 1) * gw] = y_g.astype(o_ref.dtype)


def _ssd(xbc, dt, dtt, a_log, d_skip, n_heads, q):
    t, width = xbc.shape
    inner = n_heads * SSD_HEAD_DIM
    gw = inner // SSD_GROUPS
    assert n_heads % SUBLANES == 0 and n_heads <= LANES
    alog_r = jnp.zeros((1, LANES), F32).at[0, :n_heads].set(a_log)
    alog_c = a_log.reshape(n_heads, 1)
    dskip = jnp.repeat(d_skip, SSD_HEAD_DIM).reshape(1, inner)
    return pl.pallas_call(
        functools.partial(_ssd_kernel, n_heads=n_heads),
        grid=(t // q,),
        in_specs=[pl.BlockSpec((q, width), lambda c: (c, 0)),
                  pl.BlockSpec((q, LANES), lambda c: (c, 0)),
                  pl.BlockSpec((n_heads, q), lambda c: (0, c)),
                  pl.BlockSpec((1, LANES), lambda c: (0, 0)),
                  pl.BlockSpec((n_heads, 1), lambda c: (0, 0)),
                  pl.BlockSpec((1, inner), lambda c: (0, 0))],
        out_specs=pl.BlockSpec((q, inner), lambda c: (c, 0)),
        out_shape=jax.ShapeDtypeStruct((t, inner), BF16),
        scratch_shapes=[pltpu.VMEM((SSD_GROUPS, SSD_STATE, gw), F32)],
        compiler_params=_cparams(("arbitrary",)),
        name="ssd_scan",
    )(xbc, dt, dtt, alog_r, alog_c, dskip)


def _gated_up_kernel(y_ref, z_ref, nw_ref, w_ref, o_ref):
    v = y_ref[...].astype(F32) * z_ref[...].astype(F32)
    na = _rms(v, nw_ref[...]).astype(BF16)
    o_ref[...] = jnp.dot(na, w_ref[...], preferred_element_type=F32).astype(o_ref.dtype)


def _gated_up(y, zs, norm_w, w, tm):
    t, d = y.shape
    n = w.shape[1]
    return pl.pallas_call(
        _gated_up_kernel,
        grid=(t // tm,),
        in_specs=[pl.BlockSpec((tm, d), lambda i: (i, 0)),
                  pl.BlockSpec((tm, d), lambda i: (i, 0)),
                  pl.BlockSpec((1, d), lambda i: (0, 0)),
                  pl.BlockSpec((d, n), lambda i: (0, 0))],
        out_specs=pl.BlockSpec((tm, n), lambda i: (i, 0)),
        out_shape=jax.ShapeDtypeStruct((t, n), BF16),
        compiler_params=_cparams(("parallel",)),
        name="ssd_gated_up",
    )(y, zs, norm_w.reshape(1, d), w)


def _s5_operators(lam_re, lam_im, log_dt, b_re, b_im, c_re, c_im):
    ng, ns = lam_re.shape
    nc = S5_GROUP_CH
    L = S5_CHUNK
    per = S5_SUPER // nc
    nsg = ng // per
    lr, li = lam_re.astype(F32), lam_im.astype(F32)
    dt = jnp.exp(log_dt.astype(F32))[:, None]
    mag = jnp.exp(lr * dt)
    ang = li * dt
    abar_r, abar_i = mag * jnp.cos(ang), mag * jnp.sin(ang)
    den = lr * lr + li * li
    nr, ni = abar_r - 1.0, abar_i
    coef_r = (nr * lr + ni * li) / den
    coef_i = (ni * lr - nr * li) / den
    bre, bim = b_re.astype(F32), b_im.astype(F32)
    bb_r = coef_r[..., None] * bre - coef_i[..., None] * bim
    bb_i = coef_r[..., None] * bim + coef_i[..., None] * bre
    cre, cim = c_re.astype(F32), c_im.astype(F32)
    ks = jnp.arange(L + 1, dtype=F32)[:, None, None]
    pmag = jnp.exp(ks * (lr * dt)[None])
    pang = ks * ang[None]
    pw_r, pw_i = pmag * jnp.cos(pang), pmag * jnp.sin(pang)
    ca_r = cre[None] * pw_r[:, :, None, :] - cim[None] * pw_i[:, :, None, :]
    ca_i = cre[None] * pw_i[:, :, None, :] + cim[None] * pw_r[:, :, None, :]
    def _rows_n(v, steps):
        return v.reshape(steps, nsg, per, nc, ns).transpose(1, 4, 0, 2, 3).reshape(nsg, ns, steps * per * nc)

    uc = jnp.concatenate([_rows_n(ca_r, L + 1), -_rows_n(ca_i, L + 1)], axis=1)
    ks_rev = (L - 1) - jnp.arange(L, dtype=F32)[:, None, None]
    rmag = jnp.exp(ks_rev * (lr * dt)[None])
    rang = ks_rev * ang[None]
    rev_r, rev_i = rmag * jnp.cos(rang), rmag * jnp.sin(rang)
    ab_r = rev_r[..., None] * bb_r[None] - rev_i[..., None] * bb_i[None]
    ab_i = rev_r[..., None] * bb_i[None] + rev_i[..., None] * bb_r[None]
    ab_rt, ab_it = ab_r.transpose(0, 1, 3, 2), ab_i.transpose(0, 1, 3, 2)
    wc = jnp.concatenate([_rows_n(ab_rt, L), _rows_n(ab_it, L)], axis=1)

    def _rows_gc(v):
        return v.reshape(nsg, per, ns, nc).transpose(0, 1, 3, 2).reshape(nsg, per * nc, ns)

    bbt = jnp.concatenate([_rows_gc(bb_r), _rows_gc(bb_i)], axis=2)
    a_chunk = jnp.concatenate([pw_r[L].reshape(nsg, 1, per * ns),
                               pw_i[L].reshape(nsg, 1, per * ns)], axis=2)
    return uc, wc, bbt, a_chunk


def _split_bf16(v):
    hi = v.astype(BF16)
    return hi, (v - hi.astype(F32)).astype(BF16)


def _s5_kernel(u_ref, uc_ref, wc_ref, bbt_ref, ach_ref, dsk_ref, o_ref,
               toep_ref, wt_ref, v_ref, x_ref, sp_ref, carry_ref):
    jb = pl.program_id(1)
    nl, nj, cw = u_ref.shape
    half = carry_ref.shape[1] // 2
    per = cw // S5_GROUP_CH
    ns = half // per
    nt = (((1,), (1,)), ((), ()))

    @pl.when(jb == 0)
    def _():
        carry_ref[...] = jnp.zeros(carry_ref.shape, F32)
        uc = uc_ref[...]
        wc = wc_ref[...]
        b_hi, b_lo = _split_bf16(bbt_ref[...])
        u_hi, u_lo = _split_bf16(uc[:, :nl * cw])
        kall = (jnp.dot(b_hi, u_hi, preferred_element_type=F32)
                + jnp.dot(b_lo, u_hi, preferred_element_type=F32)
                + jnp.dot(b_hi, u_lo, preferred_element_type=F32))
        row_g = lax.broadcasted_iota(jnp.int32, (cw, 1), 0) // S5_GROUP_CH
        col_g = (lax.broadcasted_iota(jnp.int32, (1, nl * cw), 1) // S5_GROUP_CH) % per
        kall = jnp.where(row_g == col_g, kall, 0.0).astype(BF16)
        toep_ref[...] = jnp.zeros(toep_ref.shape, toep_ref.dtype)
        for s_in in range(nl):
            for s_out in range(s_in, nl):
                k = s_out - s_in
                toep_ref[pl.ds(s_in * cw, cw), pl.ds(s_out * cw, cw)] = kall[:, k * cw:(k + 1) * cw]
        for gp in range(per):
            mine = col_g == gp
            for part in range(2):
                rows = pl.ds(part * half + gp * ns, ns)
                src = slice(part * ns, (part + 1) * ns)
                v_ref[rows, :] = jnp.where(mine, uc[src, cw:], 0.0).astype(BF16)
                wt_ref[rows, :] = jnp.where(mine, wc[src, :], 0.0).astype(BF16)

    lhs = jnp.concatenate([u_ref[s] for s in range(nl)], axis=1)
    x_ref[...] = lax.dot_general(lhs, wt_ref[...], nt, preferred_element_type=F32)

    pair = 2 * cw
    y_intra = []
    for tp in range(nl // 2):
        kdim = pair * (tp + 1)
        y_intra.append(jnp.dot(lhs[:, :kdim], toep_ref[pl.ds(0, kdim), pl.ds(tp * pair, pair)],
                               preferred_element_type=F32))

    a_re = ach_ref[:, :half]
    a_im = ach_ref[:, half:]
    s_re, s_im = carry_ref[:, :half], carry_ref[:, half:]
    for j in range(nj):
        sp_ref[j:j + 1, :half] = s_re
        sp_ref[j:j + 1, half:] = s_im
        xr = x_ref[j:j + 1, :half]
        xi = x_ref[j:j + 1, half:]
        s_re, s_im = a_re * s_re - a_im * s_im + xr, a_re * s_im + a_im * s_re + xi
    carry_ref[:, :half] = s_re
    carry_ref[:, half:] = s_im

    y_state = jnp.dot(sp_ref[...].astype(BF16), v_ref[...], preferred_element_type=F32)
    dsk = dsk_ref[...]
    for tp in range(nl // 2):
        y = y_intra[tp] + y_state[:, tp * pair:(tp + 1) * pair]
        for h in range(2):
            s = 2 * tp + h
            o_ref[s] = (y[:, h * cw:(h + 1) * cw] + dsk * u_ref[s].astype(F32)).astype(o_ref.dtype)


def _s5(u_steps, uc, wc, bbt, a_chunk, d_skip, tj):
    nl, nj, width = u_steps.shape
    nsg = width // S5_SUPER
    nstate = a_chunk.shape[2]
    rows = uc.shape[1]
    return pl.pallas_call(
        _s5_kernel,
        grid=(nsg, nj // tj),
        in_specs=[pl.BlockSpec((nl, tj, S5_SUPER), lambda g, j: (0, j, g)),
                  pl.BlockSpec((None, rows, (nl + 1) * S5_SUPER), lambda g, j: (g, 0, 0)),
                  pl.BlockSpec((None, rows, nl * S5_SUPER), lambda g, j: (g, 0, 0)),
                  pl.BlockSpec((None, S5_SUPER, rows), lambda g, j: (g, 0, 0)),
                  pl.BlockSpec((None, 1, nstate), lambda g, j: (g, 0, 0)),
                  pl.BlockSpec((None, 1, S5_SUPER), lambda g, j: (g, 0, 0))],
        out_specs=pl.BlockSpec((nl, tj, S5_SUPER), lambda g, j: (0, j, g)),
        out_shape=jax.ShapeDtypeStruct((nl, nj, width), BF16),
        scratch_shapes=[pltpu.VMEM((nl * S5_SUPER, nl * S5_SUPER), BF16),
                        pltpu.VMEM((nstate, nl * S5_SUPER), BF16),
                        pltpu.VMEM((nstate, nl * S5_SUPER), BF16),
                        pltpu.VMEM((tj, nstate), F32),
                        pltpu.VMEM((tj, nstate), F32),
                        pltpu.VMEM((1, nstate), F32)],
        compiler_params=_cparams(("parallel", "arbitrary")),
        name="s5_scan",
    )(u_steps, uc, wc, bbt, a_chunk, d_skip.reshape(nsg, 1, S5_SUPER))


def _glu_up_kernel(y_ref, wg_ref, wu_ref, o_ref):
    perm = _step_major_perm(S5_PERM_ROWS)
    parts = []
    for sub in range(y_ref.shape[1] // S5_CHUNK):
        y_steps = jnp.concatenate([y_ref[s, sub * S5_CHUNK:(sub + 1) * S5_CHUNK, :]
                                   for s in range(y_ref.shape[0])], axis=0)
        parts.append(jnp.dot(perm, y_steps, preferred_element_type=F32))
    y = jnp.concatenate(parts, axis=0)
    v = _gelu_tanh(y)
    gate = _sigmoid(jnp.dot(v.astype(BF16), wg_ref[...], preferred_element_type=F32))
    o_ref[...] = jnp.dot((v * gate).astype(BF16), wu_ref[...],
                         preferred_element_type=F32).astype(o_ref.dtype)


def _glu_up(y_steps, w_glu, w_up):
    nl, nj, width = y_steps.shape
    n = w_up.shape[1]
    subs = S5_TILE_SUBS if nj % (S5_TILE_SUBS * S5_CHUNK) == 0 else 1
    tm = subs * S5_PERM_ROWS
    return pl.pallas_call(
        _glu_up_kernel,
        grid=(nj // (subs * S5_CHUNK),),
        in_specs=[pl.BlockSpec((nl, subs * S5_CHUNK, width), lambda i: (0, i, 0)),
                  pl.BlockSpec((width, width), lambda i: (0, 0)),
                  pl.BlockSpec((width, n), lambda i: (0, 0))],
        out_specs=pl.BlockSpec((tm, n), lambda i: (i, 0)),
        out_shape=jax.ShapeDtypeStruct((nj * nl, n), BF16),
        compiler_params=_cparams(("parallel",)),
        name="s5_glu_up",
    )(y_steps, w_glu, w_up)


def _merge_out_kernel(ga_ref, gb_ref, ya_ref, yb_ref, x_ref, w_ref, nw_ref, rw_ref, rb_ref,
                      x1_ref, h2_ref, rt_ref):
    merged = (ga_ref[...].astype(F32) * ya_ref[...].astype(F32)
              + gb_ref[...].astype(F32) * yb_ref[...].astype(F32))
    x1 = x_ref[...] + jnp.dot(merged.astype(BF16), w_ref[...], preferred_element_type=F32)
    x1_ref[...] = x1
    h2 = _rms(x1, nw_ref[...])
    h2_ref[...] = _pack_bf16_pairs(h2)
    lg = jnp.dot(h2.astype(BF16), rw_ref[...], preferred_element_type=F32)
    rt_ref[...] = _route_math(lg + rb_ref[...])


def _merge_out(gates, ya, yb, x, w_out, norm_w, r_w, r_b, tm):
    t, d = x.shape
    row = lambda i: (i, 0)
    full = lambda i: (0, 0)
    return pl.pallas_call(
        _merge_out_kernel,
        grid=(t // tm,),
        in_specs=[pl.BlockSpec((tm, d), row),
                  pl.BlockSpec((tm, d), lambda i: (i, 1)),
                  pl.BlockSpec((tm, d), row),
                  pl.BlockSpec((tm, d), row),
                  pl.BlockSpec((tm, d), row),
                  pl.BlockSpec((d, d), full),
                  pl.BlockSpec((1, d), full),
                  pl.BlockSpec((d, LANES), full),
                  pl.BlockSpec((1, LANES), full)],
        out_specs=[pl.BlockSpec((tm, d), row),
                   pl.BlockSpec((tm, d // 2), row),
                   pl.BlockSpec((tm, LANES), row)],
        out_shape=[jax.ShapeDtypeStruct((t, d), F32),
                   jax.ShapeDtypeStruct((t, d // 2), jnp.uint32),
                   jax.ShapeDtypeStruct((t, LANES), F32)],
        compiler_params=_cparams(("parallel",)),
        name="merge_out",
    )(gates, gates, ya, yb, x, w_out, norm_w.reshape(1, d), r_w, r_b)


ROUTE_ID_LANE = 0
ROUTE_W_LANE = TOP_K_INNER
ROUTE_EXPERT_LANE0 = SUBLANES


def _route_math(lg):
    lane = lax.broadcasted_iota(jnp.int32, lg.shape, 1)
    neg = -jnp.inf
    big = LANES
    is_g = lane < N_EXPERT_GROUPS
    gl = jnp.where(is_g, lg, neg)
    gmax = jnp.max(gl, axis=1, keepdims=True)
    grp = jnp.min(jnp.where(gl == gmax, lane, big), axis=1, keepdims=True)
    pg_sel = 1.0 / jnp.sum(jnp.where(is_g, jnp.exp(lg - gmax), 0.0), axis=1, keepdims=True)
    e_lo = ROUTE_EXPERT_LANE0 + grp * EXPERTS_PER_GROUP
    in_grp = (lane >= e_lo) & (lane < e_lo + EXPERTS_PER_GROUP)
    el = jnp.where(in_grp, lg, neg)
    v0 = jnp.max(el, axis=1, keepdims=True)
    i0 = jnp.min(jnp.where(el == v0, lane, big), axis=1, keepdims=True)
    el1 = jnp.where(lane == i0, neg, el)
    v1 = jnp.max(el1, axis=1, keepdims=True)
    i1 = jnp.min(jnp.where(el1 == v1, lane, big), axis=1, keepdims=True)
    e1w = jnp.exp(v1 - v0)
    w0 = pg_sel / (1.0 + e1w)
    w1 = pg_sel * e1w / (1.0 + e1w)
    return jnp.where(lane == ROUTE_ID_LANE, (i0 - ROUTE_EXPERT_LANE0).astype(F32),
           jnp.where(lane == ROUTE_ID_LANE + 1, (i1 - ROUTE_EXPERT_LANE0).astype(F32),
           jnp.where(lane == ROUTE_W_LANE, w0, jnp.where(lane == ROUTE_W_LANE + 1, w1, 0.0))))


def _dispatch_plan(expert_ids, bm):
    n_tokens = expert_ids.shape[0]
    n_assign = n_tokens * TOP_K_INNER
    eid = expert_ids.reshape(n_assign)
    experts = jnp.arange(N_EXPERTS, dtype=jnp.int32)
    onehot = (eid[:, None] == experts[None, :]).astype(jnp.int32)
    csum = jnp.cumsum(onehot, axis=0)
    counts = csum[-1]
    rank = jnp.sum(onehot * csum, axis=1) - 1
    padded = ((counts + bm - 1) // bm) * bm
    ends = jnp.cumsum(padded)
    starts = ends - padded
    dest = jnp.sum(onehot * starts[None, :], axis=1) + rank
    n_rows = n_assign + N_EXPERTS * bm
    nb = n_rows // bm
    block_start = jnp.arange(nb, dtype=jnp.int32) * bm
    block_expert = jnp.minimum(jnp.sum((ends[None, :] <= block_start[:, None]).astype(jnp.int32), axis=1),
                               N_EXPERTS - 1)
    n_used = (ends[-1] // bm).astype(jnp.int32).reshape(1)
    return (dest.reshape(n_tokens, TOP_K_INNER), block_expert, n_used,
            (starts + counts).astype(jnp.int32), (padded - counts).astype(jnp.int32), n_rows)


def _dispatch_kernel(pad_start_ref, pad_count_ref, nused_ref, dest_ref, h_ref, xs_hbm, stage, zrow, sem, zsem):
    i = pl.program_id(0)
    n_steps = pl.num_programs(0)
    tm = h_ref.shape[0]
    bm = zrow.shape[0]
    nb = xs_hbm.shape[0] // bm
    slot = i % 2

    def tile_wait(s):
        for _ in range(TOP_K_INNER):
            pltpu.make_async_copy(stage.at[s], xs_hbm.at[pl.ds(0, tm)], sem.at[s]).wait()

    @pl.when(i >= 2)
    def _():
        tile_wait(slot)

    _store_token_tiles(stage.at[slot], h_ref[...])

    for r in range(tm):
        for k in range(TOP_K_INNER):
            pltpu.make_async_copy(stage.at[slot, r], xs_hbm.at[dest_ref[k, r]], sem.at[slot]).start()

    @pl.when(i == n_steps - 1)
    def _():
        tile_wait(slot)

        @pl.when(n_steps >= 2)
        def _():
            tile_wait(1 - slot)

        zrow[...] = jnp.zeros(zrow.shape, zrow.dtype)

        def pad_copy(e, r):
            return pltpu.make_async_copy(zrow.at[0], xs_hbm.at[pad_start_ref[e] + r], zsem)

        def block_copy(b):
            return pltpu.make_async_copy(zrow, xs_hbm.at[pl.ds(b * bm, bm)], zsem)
        for e in range(N_EXPERTS):
            lax.fori_loop(0, pad_count_ref[e], lambda r, c, e=e: (pad_copy(e, r).start(), c)[1], 0)
        lax.fori_loop(nused_ref[0], nb, lambda b, c: (block_copy(b).start(), c)[1], 0)
        for e in range(N_EXPERTS):
            lax.fori_loop(0, pad_count_ref[e], lambda r, c, e=e: (pad_copy(e, r).wait(), c)[1], 0)
        lax.fori_loop(nused_ref[0], nb, lambda b, c: (block_copy(b).wait(), c)[1], 0)


def _dispatch(h2, dest, pad_start, pad_count, n_used, n_rows, tm, bm):
    t, width = h2.shape
    tile = (width // LANES, LANES)
    nt = t // tm
    dest_t = dest.reshape(nt, tm, TOP_K_INNER).transpose(0, 2, 1)
    grid_spec = pltpu.PrefetchScalarGridSpec(
        num_scalar_prefetch=3,
        grid=(nt,),
        in_specs=[pl.BlockSpec((None, TOP_K_INNER, tm), lambda i, *_: (i, 0, 0), memory_space=pltpu.SMEM),
                  pl.BlockSpec((tm, width), lambda i, *_: (i, 0))],
        out_specs=pl.BlockSpec(memory_space=pl.ANY),
        scratch_shapes=[pltpu.VMEM((2, tm) + tile, h2.dtype),
                        pltpu.VMEM((bm,) + tile, h2.dtype),
                        pltpu.SemaphoreType.DMA((2,)),
                        pltpu.SemaphoreType.DMA(())],
    )
    return pl.pallas_call(
        _dispatch_kernel,
        grid_spec=grid_spec,
        out_shape=jax.ShapeDtypeStruct((n_rows,) + tile, h2.dtype),
        compiler_params=_cparams(("arbitrary",)),
        name="moe_dispatch",
    )(pad_start, pad_count, n_used, dest_t, h2)


def _experts_kernel(bexp_ref, nused_ref, first_ref, next_ref, slot_ref, xs_ref, wg_hbm, wu_hbm, wd_hbm,
                    o_ref, wg_f, wu_f, wd_f, wg_s, wu_s, wd_s, sem):
    b = pl.program_id(0)
    n_used = nused_ref[0]

    def weight_copies(e, s):
        return (pltpu.make_async_copy(wg_hbm.at[e], wg_f.at[s], sem.at[s, 0]),
                pltpu.make_async_copy(wu_hbm.at[e], wu_f.at[s], sem.at[s, 1]),
                pltpu.make_async_copy(wd_hbm.at[e], wd_f.at[s], sem.at[s, 2]))

    @pl.when((b == 0) & (n_used > 0))
    def _():
        for c in weight_copies(bexp_ref[0], 0):
            c.start()

    @pl.when((b < n_used) & (first_ref[b] == 1))
    def _():
        s = slot_ref[b]
        for c in weight_copies(bexp_ref[b], s):
            c.wait()

        @pl.when(next_ref[b] >= 0)
        def _():
            for c in weight_copies(next_ref[b], 1 - s):
                c.start()

        wg_s[...] = wg_f[s].astype(BF16)
        wu_s[...] = wu_f[s].astype(BF16)
        wd_s[...] = wd_f[s].astype(BF16)

    @pl.when(b < n_used)
    def _():
        xb = _unpack_bf16_pairs(_load_token_tiles(xs_ref)).astype(BF16)
        hg = jnp.dot(xb, wg_s[...], preferred_element_type=F32)
        hu = jnp.dot(xb, wu_s[...], preferred_element_type=F32)
        act = (_silu(hg) * hu).astype(BF16)
        _store_token_tiles(o_ref, _pack_bf16_pairs(jnp.dot(act, wd_s[...], preferred_element_type=F32)))

    @pl.when(b >= n_used)
    def _():
        o_ref[...] = jnp.zeros(o_ref.shape, o_ref.dtype)


def _experts(xs, block_expert, n_used, w_g, w_u, w_d, bm):
    n_rows = xs.shape[0]
    tile = xs.shape[1:]
    d, ff = w_g.shape[1], w_g.shape[2]
    assert d == 2 * tile[0] * tile[1]
    nb = n_rows // bm
    blk = jnp.arange(nb, dtype=jnp.int32)
    valid = blk < n_used[0]
    first = (valid & ((blk == 0) | (block_expert != jnp.roll(block_expert, 1)))).astype(jnp.int32)
    later = valid[None, :] & (block_expert[None, :] > block_expert[:, None])
    nxt = jnp.min(jnp.where(later, block_expert[None, :], N_EXPERTS), axis=1)
    nxt = jnp.where(nxt == N_EXPERTS, -1, nxt).astype(jnp.int32)
    slot = ((jnp.cumsum(first) - 1) % 2).astype(jnp.int32)

    def used(b, nu):
        return jnp.minimum(b, jnp.maximum(nu[0] - 1, 0))

    grid_spec = pltpu.PrefetchScalarGridSpec(
        num_scalar_prefetch=5,
        grid=(nb,),
        in_specs=[pl.BlockSpec((bm,) + tile, lambda b, be, nu, *_: (used(b, nu), 0, 0)),
                  pl.BlockSpec(memory_space=pl.ANY),
                  pl.BlockSpec(memory_space=pl.ANY),
                  pl.BlockSpec(memory_space=pl.ANY)],
        out_specs=pl.BlockSpec((bm,) + tile, lambda b, *_: (b, 0, 0)),
        scratch_shapes=[pltpu.VMEM((2, d, ff), w_g.dtype),
                        pltpu.VMEM((2, d, ff), w_u.dtype),
                        pltpu.VMEM((2, ff, d), w_d.dtype),
                        pltpu.VMEM((d, ff), BF16),
                        pltpu.VMEM((d, ff), BF16),
                        pltpu.VMEM((ff, d), BF16),
                        pltpu.SemaphoreType.DMA((2, 3))],
    )
    return pl.pallas_call(
        _experts_kernel,
        grid_spec=grid_spec,
        out_shape=jax.ShapeDtypeStruct((n_rows,) + tile, xs.dtype),
        compiler_params=_cparams(("arbitrary",)),
        name="moe_experts",
    )(block_expert, n_used, first, nxt, slot, xs, w_g, w_u, w_d)


def _combine_kernel(pos_ref, x1_ref, rt_ref, nw_ref, y_hbm, o_ref, ybuf, sem, *, normalize):
    i = pl.program_id(0)
    n = pl.num_programs(0)
    tm = x1_ref.shape[0]

    def start_tile(slot, which):
        for r in range(tm):
            for k in range(TOP_K_INNER):
                pltpu.make_async_copy(y_hbm.at[pos_ref[which, k, r]], ybuf.at[slot, k, r],
                                      sem.at[slot]).start()

    def wait_tile(slot):
        for k in range(TOP_K_INNER):
            pltpu.make_async_copy(y_hbm.at[pl.ds(0, tm)], ybuf.at[slot, k], sem.at[slot]).wait()

    slot = i % 2

    @pl.when(i == 0)
    def _():
        start_tile(0, 0)

    wait_tile(slot)
    start_tile(1 - slot, 1)
    acc = x1_ref[...]
    for k in range(TOP_K_INNER):
        y_k = _unpack_bf16_pairs(_load_token_tiles(ybuf.at[slot, k]))
        acc = acc + rt_ref[:, ROUTE_W_LANE + k:ROUTE_W_LANE + k + 1] * y_k
    o_ref[...] = _rms(acc, nw_ref[...]) if normalize else acc

    @pl.when(i == n - 1)
    def _():
        wait_tile(1 - slot)


def _combine(x1, y_rows, pos, route, norm_w, tm, normalize):
    t, d = x1.shape
    nt = t // tm
    pos_t = pos.reshape(nt, tm, TOP_K_INNER).transpose(0, 2, 1)
    pos_next = jnp.concatenate([pos_t[1:], pos_t[-1:]], axis=0)
    pos2 = jnp.stack([pos_t, pos_next], axis=1)
    return pl.pallas_call(
        functools.partial(_combine_kernel, normalize=normalize),
        grid=(nt,),
        in_specs=[pl.BlockSpec((None, 2, TOP_K_INNER, tm), lambda i: (i, 0, 0, 0), memory_space=pltpu.SMEM),
                  pl.BlockSpec((tm, d), lambda i: (i, 0)),
                  pl.BlockSpec((tm, LANES), lambda i: (i, 0)),
                  pl.BlockSpec((1, d), lambda i: (0, 0)),
                  pl.BlockSpec(memory_space=pl.ANY)],
        out_specs=pl.BlockSpec((tm, d), lambda i: (i, 0)),
        out_shape=jax.ShapeDtypeStruct((t, d), F32),
        scratch_shapes=[pltpu.VMEM((2, TOP_K_INNER, tm) + y_rows.shape[1:], y_rows.dtype),
                        pltpu.SemaphoreType.DMA((2,))],
        compiler_params=_cparams(("arbitrary",)),
        name="moe_combine",
    )(pos2, x1, route, norm_w.reshape(1, d), y_rows)


def _layer(x, p, w_in_t, layer):
    t, d = x.shape
    inner = p["w_a_up"].shape[0]
    n_heads = p["a_log"].shape[0]
    s5_width = p["w_glu"].shape[0]
    xbc_dim = inner + 2 * SSD_GROUPS * SSD_STATE
    sizes = (inner, xbc_dim, n_heads, s5_width, 2 * d)
    offs = [0]
    for s in sizes:
        offs.append(offs[-1] + s)
    assert n_heads <= LANES and offs[2] + LANES <= offs[5]
    dt_b = jnp.zeros((1, LANES), F32).at[0, :n_heads].set(p["dt_bias"].astype(F32))

    tm = min(ROW_TILE, t)
    tmm = min(MM_ROW_TILE, t)

    h, dt, dtt = _norm_dt(x, p["norm_mix_w"], w_in_t, layer, offs[2], dt_b, tm)
    zs = _proj(h, w_in_t, layer, offs[0], inner, None, "silu", tmm, MM_COL_TILE)
    xbc = _conv_proj(h, w_in_t, layer, offs[1], xbc_dim, p["conv_w"], p["conv_b"], tmm, MM_COL_TILE)
    u_steps = _chunk_proj(h, w_in_t, layer, offs[3], s5_width)
    gates = _proj(h, w_in_t, layer, offs[4], 2 * d, p["gate_b"], "sigmoid_bias", tmm, MM_COL_TILE)

    y = _ssd(xbc, dt, dtt[:n_heads], p["a_log"].astype(F32), p["d_ssd"].astype(F32), n_heads,
             min(SSD_CHUNK, t))
    ya = _gated_up(y, zs, p["norm_ssd_w"], p["w_a_up"].astype(BF16), tm)

    uc, wc, bbt, a_chunk = _s5_operators(p["s5_lambda_re"], p["s5_lambda_im"], p["s5_log_dt"],
                                         p["s5_b_re"], p["s5_b_im"], p["s5_c_re"], p["s5_c_im"])
    y5 = _s5(u_steps, uc, wc, bbt, a_chunk, p["s5_d"].astype(F32), min(S5_ROWS, t // S5_CHUNK))
    yb = _glu_up(y5, p["w_glu"].astype(BF16), p["w_b_up"].astype(BF16))

    w_router = jnp.zeros((d, LANES), F32)
    w_router = w_router.at[:, :N_EXPERT_GROUPS].set(p["w_route_group"].astype(F32))
    w_router = w_router.at[:, ROUTE_EXPERT_LANE0:ROUTE_EXPERT_LANE0 + N_EXPERTS].set(
        p["w_route_expert"].astype(F32))
    r_b = jnp.zeros((1, LANES), F32)
    r_b = r_b.at[0, :N_EXPERT_GROUPS].set(p["b_route_group"].astype(F32))
    r_b = r_b.at[0, ROUTE_EXPERT_LANE0:ROUTE_EXPERT_LANE0 + N_EXPERTS].set(p["b_route_expert"].astype(F32))
    x1, h2, route = _merge_out(gates, ya, yb, x, p["w_out"].astype(BF16), p["norm_ffn_w"],
                               w_router.astype(BF16), r_b, tm)

    expert_ids = route[:, ROUTE_ID_LANE:ROUTE_ID_LANE + TOP_K_INNER].astype(jnp.int32)
    bm = MOE_BLOCK
    tg = min(GATHER_TILE, t)
    pos, block_expert, n_used, pad_start, pad_count, n_rows = _dispatch_plan(expert_ids, bm)
    xs = _dispatch(h2, pos, pad_start, pad_count, n_used, n_rows, tg, bm)
    y_rows = _experts(xs, block_expert, n_used, p["w_exp_gate"], p["w_exp_up"], p["w_exp_down"], bm)
    return x1, y_rows, pos, route


def kernel(x, norm_mix_w, w_in, conv_w, conv_b, dt_bias, a_log, d_ssd, norm_ssd_w, w_a_up,
           s5_lambda_re, s5_lambda_im, s5_log_dt, s5_b_re, s5_b_im, s5_c_re, s5_c_im, s5_d,
           w_glu, w_b_up, gate_b, w_out, norm_ffn_w, w_route_group, b_route_group,
           w_route_expert, b_route_expert, w_exp_gate, w_exp_up, w_exp_down, norm_final_w):
    b, seq, d = x.shape
    assert b == 1, "the scans carry state along the flattened token axis"
    depth = w_in.shape[0]
    per_layer = dict(norm_mix_w=norm_mix_w, conv_w=conv_w, conv_b=conv_b, dt_bias=dt_bias,
                     a_log=a_log, d_ssd=d_ssd, norm_ssd_w=norm_ssd_w, w_a_up=w_a_up,
                     s5_lambda_re=s5_lambda_re, s5_lambda_im=s5_lambda_im, s5_log_dt=s5_log_dt,
                     s5_b_re=s5_b_re, s5_b_im=s5_b_im, s5_c_re=s5_c_re, s5_c_im=s5_c_im, s5_d=s5_d,
                     w_glu=w_glu, w_b_up=w_b_up, gate_b=gate_b, w_out=w_out, norm_ffn_w=norm_ffn_w,
                     w_route_group=w_route_group, b_route_group=b_route_group,
                     w_route_expert=w_route_expert, b_route_expert=b_route_expert,
                     w_exp_gate=w_exp_gate, w_exp_up=w_exp_up, w_exp_down=w_exp_down)
    xt = x.reshape(b * seq, d)
    tg = min(GATHER_TILE, b * seq)
    w_in_t = jnp.swapaxes(w_in.astype(F32), 1, 2)
    for i in range(depth):
        p = {k: v[i] for k, v in per_layer.items()}
        x1, y_rows, pos, route = _layer(xt, p, w_in_t, i)
        xt = _combine(x1, y_rows, pos, route, norm_final_w, tg, normalize=(i == depth - 1))
    return xt.reshape(b, seq, d)
```

```python
import functools
import math

import jax
import jax.numpy as jnp
from jax import lax
from jax.experimental import pallas as pl
from jax.experimental.pallas import tpu as pltpu

F32 = jnp.float32
BF16 = jnp.bfloat16

SSD_HEAD_DIM = 64
SSD_GROUPS = 8
SSD_STATE = 128
CONV_WIDTH = 4
S5_GROUP_CH = 16
S5_STATE = 64
N_EXPERT_GROUPS = 4
EXPERTS_PER_GROUP = 8
N_EXPERTS = N_EXPERT_GROUPS * EXPERTS_PER_GROUP
TOP_K_INNER = 2
RMS_EPS = 1e-6

LANES = 128
SUBLANES = 8
VMEM_LIMIT_BYTES = 52 * 1024 * 1024

ROW_TILE = 512
MM_ROW_TILE = 1024
MM_COL_TILE = 1024
SSD_CHUNK = 128
S5_CHUNK = 16
S5_SUPER = S5_GROUP_CH * 8
S5_ROWS = 256
MOE_BLOCK = 256
GATHER_TILE = 256


def _cparams(sem, vmem=VMEM_LIMIT_BYTES):
    return pltpu.CompilerParams(dimension_semantics=sem, vmem_limit_bytes=vmem)


def _sigmoid(v):
    return 1.0 / (1.0 + jnp.exp(-v))


def _silu(v):
    return v * _sigmoid(v)


def _softplus(v):
    return jnp.maximum(v, 0.0) + jnp.log(1.0 + jnp.exp(-jnp.abs(v)))


def _gelu_tanh(v):
    c = math.sqrt(2.0 / math.pi)
    return 0.5 * v * (1.0 + jnp.tanh(c * (v + 0.044715 * (v * v * v))))


def _rms(v, w):
    ms = jnp.mean(v * v, axis=-1, keepdims=True)
    return v * lax.rsqrt(ms + RMS_EPS) * w


def _pack_bf16_pairs(v):
    n = v.shape[1] // 2
    lo = pltpu.bitcast(v[:, :n].astype(BF16).astype(F32), jnp.uint32)
    hi = pltpu.bitcast(v[:, n:].astype(BF16).astype(F32), jnp.uint32)
    return hi | (lo >> 16)


def _unpack_bf16_pairs(w):
    lo = pltpu.bitcast(w << 16, F32)
    hi = pltpu.bitcast(w & jnp.uint32(0xFFFF0000), F32)
    return jnp.concatenate([lo, hi], axis=1)


def _store_token_tiles(ref, rows):
    for a in range(ref.shape[-2]):
        ref[:, a, :] = rows[:, a * LANES:(a + 1) * LANES]


def _load_token_tiles(ref):
    return jnp.concatenate([ref[:, a, :] for a in range(ref.shape[-2])], axis=1)


def _resident_weight(wt_ref, wbf_ref, row_axis=1):
    @pl.when(pl.program_id(row_axis) == 0)
    def _():
        wbf_ref[...] = wt_ref[...].T.astype(BF16)
    return wbf_ref


def _weight_rows_spec(layer, row0, n_rows, k, index_of_step):
    assert row0 % SUBLANES == 0 and n_rows % SUBLANES == 0
    return pl.BlockSpec((None, pl.Element(n_rows), pl.Element(k)),
                        lambda *idx: (layer, pl.multiple_of(row0 + n_rows * index_of_step(*idx), SUBLANES), 0))


def _proj_kernel(a_ref, wt_ref, b_ref, o_ref, wbf_ref, *, act):
    w = _resident_weight(wt_ref, wbf_ref)
    p = jnp.dot(a_ref[...], w[...], preferred_element_type=F32)
    if act == "silu":
        p = _silu(p)
    elif act == "sigmoid_bias":
        p = _sigmoid(p + b_ref[...])
    o_ref[...] = p.astype(o_ref.dtype)


def _proj(h, wt, layer, row0, n, b, act, tm, tn):
    t, k = h.shape
    tn = min(tn, n)
    assert n % tn == 0 and row0 % SUBLANES == 0
    if b is None:
        b = jnp.zeros((1, n), F32)
    return pl.pallas_call(
        functools.partial(_proj_kernel, act=act),
        grid=(n // tn, t // tm),
        in_specs=[pl.BlockSpec((tm, k), lambda j, i: (i, 0)),
                  _weight_rows_spec(layer, row0, tn, k, lambda j, i: j),
                  pl.BlockSpec((1, tn), lambda j, i: (0, j))],
        out_specs=pl.BlockSpec((tm, tn), lambda j, i: (i, j)),
        out_shape=jax.ShapeDtypeStruct((t, n), BF16),
        scratch_shapes=[pltpu.VMEM((k, tn), BF16)],
        compiler_params=_cparams(("parallel", "arbitrary")),
        name="proj_" + act,
    )(h, wt, b.reshape(1, n))


def _conv_proj_kernel(a_ref, w_ref, cw_ref, cb_ref, o_ref, ext_ref, wbf_ref):
    tm = a_ref.shape[0]
    halo = SUBLANES

    @pl.when(pl.program_id(1) == 0)
    def _():
        ext_ref[pl.ds(0, halo), :] = jnp.zeros((halo, ext_ref.shape[1]), F32)

    w = _resident_weight(w_ref, wbf_ref)
    p = jnp.dot(a_ref[...], w[...], preferred_element_type=F32)
    ext_ref[pl.ds(halo, tm), :] = p
    acc = cb_ref[...] + cw_ref[CONV_WIDTH - 1:CONV_WIDTH, :] * p
    for k in range(CONV_WIDTH - 1):
        back = CONV_WIDTH - 1 - k
        acc = acc + cw_ref[k:k + 1, :] * ext_ref[pl.ds(halo - back, tm), :]
    o_ref[...] = _silu(acc).astype(o_ref.dtype)
    ext_ref[pl.ds(0, halo), :] = p[tm - halo:, :]


def _conv_proj(h, wt, layer, row0, n, conv_w, conv_b, tm, tn):
    t, k = h.shape
    assert n % tn == 0 and row0 % SUBLANES == 0
    return pl.pallas_call(
        _conv_proj_kernel,
        grid=(n // tn, t // tm),
        in_specs=[pl.BlockSpec((tm, k), lambda j, i: (i, 0)),
                  _weight_rows_spec(layer, row0, tn, k, lambda j, i: j),
                  pl.BlockSpec((CONV_WIDTH, tn), lambda j, i: (0, j)),
                  pl.BlockSpec((1, tn), lambda j, i: (0, j))],
        out_specs=pl.BlockSpec((tm, tn), lambda j, i: (i, j)),
        out_shape=jax.ShapeDtypeStruct((t, n), BF16),
        scratch_shapes=[pltpu.VMEM((tm + SUBLANES, tn), F32),
                        pltpu.VMEM((k, tn), BF16)],
        compiler_params=_cparams(("parallel", "arbitrary")),
        name="proj_conv",
    )(h, wt, conv_w, conv_b.reshape(1, n))


def _norm_dt_kernel(x_ref, nw_ref, wt_ref, b_ref, h_ref, dt_ref, dtt_ref, wbf_ref):
    w = _resident_weight(wt_ref, wbf_ref, row_axis=0)
    h = _rms(x_ref[...], nw_ref[...]).astype(BF16)
    h_ref[...] = h
    p = jnp.dot(h, w[...], preferred_element_type=F32) + b_ref[...]
    dt = _softplus(p)
    dt_ref[...] = dt
    dtt_ref[...] = dt.T


def _norm_dt(x, norm_w, wt, layer, row0, b_pad, tm):
    t, k = x.shape
    return pl.pallas_call(
        _norm_dt_kernel,
        grid=(t // tm,),
        in_specs=[pl.BlockSpec((tm, k), lambda i: (i, 0)),
                  pl.BlockSpec((1, k), lambda i: (0, 0)),
                  _weight_rows_spec(layer, row0, LANES, k, lambda i: 0),
                  pl.BlockSpec((1, LANES), lambda i: (0, 0))],
        out_specs=[pl.BlockSpec((tm, k), lambda i: (i, 0)),
                   pl.BlockSpec((tm, LANES), lambda i: (i, 0)),
                   pl.BlockSpec((LANES, tm), lambda i: (0, i))],
        out_shape=[jax.ShapeDtypeStruct((t, k), BF16),
                   jax.ShapeDtypeStruct((t, LANES), F32),
                   jax.ShapeDtypeStruct((LANES, t), F32)],
        scratch_shapes=[pltpu.VMEM((k, LANES), BF16)],
        compiler_params=_cparams(("arbitrary",)),
        name="norm_dt",
    )(x, norm_w.reshape(1, k), wt, b_pad)


def _step_major_perm(n_rows):
    assert n_rows == S5_CHUNK * S5_CHUNK
    shift = S5_CHUNK.bit_length() - 1
    row = lax.broadcasted_iota(jnp.int32, (n_rows, n_rows), 0)
    col = lax.broadcasted_iota(jnp.int32, (n_rows, n_rows), 1)
    swapped = ((row & (S5_CHUNK - 1)) << shift) | (row >> shift)
    return jnp.where(col == swapped, 1.0, 0.0).astype(BF16)


S5_PERM_ROWS = S5_CHUNK * S5_CHUNK
S5_TILE_SUBS = 2


def _chunk_proj_kernel(a_ref, w_ref, o_ref, wbf_ref):
    w = _resident_weight(w_ref, wbf_ref, row_axis=0)
    u = jnp.dot(a_ref[...], w[...], preferred_element_type=F32).astype(BF16)
    perm = _step_major_perm(S5_PERM_ROWS)
    for sub in range(u.shape[0] // S5_PERM_ROWS):
        u_steps = jnp.dot(perm, u[sub * S5_PERM_ROWS:(sub + 1) * S5_PERM_ROWS, :],
                          preferred_element_type=F32).astype(o_ref.dtype)
        for s in range(o_ref.shape[0]):
            o_ref[s, sub * S5_CHUNK:(sub + 1) * S5_CHUNK, :] = u_steps[s * S5_CHUNK:(s + 1) * S5_CHUNK, :]


def _chunk_proj(h, wt, layer, row0, n):
    t, k = h.shape
    nj = t // S5_CHUNK
    subs = S5_TILE_SUBS if t % (S5_TILE_SUBS * S5_PERM_ROWS) == 0 else 1
    tm = subs * S5_PERM_ROWS
    assert row0 % SUBLANES == 0
    return pl.pallas_call(
        _chunk_proj_kernel,
        grid=(t // tm,),
        in_specs=[pl.BlockSpec((tm, k), lambda i: (i, 0)),
                  _weight_rows_spec(layer, row0, n, k, lambda i: 0)],
        out_specs=pl.BlockSpec((S5_CHUNK, subs * S5_CHUNK, n), lambda i: (0, i, 0)),
        out_shape=jax.ShapeDtypeStruct((S5_CHUNK, nj, n), BF16),
        scratch_shapes=[pltpu.VMEM((k, n), BF16)],
        compiler_params=_cparams(("arbitrary",)),
        name="proj_u",
    )(h, wt)


def _cumsum_rows(v):
    n = v.shape[0]
    idx = lax.broadcasted_iota(jnp.int32, v.shape, 0)
    k = 1
    while k < n:
        v = v + jnp.where(idx >= k, pltpu.roll(v, k, 0), 0.0)
        k *= 2
    return v


def _cumsum_lanes(v):
    n = v.shape[1]
    idx = lax.broadcasted_iota(jnp.int32, v.shape, 1)
    k = 1
    while k < n:
        v = v + jnp.where(idx >= k, pltpu.roll(v, k, 1), 0.0)
        k *= 2
    return v


def _ssd_kernel(xbc_ref, dt_ref, dtt_ref, alog_r_ref, alog_c_ref, dskip_ref, o_ref, state_ref, *,
                n_heads):
    q = dt_ref.shape[0]
    n = SSD_STATE
    p_dim = SSD_HEAD_DIM
    r_heads = n_heads // SSD_GROUPS
    gw = r_heads * p_dim
    inner = n_heads * p_dim

    @pl.when(pl.program_id(0) == 0)
    def _():
        state_ref[...] = jnp.zeros(state_ref.shape, F32)

    dtt = dtt_ref[...]
    cs_col = _cumsum_rows(dt_ref[...] * -jnp.exp(alog_r_ref[...]))
    cs_row = _cumsum_lanes(dtt * -jnp.exp(alog_c_ref[...]))
    causal = lax.broadcasted_iota(jnp.int32, (q, q), 0) >= lax.broadcasted_iota(jnp.int32, (q, q), 1)
    head_of_lane = lax.broadcasted_iota(jnp.int32, (1, gw), 1) // p_dim
    eye = jnp.where(lax.broadcasted_iota(jnp.int32, (n, n), 0) == lax.broadcasted_iota(jnp.int32, (n, n), 1),
                    1.0, 0.0).astype(BF16)
    nt = (((1,), (1,)), ((), ()))

    cbs, b_ts = [], []
    for g in range(SSD_GROUPS):
        b_g = xbc_ref[:, inner + g * n:inner + (g + 1) * n]
        c_g = xbc_ref[:, inner + (SSD_GROUPS + g) * n:inner + (SSD_GROUPS + g + 1) * n]
        cbs.append(lax.dot_general(c_g, b_g, nt, preferred_element_type=F32))
        b_ts.append(lax.dot_general(eye, b_g, nt, preferred_element_type=F32))

    def operands(g):
        c_g = xbc_ref[:, inner + (SSD_GROUPS + g) * n:inner + (SSD_GROUPS + g + 1) * n]
        cb, b_t = cbs[g], b_ts[g]
        c_f = c_g.astype(F32)
        lhs_parts, bt_parts, cd = [], [], jnp.zeros((1, gw), F32)
        for r in range(r_heads):
            h = g * r_heads + r
            csb = jnp.broadcast_to(cs_col[:, h:h + 1], (q, n))
            csr = cs_row[h:h + 1, :]
            dtr = dtt[h:h + 1, :]
            cs_last = csr[:, q - 1:q]
            decay = jnp.exp(jnp.where(causal, csb - csr, -1e30))
            lhs_parts.append(jnp.concatenate([cb * decay * dtr, c_f * jnp.exp(csb)], axis=1).astype(BF16))
            bt_parts.append((b_t * (dtr * jnp.exp(cs_last - csr))).astype(BF16))
            cd = jnp.where(head_of_lane == r, jnp.exp(cs_last), cd)
        return lhs_parts, bt_parts, cd

    per_half = LANES // p_dim
    n_half = gw // LANES
    half_heads = [range(half * per_half, (half + 1) * per_half) for half in range(n_half)]
    half_cols = [slice(half * LANES, (half + 1) * LANES) for half in range(n_half)]

    def issue(g):
        lhs_parts, bt_parts, cd = operands(g)
        x_g = xbc_ref[:, g * gw:(g + 1) * gw]
        s_g = state_ref[g]
        rhs = jnp.concatenate([x_g, s_g.astype(BF16)], axis=0)
        y_alls = [jnp.dot(jnp.concatenate([lhs_parts[r] for r in half_heads[hf]], axis=0),
                          rhs[:, half_cols[hf]], preferred_element_type=F32) for hf in range(n_half)]
        s_alls = [jnp.dot(jnp.concatenate([bt_parts[r] for r in half_heads[hf]], axis=0),
                          x_g[:, half_cols[hf]], preferred_element_type=F32) for hf in range(n_half)]
        return g, x_g, s_g, cd, y_alls, s_alls

    def finish(g, x_g, s_g, cd, y_alls, s_alls):
        y_halves, s_halves = [], []
        for half in range(n_half):
            heads, cols = half_heads[half], half_cols[half]
            y_all, s_all = y_alls[half], s_alls[half]
            lane_head = head_of_lane[:, cols]
            y_h, s_h = y_all[:q, :], s_all[:n, :]
            for j, r in enumerate(heads):
                if j:
                    y_h = jnp.where(lane_head == r, y_all[j * q:(j + 1) * q, :], y_h)
                    s_h = jnp.where(lane_head == r, s_all[j * n:(j + 1) * n, :], s_h)
            y_halves.append(y_h)
            s_halves.append(s_h)
        state_ref[g] = s_g * cd + jnp.concatenate(s_halves, axis=1)
        y_g = jnp.concatenate(y_halves, axis=1) + x_g.astype(F32) * dskip_ref[:, g * gw:(g + 1) * gw]
        o_ref[:, g * gw:(g + 1) * gw] = y_g.astype(o_ref.dtype)

    for g in range(SSD_GROUPS):
        finish(*issue(g))


def _ssd(xbc, dt, dtt, a_log, d_skip, n_heads, q):
    t, width = xbc.shape
    inner = n_heads * SSD_HEAD_DIM
    gw = inner // SSD_GROUPS
    assert n_heads % SUBLANES == 0 and n_heads <= LANES
    alog_r = jnp.zeros((1, LANES), F32).at[0, :n_heads].set(a_log)
    alog_c = a_log.reshape(n_heads, 1)
    dskip = jnp.repeat(d_skip, SSD_HEAD_DIM).reshape(1, inner)
    return pl.pallas_call(
        functools.partial(_ssd_kernel, n_heads=n_heads),
        grid=(t // q,),
        in_specs=[pl.BlockSpec((q, width), lambda c: (c, 0)),
                  pl.BlockSpec((q, LANES), lambda c: (c, 0)),
                  pl.BlockSpec((n_heads, q), lambda c: (0, c)),
                  pl.BlockSpec((1, LANES), lambda c: (0, 0)),
                  pl.BlockSpec((n_heads, 1), lambda c: (0, 0)),
                  pl.BlockSpec((1, inner), lambda c: (0, 0))],
        out_specs=pl.BlockSpec((q, inner), lambda c: (c, 0)),
        out_shape=jax.ShapeDtypeStruct((t, inner), BF16),
        scratch_shapes=[pltpu.VMEM((SSD_GROUPS, SSD_STATE, gw), F32)],
        compiler_params=_cparams(("arbitrary",)),
        name="ssd_scan",
    )(xbc, dt, dtt, alog_r, alog_c, dskip)


def _gated_up_kernel(y_ref, z_ref, nw_ref, w_ref, o_ref):
    v = y_ref[...].astype(F32) * z_ref[...].astype(F32)
    na = _rms(v, nw_ref[...]).astype(BF16)
    o_ref[...] = jnp.dot(na, w_ref[...], preferred_element_type=F32).astype(o_ref.dtype)


def _gated_up(y, zs, norm_w, w, tm):
    t, d = y.shape
    n = w.shape[1]
    return pl.pallas_call(
        _gated_up_kernel,
        grid=(t // tm,),
        in_specs=[pl.BlockSpec((tm, d), lambda i: (i, 0)),
                  pl.BlockSpec((tm, d), lambda i: (i, 0)),
                  pl.BlockSpec((1, d), lambda i: (0, 0)),
                  pl.BlockSpec((d, n), lambda i: (0, 0))],
        out_specs=pl.BlockSpec((tm, n), lambda i: (i, 0)),
        out_shape=jax.ShapeDtypeStruct((t, n), BF16),
        compiler_params=_cparams(("parallel",)),
        name="ssd_gated_up",
    )(y, zs, norm_w.reshape(1, d), w)


def _s5_operators(lam_re, lam_im, log_dt, b_re, b_im, c_re, c_im):
    ng, ns = lam_re.shape
    nc = S5_GROUP_CH
    L = S5_CHUNK
    per = S5_SUPER // nc
    nsg = ng // per
    lr, li = lam_re.astype(F32), lam_im.astype(F32)
    dt = jnp.exp(log_dt.astype(F32))[:, None]
    mag = jnp.exp(lr * dt)
    ang = li * dt
    abar_r, abar_i = mag * jnp.cos(ang), mag * jnp.sin(ang)
    den = lr * lr + li * li
    nr, ni = abar_r - 1.0, abar_i
    coef_r = (nr * lr + ni * li) / den
    coef_i = (ni * lr - nr * li) / den
    bre, bim = b_re.astype(F32), b_im.astype(F32)
    bb_r = coef_r[..., None] * bre - coef_i[..., None] * bim
    bb_i = coef_r[..., None] * bim + coef_i[..., None] * bre
    cre, cim = c_re.astype(F32), c_im.astype(F32)
    ks = jnp.arange(L + 1, dtype=F32)[:, None, None]
    pmag = jnp.exp(ks * (lr * dt)[None])
    pang = ks * ang[None]
    pw_r, pw_i = pmag * jnp.cos(pang), pmag * jnp.sin(pang)
    ca_r = cre[None] * pw_r[:, :, None, :] - cim[None] * pw_i[:, :, None, :]
    ca_i = cre[None] * pw_i[:, :, None, :] + cim[None] * pw_r[:, :, None, :]
    def _rows_n(v, steps):
        return v.reshape(steps, nsg, per, nc, ns).transpose(1, 4, 0, 2, 3).reshape(nsg, ns, steps * per * nc)

    uc = jnp.concatenate([_rows_n(ca_r, L + 1), -_rows_n(ca_i, L + 1)], axis=1)
    ks_rev = (L - 1) - jnp.arange(L, dtype=F32)[:, None, None]
    rmag = jnp.exp(ks_rev * (lr * dt)[None])
    rang = ks_rev * ang[None]
    rev_r, rev_i = rmag * jnp.cos(rang), rmag * jnp.sin(rang)
    ab_r = rev_r[..., None] * bb_r[None] - rev_i[..., None] * bb_i[None]
    ab_i = rev_r[..., None] * bb_i[None] + rev_i[..., None] * bb_r[None]
    ab_rt, ab_it = ab_r.transpose(0, 1, 3, 2), ab_i.transpose(0, 1, 3, 2)
    wc = jnp.concatenate([_rows_n(ab_rt, L), _rows_n(ab_it, L)], axis=1)

    def _rows_gc(v):
        return v.reshape(nsg, per, ns, nc).transpose(0, 1, 3, 2).reshape(nsg, per * nc, ns)

    bbt = jnp.concatenate([_rows_gc(bb_r), _rows_gc(bb_i)], axis=2)
    a_chunk = jnp.concatenate([pw_r[L].reshape(nsg, 1, per * ns),
                               pw_i[L].reshape(nsg, 1, per * ns)], axis=2)
    return uc, wc, bbt, a_chunk


def _split_bf16(v):
    hi = v.astype(BF16)
    return hi, (v - hi.astype(F32)).astype(BF16)


def _s5_kernel(u_ref, uc_ref, wc_ref, bbt_ref, ach_ref, dsk_ref, o_ref,
               toep_ref, wt_ref, v_ref, x_ref, sp_ref, carry_ref):
    jb = pl.program_id(1)
    nl, nj, cw = u_ref.shape
    half = carry_ref.shape[1] // 2
    per = cw // S5_GROUP_CH
    ns = half // per
    nt = (((1,), (1,)), ((), ()))

    @pl.when(jb == 0)
    def _():
        carry_ref[...] = jnp.zeros(carry_ref.shape, F32)
        uc = uc_ref[...]
        wc = wc_ref[...]
        b_hi, b_lo = _split_bf16(bbt_ref[...])
        u_hi, u_lo = _split_bf16(uc[:, :nl * cw])
        kall = (jnp.dot(b_hi, u_hi, preferred_element_type=F32)
                + jnp.dot(b_lo, u_hi, preferred_element_type=F32)
                + jnp.dot(b_hi, u_lo, preferred_element_type=F32))
        row_g = lax.broadcasted_iota(jnp.int32, (cw, 1), 0) // S5_GROUP_CH
        col_g = (lax.broadcasted_iota(jnp.int32, (1, nl * cw), 1) // S5_GROUP_CH) % per
        kall = jnp.where(row_g == col_g, kall, 0.0).astype(BF16)
        toep_ref[...] = jnp.zeros(toep_ref.shape, toep_ref.dtype)
        for s_in in range(nl):
            for s_out in range(s_in, nl):
                k = s_out - s_in
                toep_ref[pl.ds(s_in * cw, cw), pl.ds(s_out * cw, cw)] = kall[:, k * cw:(k + 1) * cw]
        for gp in range(per):
            mine = col_g == gp
            for part in range(2):
                rows = pl.ds(part * half + gp * ns, ns)
                src = slice(part * ns, (part + 1) * ns)
                v_ref[rows, :] = jnp.where(mine, uc[src, cw:], 0.0).astype(BF16)
                wt_ref[rows, :] = jnp.where(mine, wc[src, :], 0.0).astype(BF16)

    lhs = jnp.concatenate([u_ref[s] for s in range(nl)], axis=1)
    x_ref[...] = lax.dot_general(lhs, wt_ref[...], nt, preferred_element_type=F32)

    pair = 2 * cw
    y_intra = []
    for tp in range(nl // 2):
        kdim = pair * (tp + 1)
        y_intra.append(jnp.dot(lhs[:, :kdim], toep_ref[pl.ds(0, kdim), pl.ds(tp * pair, pair)],
                               preferred_element_type=F32))

    a_re = ach_ref[:, :half]
    a_im = ach_ref[:, half:]
    s_re, s_im = carry_ref[:, :half], carry_ref[:, half:]
    for j in range(nj):
        sp_ref[j:j + 1, :half] = s_re
        sp_ref[j:j + 1, half:] = s_im
        xr = x_ref[j:j + 1, :half]
        xi = x_ref[j:j + 1, half:]
        s_re, s_im = a_re * s_re - a_im * s_im + xr, a_re * s_im + a_im * s_re + xi
    carry_ref[:, :half] = s_re
    carry_ref[:, half:] = s_im

    y_state = jnp.dot(sp_ref[...].astype(BF16), v_ref[...], preferred_element_type=F32)
    dsk = dsk_ref[...]
    for tp in range(nl // 2):
        y = y_intra[tp] + y_state[:, tp * pair:(tp + 1) * pair]
        for h in range(2):
            s = 2 * tp + h
            o_ref[s] = (y[:, h * cw:(h + 1) * cw] + dsk * u_ref[s].astype(F32)).astype(o_ref.dtype)


def _s5(u_steps, uc, wc, bbt, a_chunk, d_skip, tj):
    nl, nj, width = u_steps.shape
    nsg = width // S5_SUPER
    nstate = a_chunk.shape[2]
    rows = uc.shape[1]
    return pl.pallas_call(
        _s5_kernel,
        grid=(nsg, nj // tj),
        in_specs=[pl.BlockSpec((nl, tj, S5_SUPER), lambda g, j: (0, j, g)),
                  pl.BlockSpec((None, rows, (nl + 1) * S5_SUPER), lambda g, j: (g, 0, 0)),
                  pl.BlockSpec((None, rows, nl * S5_SUPER), lambda g, j: (g, 0, 0)),
                  pl.BlockSpec((None, S5_SUPER, rows), lambda g, j: (g, 0, 0)),
                  pl.BlockSpec((None, 1, nstate), lambda g, j: (g, 0, 0)),
                  pl.BlockSpec((None, 1, S5_SUPER), lambda g, j: (g, 0, 0))],
        out_specs=pl.BlockSpec((nl, tj, S5_SUPER), lambda g, j: (0, j, g)),
        out_shape=jax.ShapeDtypeStruct((nl, nj, width), BF16),
        scratch_shapes=[pltpu.VMEM((nl * S5_SUPER, nl * S5_SUPER), BF16),
                        pltpu.VMEM((nstate, nl * S5_SUPER), BF16),
                        pltpu.VMEM((nstate, nl * S5_SUPER), BF16),
                        pltpu.VMEM((tj, nstate), F32),
                        pltpu.VMEM((tj, nstate), F32),
                        pltpu.VMEM((1, nstate), F32)],
        compiler_params=_cparams(("parallel", "arbitrary")),
        name="s5_scan",
    )(u_steps, uc, wc, bbt, a_chunk, d_skip.reshape(nsg, 1, S5_SUPER))


def _glu_up_kernel(y_ref, wg_ref, wu_ref, o_ref):
    perm = _step_major_perm(S5_PERM_ROWS)
    parts = []
    for sub in range(y_ref.shape[1] // S5_CHUNK):
        y_steps = jnp.concatenate([y_ref[s, sub * S5_CHUNK:(sub + 1) * S5_CHUNK, :]
                                   for s in range(y_ref.shape[0])], axis=0)
        parts.append(jnp.dot(perm, y_steps, preferred_element_type=F32))
    y = jnp.concatenate(parts, axis=0)
    v = _gelu_tanh(y)
    gate = _sigmoid(jnp.dot(v.astype(BF16), wg_ref[...], preferred_element_type=F32))
    o_ref[...] = jnp.dot((v * gate).astype(BF16), wu_ref[...],
                         preferred_element_type=F32).astype(o_ref.dtype)


def _glu_up(y_steps, w_glu, w_up):
    nl, nj, width = y_steps.shape
    n = w_up.shape[1]
    subs = S5_TILE_SUBS if nj % (S5_TILE_SUBS * S5_CHUNK) == 0 else 1
    tm = subs * S5_PERM_ROWS
    return pl.pallas_call(
        _glu_up_kernel,
        grid=(nj // (subs * S5_CHUNK),),
        in_specs=[pl.BlockSpec((nl, subs * S5_CHUNK, width), lambda i: (0, i, 0)),
                  pl.BlockSpec((width, width), lambda i: (0, 0)),
                  pl.BlockSpec((width, n), lambda i: (0, 0))],
        out_specs=pl.BlockSpec((tm, n), lambda i: (i, 0)),
        out_shape=jax.ShapeDtypeStruct((nj * nl, n), BF16),
        compiler_params=_cparams(("parallel",)),
        name="s5_glu_up",
    )(y_steps, w_glu, w_up)


def _merge_out_kernel(ga_ref, gb_ref, ya_ref, yb_ref, x_ref, w_ref, nw_ref, rw_ref, rb_ref,
                      x1_ref, h2_ref, rt_ref):
    merged = (ga_ref[...].astype(F32) * ya_ref[...].astype(F32)
              + gb_ref[...].astype(F32) * yb_ref[...].astype(F32))
    x1 = x_ref[...] + jnp.dot(merged.astype(BF16), w_ref[...], preferred_element_type=F32)
    x1_ref[...] = x1
    h2 = _rms(x1, nw_ref[...])
    h2_ref[...] = _pack_bf16_pairs(h2)
    lg = jnp.dot(h2.astype(BF16), rw_ref[...], preferred_element_type=F32)
    rt_ref[...] = _route_math(lg + rb_ref[...])


def _merge_out(gates, ya, yb, x, w_out, norm_w, r_w, r_b, tm):
    t, d = x.shape
    row = lambda i: (i, 0)
    full = lambda i: (0, 0)
    return pl.pallas_call(
        _merge_out_kernel,
        grid=(t // tm,),
        in_specs=[pl.BlockSpec((tm, d), row),
                  pl.BlockSpec((tm, d), lambda i: (i, 1)),
                  pl.BlockSpec((tm, d), row),
                  pl.BlockSpec((tm, d), row),
                  pl.BlockSpec((tm, d), row),
                  pl.BlockSpec((d, d), full),
                  pl.BlockSpec((1, d), full),
                  pl.BlockSpec((d, LANES), full),
                  pl.BlockSpec((1, LANES), full)],
        out_specs=[pl.BlockSpec((tm, d), row),
                   pl.BlockSpec((tm, d // 2), row),
                   pl.BlockSpec((tm, LANES), row)],
        out_shape=[jax.ShapeDtypeStruct((t, d), F32),
                   jax.ShapeDtypeStruct((t, d // 2), jnp.uint32),
                   jax.ShapeDtypeStruct((t, LANES), F32)],
        compiler_params=_cparams(("parallel",)),
        name="merge_out",
    )(gates, gates, ya, yb, x, w_out, norm_w.reshape(1, d), r_w, r_b)


ROUTE_ID_LANE = 0
ROUTE_W_LANE = TOP_K_INNER
ROUTE_EXPERT_LANE0 = SUBLANES


def _route_math(lg):
    lane = lax.broadcasted_iota(jnp.int32, lg.shape, 1)
    neg = -jnp.inf
    big = LANES
    is_g = lane < N_EXPERT_GROUPS
    gl = jnp.where(is_g, lg, neg)
    gmax = jnp.max(gl, axis=1, keepdims=True)
    grp = jnp.min(jnp.where(gl == gmax, lane, big), axis=1, keepdims=True)
    pg_sel = 1.0 / jnp.sum(jnp.where(is_g, jnp.exp(lg - gmax), 0.0), axis=1, keepdims=True)
    e_lo = ROUTE_EXPERT_LANE0 + grp * EXPERTS_PER_GROUP
    in_grp = (lane >= e_lo) & (lane < e_lo + EXPERTS_PER_GROUP)
    el = jnp.where(in_grp, lg, neg)
    v0 = jnp.max(el, axis=1, keepdims=True)
    i0 = jnp.min(jnp.where(el == v0, lane, big), axis=1, keepdims=True)
    el1 = jnp.where(lane == i0, neg, el)
    v1 = jnp.max(el1, axis=1, keepdims=True)
    i1 = jnp.min(jnp.where(el1 == v1, lane, big), axis=1, keepdims=True)
    e1w = jnp.exp(v1 - v0)
    w0 = pg_sel / (1.0 + e1w)
    w1 = pg_sel * e1w / (1.0 + e1w)
    return jnp.where(lane == ROUTE_ID_LANE, (i0 - ROUTE_EXPERT_LANE0).astype(F32),
           jnp.where(lane == ROUTE_ID_LANE + 1, (i1 - ROUTE_EXPERT_LANE0).astype(F32),
           jnp.where(lane == ROUTE_W_LANE, w0, jnp.where(lane == ROUTE_W_LANE + 1, w1, 0.0))))


def _dispatch_plan(expert_ids, bm):
    n_tokens = expert_ids.shape[0]
    n_assign = n_tokens * TOP_K_INNER
    eid = expert_ids.reshape(n_assign)
    experts = jnp.arange(N_EXPERTS, dtype=jnp.int32)
    onehot = (eid[:, None] == experts[None, :]).astype(jnp.int32)
    csum = jnp.cumsum(onehot, axis=0)
    counts = csum[-1]
    rank = jnp.sum(onehot * csum, axis=1) - 1
    padded = ((counts + bm - 1) // bm) * bm
    ends = jnp.cumsum(padded)
    starts = ends - padded
    dest = jnp.sum(onehot * starts[None, :], axis=1) + rank
    n_rows = n_assign + N_EXPERTS * bm
    nb = n_rows // bm
    block_start = jnp.arange(nb, dtype=jnp.int32) * bm
    block_expert = jnp.minimum(jnp.sum((ends[None, :] <= block_start[:, None]).astype(jnp.int32), axis=1),
                               N_EXPERTS - 1)
    n_used = (ends[-1] // bm).astype(jnp.int32).reshape(1)
    return (dest.reshape(n_tokens, TOP_K_INNER), block_expert, n_used,
            (starts + counts).astype(jnp.int32), (padded - counts).astype(jnp.int32), n_rows)


def _dispatch_kernel(pad_start_ref, pad_count_ref, nused_ref, dest_ref, h_ref, xs_hbm, stage, zrow, sem, zsem):
    i = pl.program_id(0)
    n_steps = pl.num_programs(0)
    tm = h_ref.shape[0]
    bm = zrow.shape[0]
    nb = xs_hbm.shape[0] // bm
    slot = i % 2

    def tile_wait(s):
        for _ in range(TOP_K_INNER):
            pltpu.make_async_copy(stage.at[s], xs_hbm.at[pl.ds(0, tm)], sem.at[s]).wait()

    @pl.when(i >= 2)
    def _():
        tile_wait(slot)

    _store_token_tiles(stage.at[slot], h_ref[...])

    for r in range(tm):
        for k in range(TOP_K_INNER):
            pltpu.make_async_copy(stage.at[slot, r], xs_hbm.at[dest_ref[k, r]], sem.at[slot]).start()

    @pl.when(i == n_steps - 1)
    def _():
        tile_wait(slot)

        @pl.when(n_steps >= 2)
        def _():
            tile_wait(1 - slot)

        zrow[...] = jnp.zeros(zrow.shape, zrow.dtype)

        def pad_copy(e, r):
            return pltpu.make_async_copy(zrow.at[0], xs_hbm.at[pad_start_ref[e] + r], zsem)

        def block_copy(b):
            return pltpu.make_async_copy(zrow, xs_hbm.at[pl.ds(b * bm, bm)], zsem)
        for e in range(N_EXPERTS):
            lax.fori_loop(0, pad_count_ref[e], lambda r, c, e=e: (pad_copy(e, r).start(), c)[1], 0)
        lax.fori_loop(nused_ref[0], nb, lambda b, c: (block_copy(b).start(), c)[1], 0)
        for e in range(N_EXPERTS):
            lax.fori_loop(0, pad_count_ref[e], lambda r, c, e=e: (pad_copy(e, r).wait(), c)[1], 0)
        lax.fori_loop(nused_ref[0], nb, lambda b, c: (block_copy(b).wait(), c)[1], 0)


def _dispatch(h2, dest, pad_start, pad_count, n_used, n_rows, tm, bm):
    t, width = h2.shape
    tile = (width // LANES, LANES)
    nt = t // tm
    dest_t = dest.reshape(nt, tm, TOP_K_INNER).transpose(0, 2, 1)
    grid_spec = pltpu.PrefetchScalarGridSpec(
        num_scalar_prefetch=3,
        grid=(nt,),
        in_specs=[pl.BlockSpec((None, TOP_K_INNER, tm), lambda i, *_: (i, 0, 0), memory_space=pltpu.SMEM),
                  pl.BlockSpec((tm, width), lambda i, *_: (i, 0))],
        out_specs=pl.BlockSpec(memory_space=pl.ANY),
        scratch_shapes=[pltpu.VMEM((2, tm) + tile, h2.dtype),
                        pltpu.VMEM((bm,) + tile, h2.dtype),
                        pltpu.SemaphoreType.DMA((2,)),
                        pltpu.SemaphoreType.DMA(())],
    )
    return pl.pallas_call(
        _dispatch_kernel,
        grid_spec=grid_spec,
        out_shape=jax.ShapeDtypeStruct((n_rows,) + tile, h2.dtype),
        compiler_params=_cparams(("arbitrary",)),
        name="moe_dispatch",
    )(pad_start, pad_count, n_used, dest_t, h2)


def _experts_kernel(bexp_ref, nused_ref, first_ref, next_ref, slot_ref, xs_ref, wg_hbm, wu_hbm, wd_hbm,
                    o_ref, wg_f, wu_f, wd_f, wg_s, wu_s, wd_s, sem):
    b = pl.program_id(0)
    n_used = nused_ref[0]

    def weight_copies(e, s):
        return (pltpu.make_async_copy(wg_hbm.at[e], wg_f.at[s], sem.at[s, 0]),
                pltpu.make_async_copy(wu_hbm.at[e], wu_f.at[s], sem.at[s, 1]),
                pltpu.make_async_copy(wd_hbm.at[e], wd_f.at[s], sem.at[s, 2]))

    @pl.when((b == 0) & (n_used > 0))
    def _():
        for c in weight_copies(bexp_ref[0], 0):
            c.start()

    @pl.when((b < n_used) & (first_ref[b] == 1))
    def _():
        s = slot_ref[b]
        for c in weight_copies(bexp_ref[b], s):
            c.wait()

        @pl.when(next_ref[b] >= 0)
        def _():
            for c in weight_copies(next_ref[b], 1 - s):
                c.start()

        wg_s[...] = wg_f[s].astype(BF16)
        wu_s[...] = wu_f[s].astype(BF16)
        wd_s[...] = wd_f[s].astype(BF16)

    @pl.when(b < n_used)
    def _():
        xb = _unpack_bf16_pairs(_load_token_tiles(xs_ref)).astype(BF16)
        hg = jnp.dot(xb, wg_s[...], preferred_element_type=F32)
        hu = jnp.dot(xb, wu_s[...], preferred_element_type=F32)
        act = (_silu(hg) * hu).astype(BF16)
        _store_token_tiles(o_ref, _pack_bf16_pairs(jnp.dot(act, wd_s[...], preferred_element_type=F32)))

    @pl.when(b >= n_used)
    def _():
        o_ref[...] = jnp.zeros(o_ref.shape, o_ref.dtype)


def _experts(xs, block_expert, n_used, w_g, w_u, w_d, bm):
    n_rows = xs.shape[0]
    tile = xs.shape[1:]
    d, ff = w_g.shape[1], w_g.shape[2]
    assert d == 2 * tile[0] * tile[1]
    nb = n_rows // bm
    blk = jnp.arange(nb, dtype=jnp.int32)
    valid = blk < n_used[0]
    first = (valid & ((blk == 0) | (block_expert != jnp.roll(block_expert, 1)))).astype(jnp.int32)
    later = valid[None, :] & (block_expert[None, :] > block_expert[:, None])
    nxt = jnp.min(jnp.where(later, block_expert[None, :], N_EXPERTS), axis=1)
    nxt = jnp.where(nxt == N_EXPERTS, -1, nxt).astype(jnp.int32)
    slot = ((jnp.cumsum(first) - 1) % 2).astype(jnp.int32)

    def used(b, nu):
        return jnp.minimum(b, jnp.maximum(nu[0] - 1, 0))

    grid_spec = pltpu.PrefetchScalarGridSpec(
        num_scalar_prefetch=5,
        grid=(nb,),
        in_specs=[pl.BlockSpec((bm,) + tile, lambda b, be, nu, *_: (used(b, nu), 0, 0)),
                  pl.BlockSpec(memory_space=pl.ANY),
                  pl.BlockSpec(memory_space=pl.ANY),
                  pl.BlockSpec(memory_space=pl.ANY)],
        out_specs=pl.BlockSpec((bm,) + tile, lambda b, *_: (b, 0, 0)),
        scratch_shapes=[pltpu.VMEM((2, d, ff), w_g.dtype),
                        pltpu.VMEM((2, d, ff), w_u.dtype),
                        pltpu.VMEM((2, ff, d), w_d.dtype),
                        pltpu.VMEM((d, ff), BF16),
                        pltpu.VMEM((d, ff), BF16),
                        pltpu.VMEM((ff, d), BF16),
                        pltpu.SemaphoreType.DMA((2, 3))],
    )
    return pl.pallas_call(
        _experts_kernel,
        grid_spec=grid_spec,
        out_shape=jax.ShapeDtypeStruct((n_rows,) + tile, xs.dtype),
        compiler_params=_cparams(("arbitrary",)),
        name="moe_experts",
    )(block_expert, n_used, first, nxt, slot, xs, w_g, w_u, w_d)


def _combine_kernel(pos_ref, x1_ref, rt_ref, nw_ref, y_hbm, o_ref, ybuf, sem, *, normalize):
    i = pl.program_id(0)
    n = pl.num_programs(0)
    tm = x1_ref.shape[0]

    def start_tile(slot, which):
        for r in range(tm):
            for k in range(TOP_K_INNER):
                pltpu.make_async_copy(y_hbm.at[pos_ref[which, k, r]], ybuf.at[slot, k, r],
                                      sem.at[slot]).start()

    def wait_tile(slot):
        for k in range(TOP_K_INNER):
            pltpu.make_async_copy(y_hbm.at[pl.ds(0, tm)], ybuf.at[slot, k], sem.at[slot]).wait()

    slot = i % 2

    @pl.when(i == 0)
    def _():
        start_tile(0, 0)

    wait_tile(slot)
    start_tile(1 - slot, 1)
    acc = x1_ref[...]
    for k in range(TOP_K_INNER):
        y_k = _unpack_bf16_pairs(_load_token_tiles(ybuf.at[slot, k]))
        acc = acc + rt_ref[:, ROUTE_W_LANE + k:ROUTE_W_LANE + k + 1] * y_k
    o_ref[...] = _rms(acc, nw_ref[...]) if normalize else acc

    @pl.when(i == n - 1)
    def _():
        wait_tile(1 - slot)


def _combine(x1, y_rows, pos, route, norm_w, tm, normalize):
    t, d = x1.shape
    nt = t // tm
    pos_t = pos.reshape(nt, tm, TOP_K_INNER).transpose(0, 2, 1)
    pos_next = jnp.concatenate([pos_t[1:], pos_t[-1:]], axis=0)
    pos2 = jnp.stack([pos_t, pos_next], axis=1)
    return pl.pallas_call(
        functools.partial(_combine_kernel, normalize=normalize),
        grid=(nt,),
        in_specs=[pl.BlockSpec((None, 2, TOP_K_INNER, tm), lambda i: (i, 0, 0, 0), memory_space=pltpu.SMEM),
                  pl.BlockSpec((tm, d), lambda i: (i, 0)),
                  pl.BlockSpec((tm, LANES), lambda i: (i, 0)),
                  pl.BlockSpec((1, d), lambda i: (0, 0)),
                  pl.BlockSpec(memory_space=pl.ANY)],
        out_specs=pl.BlockSpec((tm, d), lambda i: (i, 0)),
        out_shape=jax.ShapeDtypeStruct((t, d), F32),
        scratch_shapes=[pltpu.VMEM((2, TOP_K_INNER, tm) + y_rows.shape[1:], y_rows.dtype),
                        pltpu.SemaphoreType.DMA((2,))],
        compiler_params=_cparams(("arbitrary",)),
        name="moe_combine",
    )(pos2, x1, route, norm_w.reshape(1, d), y_rows)


def _layer(x, p, w_in_t, layer):
    t, d = x.shape
    inner = p["w_a_up"].shape[0]
    n_heads = p["a_log"].shape[0]
    s5_width = p["w_glu"].shape[0]
    xbc_dim = inner + 2 * SSD_GROUPS * SSD_STATE
    sizes = (inner, xbc_dim, n_heads, s5_width, 2 * d)
    offs = [0]
    for s in sizes:
        offs.append(offs[-1] + s)
    assert n_heads <= LANES and offs[2] + LANES <= offs[5]
    dt_b = jnp.zeros((1, LANES), F32).at[0, :n_heads].set(p["dt_bias"].astype(F32))

    tm = min(ROW_TILE, t)
    tmm = min(MM_ROW_TILE, t)

    h, dt, dtt = _norm_dt(x, p["norm_mix_w"], w_in_t, layer, offs[2], dt_b, tm)
    zs = _proj(h, w_in_t, layer, offs[0], inner, None, "silu", tmm, MM_COL_TILE)
    xbc = _conv_proj(h, w_in_t, layer, offs[1], xbc_dim, p["conv_w"], p["conv_b"], tmm, MM_COL_TILE)
    u_steps = _chunk_proj(h, w_in_t, layer, offs[3], s5_width)
    gates = _proj(h, w_in_t, layer, offs[4], 2 * d, p["gate_b"], "sigmoid_bias", tmm, MM_COL_TILE)

    y = _ssd(xbc, dt, dtt[:n_heads], p["a_log"].astype(F32), p["d_ssd"].astype(F32), n_heads,
             min(SSD_CHUNK, t))
    ya = _gated_up(y, zs, p["norm_ssd_w"], p["w_a_up"].astype(BF16), tm)

    uc, wc, bbt, a_chunk = _s5_operators(p["s5_lambda_re"], p["s5_lambda_im"], p["s5_log_dt"],
                                         p["s5_b_re"], p["s5_b_im"], p["s5_c_re"], p["s5_c_im"])
    y5 = _s5(u_steps, uc, wc, bbt, a_chunk, p["s5_d"].astype(F32), min(S5_ROWS, t // S5_CHUNK))
    yb = _glu_up(y5, p["w_glu"].astype(BF16), p["w_b_up"].astype(BF16))

    w_router = jnp.zeros((d, LANES), F32)
    w_router = w_router.at[:, :N_EXPERT_GROUPS].set(p["w_route_group"].astype(F32))
    w_router = w_router.at[:, ROUTE_EXPERT_LANE0:ROUTE_EXPERT_LANE0 + N_EXPERTS].set(
        p["w_route_expert"].astype(F32))
    r_b = jnp.zeros((1, LANES), F32)
    r_b = r_b.at[0, :N_EXPERT_GROUPS].set(p["b_route_group"].astype(F32))
    r_b = r_b.at[0, ROUTE_EXPERT_LANE0:ROUTE_EXPERT_LANE0 + N_EXPERTS].set(p["b_route_expert"].astype(F32))
    x1, h2, route = _merge_out(gates, ya, yb, x, p["w_out"].astype(BF16), p["norm_ffn_w"],
                               w_router.astype(BF16), r_b, tm)

    expert_ids = route[:, ROUTE_ID_LANE:ROUTE_ID_LANE + TOP_K_INNER].astype(jnp.int32)
    bm = MOE_BLOCK
    tg = min(GATHER_TILE, t)
    pos, block_expert, n_used, pad_start, pad_count, n_rows = _dispatch_plan(expert_ids, bm)
    xs = _dispatch(h2, pos, pad_start, pad_count, n_used, n_rows, tg, bm)
    y_rows = _experts(xs, block_expert, n_used, p["w_exp_gate"], p["w_exp_up"], p["w_exp_down"], bm)
    return x1, y_rows, pos, route


def kernel(x, norm_mix_w, w_in, conv_w, conv_b, dt_bias, a_log, d_ssd, norm_ssd_w, w_a_up,
           s5_lambda_re, s5_lambda_im, s5_log_dt, s5_b_re, s5_b_im, s5_c_re, s5_c_im, s5_d,
           w_glu, w_b_up, gate_b, w_out, norm_ffn_w, w_route_group, b_route_group,
           w_route_expert, b_route_expert, w_exp_gate, w_exp_up, w_exp_down, norm_final_w):
    b, seq, d = x.shape
    assert b == 1, "the scans carry state along the flattened token axis"
    depth = w_in.shape[0]
    per_layer = dict(norm_mix_w=norm_mix_w, conv_w=conv_w, conv_b=conv_b, dt_bias=dt_bias,
                     a_log=a_log, d_ssd=d_ssd, norm_ssd_w=norm_ssd_w, w_a_up=w_a_up,
                     s5_lambda_re=s5_lambda_re, s5_lambda_im=s5_lambda_im, s5_log_dt=s5_log_dt,
                     s5_b_re=s5_b_re, s5_b_im=s5_b_im, s5_c_re=s5_c_re, s5_c_im=s5_c_im, s5_d=s5_d,
                     w_glu=w_glu, w_b_up=w_b_up, gate_b=gate_b, w_out=w_out, norm_ffn_w=norm_ffn_w,
                     w_route_group=w_route_group, b_route_group=b_route_group,
                     w_route_expert=w_route_expert, b_route_expert=b_route_expert,
                     w_exp_gate=w_exp_gate, w_exp_up=w_exp_up, w_exp_down=w_exp_down)
    xt = x.reshape(b * seq, d)
    tg = min(GATHER_TILE, b * seq)
    w_in_t = jnp.swapaxes(w_in.astype(F32), 1, 2)
    for i in range(depth):
        p = {k: v[i] for k, v in per_layer.items()}
        x1, y_rows, pos, route = _layer(xt, p, w_in_t, i)
        xt = _combine(x1, y_rows, pos, route, norm_final_w, tg, normalize=(i == depth - 1))
    return xt.reshape(b, seq, d)
```
